```python
import jax
import jax.numpy as jnp
from jax import lax
import numpy as np

D_MODEL = 1024
BATCH = 4
SEQ = 4096
DEPTH = 4
DEC_BATCH = 32
DEC_SEQ = 8
PAST_LEN = 8192
PAGE_SIZE = 128

N_A_LAYERS = DEPTH // 2
N_B_LAYERS = DEPTH - N_A_LAYERS
RET_HEADS = 4
RET_DK = D_MODEL // RET_HEADS
RET_DV = 2 * D_MODEL // RET_HEADS
RET_CHUNK = 128
NSA_HEADS = 16
NSA_GROUPS = 4
HEADS_PER_GROUP = NSA_HEADS // NSA_GROUPS
HEAD_DIM = D_MODEL // NSA_HEADS
CMP_BLOCK = 32
CMP_STRIDE = 16
CMP_HIDDEN = HEAD_DIM
SEL_BLOCK = 64
N_SELECT = 16
N_LOCAL = 2
WINDOW = 512
Q_BLOCK = 64
D_FF = 2816
ROPE_THETA = 10000.0
EPS = 1e-6
NEG = -1e30
BIG = 1e9
N_ROWS = 4

kernel_name = 'yoco_retnet_nsa_macaron_step'


def rmsnorm(x, g):
    x32 = x.astype(jnp.float32)
    y = x32 * lax.rsqrt(jnp.mean(x32 * x32, axis=-1, keepdims=True) + EPS)
    return (y * g.astype(jnp.float32)).astype(x.dtype)


def rope(x, pos):
    half = x.shape[-1] // 2
    inv = ROPE_THETA ** (-jnp.arange(half, dtype=jnp.float32) / half)
    ang = pos.astype(jnp.float32)[:, None] * inv[None, :]
    cos = jnp.cos(ang)[:, None, :]
    sin = jnp.sin(ang)[:, None, :]
    x32 = x.astype(jnp.float32)
    x1, x2 = x32[..., :half], x32[..., half:]
    return jnp.concatenate([x1 * cos - x2 * sin, x1 * sin + x2 * cos], axis=-1).astype(x.dtype)


def swiglu(xn, w_in, w_out):
    a, b = jnp.split(xn @ w_in, 2, axis=-1)
    return (jax.nn.silu(a) * b) @ w_out


def retention_log_decay():
    return jnp.log(1.0 - 2.0 ** (-5.0 - jnp.arange(RET_HEADS, dtype=jnp.float32)))


def retention_chunk(q, k, v, s):
    f32 = jnp.float32
    c = q.shape[1]
    lg = retention_log_decay()
    idx = jnp.arange(c, dtype=f32)
    rel = idx[:, None] - idx[None, :]
    decay = jnp.where(rel >= 0, jnp.exp(jnp.maximum(rel, 0.0)[None] * lg[:, None, None]), 0.0)
    scores = jnp.einsum('bihd,bjhd->bhij', q, k, preferred_element_type=f32) * decay
    intra = jnp.einsum('bhij,bjhv->bihv', scores, v.astype(f32))
    q_dec = jnp.exp((idx + 1.0)[:, None] * lg[None, :])
    cross = jnp.einsum('bihd,bhdv->bihv', q.astype(f32) * q_dec[None, :, :, None], s)
    k_dec = jnp.exp((c - 1.0 - idx)[:, None] * lg[None, :])
    s_new = jnp.exp(c * lg)[None, :, None, None] * s + jnp.einsum(
        'bjhd,bjhv->bhdv', k.astype(f32) * k_dec[None, :, :, None], v.astype(f32))
    return intra + cross, s_new


def retention(xn, w_in, w_out, s0, pos):
    b, t, _ = xn.shape
    q, k, v, g = jnp.split(xn @ w_in, [D_MODEL, 2 * D_MODEL, 4 * D_MODEL], axis=-1)
    q = rope(q.reshape(b, t, RET_HEADS, RET_DK), pos)
    k = rope(k.reshape(b, t, RET_HEADS, RET_DK), pos) * (RET_DK ** -0.5)
    v = v.reshape(b, t, RET_HEADS, RET_DV)
    chunk = RET_CHUNK if t % RET_CHUNK == 0 else t
    n = t // chunk

    def to_chunks(a):
        return a.reshape(b, n, chunk, *a.shape[2:]).swapaxes(0, 1)

    def step(s, qkv):
        o, s_next = retention_chunk(qkv[0], qkv[1], qkv[2], s)
        return s_next, o

    s_fin, o = lax.scan(step, s0.astype(jnp.float32), (to_chunks(q), to_chunks(k), to_chunks(v)))
    o = o.swapaxes(0, 1).reshape(b, t, RET_HEADS, RET_DV)
    mu = jnp.mean(o, axis=-1, keepdims=True)
    var = jnp.mean(jnp.square(o - mu), axis=-1, keepdims=True)
    o = ((o - mu) * lax.rsqrt(var + EPS)).reshape(b, t, 2 * D_MODEL).astype(xn.dtype)
    return (jax.nn.silu(g) * o) @ w_out, s_fin


def shared_kv_rows(h, kv_norm, kv_w, k_norm, pos):
    b, t, _ = h.shape
    kv = (rmsnorm(h, kv_norm) @ kv_w).reshape(b, t, 6, NSA_GROUPS, HEAD_DIM)
    k_slc = rope(rmsnorm(kv[:, :, 2], k_norm[1]), pos)
    k_win = rope(rmsnorm(kv[:, :, 4], k_norm[2]), pos)
    rows = jnp.stack([kv[:, :, 0], kv[:, :, 1], k_slc, kv[:, :, 3]], axis=2)
    win = jnp.stack([k_win, kv[:, :, 5]], axis=2)
    return rows, win


def compress_blocks(tok, w1, b1, w2, pe):
    b, l = tok.shape[:2]
    nc = (l - CMP_BLOCK) // CMP_STRIDE + 1
    r = CMP_BLOCK // CMP_STRIDE
    nchunk = nc + r - 1
    chunks = tok[:, :nchunk * CMP_STRIDE].reshape(b, nchunk, CMP_STRIDE, NSA_GROUPS, HEAD_DIM)
    chunks = chunks.transpose(0, 1, 3, 2, 4).reshape(b, nchunk, NSA_GROUPS, CMP_STRIDE * HEAD_DIM)
    w1r = w1.reshape(r, CMP_STRIDE * HEAD_DIM, CMP_HIDDEN)
    h = b1 + pe.reshape(-1) @ w1
    for j in range(r):
        h = h + chunks[:, j:j + nc] @ w1r[j]
    return jax.nn.gelu(h) @ w2


def build_context(rows, k_norm, cmp_w1, cmp_b1, cmp_w2, cmp_pe):
    b, l = rows.shape[:2]
    k_c = compress_blocks(rows[:, :, 0], cmp_w1[0], cmp_b1[0], cmp_w2[0], cmp_pe[0])
    v_c = compress_blocks(rows[:, :, 1], cmp_w1[1], cmp_b1[1], cmp_w2[1], cmp_pe[1])
    nc = k_c.shape[1]
    c_end = jnp.arange(nc, dtype=jnp.int32) * CMP_STRIDE + (CMP_BLOCK - 1)
    k_c = rope(rmsnorm(k_c, k_norm[0]), c_end)
    ns = -(-l // SEL_BLOCK)
    sel = jnp.pad(rows[:, :, 2:4], ((0, 0), (0, ns * SEL_BLOCK - l), (0, 0), (0, 0), (0, 0)))
    sel = sel.reshape(b, ns, SEL_BLOCK, 2, NSA_GROUPS, HEAD_DIM).transpose(3, 0, 4, 1, 2, 5)
    ci = np.arange(nc)[:, None]
    sj = np.arange(ns)[None, :]
    overlap = (ci * CMP_STRIDE < (sj + 1) * SEL_BLOCK) & (ci * CMP_STRIDE + CMP_BLOCK > sj * SEL_BLOCK)
    return (k_c, v_c, c_end, jnp.asarray(overlap, jnp.float32), sel[0], sel[1])


def nsa_attend(q, gates, qpos, ctx, k_w, v_w, kpos_w):
    k_c, v_c, c_end, overlap, k_blk, v_blk = ctx
    f32 = jnp.float32
    b, nq = q.shape[:2]
    scale = HEAD_DIM ** -0.5
    qg = q.reshape(b, nq, NSA_GROUPS, HEADS_PER_GROUP, HEAD_DIM)
    s_c = jnp.einsum('bqghd,bcgd->bqghc', qg, k_c, preferred_element_type=f32) * scale
    valid_c = (c_end[None, :] <= qpos[:, None])[None, :, None, None, :]
    p_c = jax.nn.softmax(jnp.where(valid_c, s_c, NEG), axis=-1) * valid_c
    o_c = jnp.einsum('bqghc,bcgd->bqghd', p_c, v_c.astype(f32))
    ns = k_blk.shape[2]
    imp = jnp.einsum('bqghc,cs->bqgs', p_c, overlap)
    q_blk = (qpos // SEL_BLOCK)[:, None]
    s_idx = jnp.arange(ns)[None, :]
    valid_s = s_idx <= q_blk
    forced = (s_idx == 0) | (valid_s & (q_blk - s_idx < N_LOCAL))
    score = jnp.where(forced[None, :, None, :], BIG, jnp.where(valid_s[None, :, None, :], imp, NEG))
    _, idx = lax.top_k(score, min(N_SELECT, ns))
    idx = idx.transpose(0, 2, 1, 3)
    nsel = idx.shape[-1]
    take = jax.vmap(jax.vmap(lambda blocks, ii: blocks[ii]))
    k_s = take(k_blk, idx)
    v_s = take(v_blk, idx)
    kpos_s = idx[..., None] * SEL_BLOCK + jnp.arange(SEL_BLOCK)
    valid_k = (kpos_s <= qpos[None, None, :, None, None]).transpose(0, 2, 1, 3, 4)
    valid_k = valid_k.reshape(b, nq, NSA_GROUPS, 1, nsel * SEL_BLOCK)
    s_s = jnp.einsum('bqghd,bgqskd->bqghsk', qg, k_s, preferred_element_type=f32)
    s_s = s_s.reshape(b, nq, NSA_GROUPS, HEADS_PER_GROUP, nsel * SEL_BLOCK) * scale
    p_s = jax.nn.softmax(jnp.where(valid_k, s_s, NEG), axis=-1)
    o_s = jnp.einsum('bqghn,bgqnd->bqghd', p_s,
                     v_s.reshape(b, NSA_GROUPS, nq, nsel * SEL_BLOCK, HEAD_DIM).astype(f32))
    s_w = jnp.einsum('bqghd,bkgd->bqghk', qg, k_w, preferred_element_type=f32) * scale
    valid_w = ((kpos_w[None, :] <= qpos[:, None]) & (kpos_w[None, :] > qpos[:, None] - WINDOW)
               & (kpos_w[None, :] >= 0))[None, :, None, None, :]
    p_w = jax.nn.softmax(jnp.where(valid_w, s_w, NEG), axis=-1)
    o_w = jnp.einsum('bqghk,bkgd->bqghd', p_w, v_w.astype(f32))
    g = gates.reshape(b, nq, NSA_GROUPS, HEADS_PER_GROUP, 3)
    o = g[..., 0:1] * o_c + g[..., 1:2] * o_s + g[..., 2:3] * o_w
    return o.reshape(b, nq, NSA_HEADS * HEAD_DIM)


def nsa_mixer(xn, w_in, q_gain, w_out, pos, ctx, win_kv, win_pos, banded):
    b, t, _ = xn.shape
    proj = xn @ w_in
    q = rope(rmsnorm(proj[..., :NSA_HEADS * HEAD_DIM].reshape(b, t, NSA_HEADS, HEAD_DIM), q_gain), pos)
    gates = jax.nn.sigmoid(proj[..., NSA_HEADS * HEAD_DIM:].astype(jnp.float32))
    if banded:
        span = WINDOW + Q_BLOCK

        def block(i):
            start = i * Q_BLOCK
            kv = lax.dynamic_slice_in_dim(win_kv, start, span, axis=1)
            return nsa_attend(lax.dynamic_slice_in_dim(q, start, Q_BLOCK, axis=1),
                              lax.dynamic_slice_in_dim(gates, start, Q_BLOCK, axis=1),
                              start + jnp.arange(Q_BLOCK, dtype=jnp.int32), ctx,
                              kv[:, :, 0], kv[:, :, 1],
                              start - WINDOW + jnp.arange(span, dtype=jnp.int32))

        o = lax.map(block, jnp.arange(t // Q_BLOCK, dtype=jnp.int32))
        o = o.swapaxes(0, 1).reshape(b, t, NSA_HEADS * HEAD_DIM)
    else:
        o = nsa_attend(q, gates, pos, ctx, win_kv[:, :, 0], win_kv[:, :, 1], win_pos)
    return o.astype(xn.dtype) @ w_out


def trunk(x, pos, ret_s0, past_rows, past_win, w):
    (ffn_norm, ffn_w_in, ffn_w_out, ret_norm, ret_w_in, ret_w_out, kv_norm, kv_w, k_norm,
     cmp_w1, cmp_b1, cmp_w2, cmp_pe, nsa_norm, nsa_w_in, q_norm, nsa_w_out) = w
    t = x.shape[1]
    prompt = past_rows is None
    ret_states = []
    rows = win = ctx = win_kv = win_pos = None
    for layer in range(DEPTH):
        if layer == N_A_LAYERS:
            rows, win = shared_kv_rows(x, kv_norm, kv_w, k_norm, pos)
            if prompt:
                all_rows = rows
                win_kv = jnp.pad(win, ((0, 0), (WINDOW, 0), (0, 0), (0, 0), (0, 0)))
            else:
                all_rows = jnp.concatenate([past_rows.astype(rows.dtype), rows], axis=1)
                win_kv = jnp.concatenate([past_win.astype(win.dtype), win], axis=1)
                win_pos = pos[0] - past_win.shape[1] + jnp.arange(win_kv.shape[1], dtype=jnp.int32)
            ctx = build_context(all_rows, k_norm, cmp_w1, cmp_b1, cmp_w2, cmp_pe)
        x = x + 0.5 * swiglu(rmsnorm(x, ffn_norm[layer, 0]), ffn_w_in[layer, 0], ffn_w_out[layer, 0])
        if layer < N_A_LAYERS:
            mix, s = retention(rmsnorm(x, ret_norm[layer]), ret_w_in[layer], ret_w_out[layer], ret_s0[layer], pos)
            ret_states.append(s)
        else:
            j = layer - N_A_LAYERS
            mix = nsa_mixer(rmsnorm(x, nsa_norm[j]), nsa_w_in[j], q_norm[j], nsa_w_out[j],
                            pos, ctx, win_kv, win_pos, prompt)
        x = x + mix
        x = x + 0.5 * swiglu(rmsnorm(x, ffn_norm[layer, 1]), ffn_w_in[layer, 1], ffn_w_out[layer, 1])
    new_win = win[:, -min(WINDOW, t):] if prompt else win_kv[:, -past_win.shape[1]:]
    return x, jnp.stack(ret_states), rows, new_win


def setup_inputs(seed: int = 0) -> dict:
    key = jax.random.key(seed)
    ks = jax.random.split(key, 24)
    f32 = jnp.float32
    n_pages = PAST_LEN // PAGE_SIZE
    n_phys = (DEC_BATCH * n_pages * 5) // 4
    win_buf = min(WINDOW, PAST_LEN)

    def dense(k, shape, fan_in):
        return jax.random.normal(k, shape, f32) * fan_in ** -0.5

    def gain(k, shape):
        return 1.0 + 0.02 * jax.random.normal(k, shape, f32)

    page_table = jax.random.permutation(ks[5], n_phys)[:DEC_BATCH * n_pages]
    page_table = page_table.reshape(DEC_BATCH, n_pages).astype(jnp.int32)
    return {
        'x_prompt': jax.random.normal(ks[0], (BATCH, SEQ, D_MODEL), f32),
        'x_sample': jax.random.normal(ks[1], (DEC_BATCH, DEC_SEQ, D_MODEL), f32),
        'state_ret': jax.random.normal(ks[2], (N_A_LAYERS, DEC_BATCH, RET_HEADS, RET_DK, RET_DV), f32),
        'cache_kv': jax.random.normal(ks[3], (n_phys, PAGE_SIZE, N_ROWS, NSA_GROUPS, HEAD_DIM), f32),
        'cache_win': jax.random.normal(ks[4], (DEC_BATCH, win_buf, 2, NSA_GROUPS, HEAD_DIM), f32),
        'page_table': page_table,
        'ffn_norm': gain(ks[6], (DEPTH, 2, D_MODEL)),
        'ffn_w_in': dense(ks[7], (DEPTH, 2, D_MODEL, 2 * D_FF), D_MODEL),
        'ffn_w_out': dense(ks[8], (DEPTH, 2, D_FF, D_MODEL), D_FF),
        'ret_norm': gain(ks[9], (N_A_LAYERS, D_MODEL)),
        'ret_w_in': dense(ks[10], (N_A_LAYERS, D_MODEL, 6 * D_MODEL), D_MODEL),
        'ret_w_out': dense(ks[11], (N_A_LAYERS, 2 * D_MODEL, D_MODEL), 2 * D_MODEL),
        'kv_norm': gain(ks[12], (D_MODEL,)),
        'kv_w': dense(ks[13], (D_MODEL, 6 * NSA_GROUPS * HEAD_DIM), D_MODEL),
        'k_norm': gain(ks[14], (3, HEAD_DIM)),
        'cmp_w1': dense(ks[15], (2, CMP_BLOCK * HEAD_DIM, CMP_HIDDEN), CMP_BLOCK * HEAD_DIM),
        'cmp_b1': 0.02 * jax.random.normal(ks[16], (2, CMP_HIDDEN), f32),
        'cmp_w2': dense(ks[17], (2, CMP_HIDDEN, HEAD_DIM), CMP_HIDDEN),
        'cmp_pe': 0.1 * jax.random.normal(ks[18], (2, CMP_BLOCK, HEAD_DIM), f32),
        'nsa_norm': gain(ks[19], (N_B_LAYERS, D_MODEL)),
        'nsa_w_in': dense(ks[20], (N_B_LAYERS, D_MODEL, NSA_HEADS * HEAD_DIM + 3 * NSA_HEADS), D_MODEL),
        'q_norm': gain(ks[21], (N_B_LAYERS, HEAD_DIM)),
        'nsa_w_out': dense(ks[22], (N_B_LAYERS, NSA_HEADS * HEAD_DIM, D_MODEL), NSA_HEADS * HEAD_DIM),
    }


def reference(x_prompt, x_sample, state_ret, cache_kv, cache_win, page_table,
              ffn_norm, ffn_w_in, ffn_w_out, ret_norm, ret_w_in, ret_w_out,
              kv_norm, kv_w, k_norm, cmp_w1, cmp_b1, cmp_w2, cmp_pe,
              nsa_norm, nsa_w_in, q_norm, nsa_w_out):
    w = (ffn_norm, ffn_w_in, ffn_w_out, ret_norm, ret_w_in, ret_w_out, kv_norm, kv_w, k_norm,
         cmp_w1, cmp_b1, cmp_w2, cmp_pe, nsa_norm, nsa_w_in, q_norm, nsa_w_out)
    b, t, _ = x_prompt.shape
    db, dq, _ = x_sample.shape
    past_len = page_table.shape[1] * cache_kv.shape[1]
    pos_p = jnp.arange(t, dtype=jnp.int32)
    pos_s = past_len + jnp.arange(dq, dtype=jnp.int32)
    s0_p = jnp.zeros((N_A_LAYERS, b, RET_HEADS, RET_DK, RET_DV), jnp.float32)
    y_prompt, ret_p, rows_p, win_p = trunk(x_prompt, pos_p, s0_p, None, None, w)
    past_rows = cache_kv[page_table].reshape(db, past_len, *cache_kv.shape[2:])
    y_sample, ret_s, rows_s, win_s = trunk(x_sample, pos_s, state_ret, past_rows, cache_win, w)
    return (y_prompt, y_sample, ret_p.astype(state_ret.dtype), ret_s.astype(state_ret.dtype),
            rows_p, rows_s, win_p, win_s)
```

```python
import functools

import jax
import jax.numpy as jnp
import numpy as np
from jax import lax
from jax.experimental import pallas as pl
from jax.experimental.pallas import tpu as pltpu

F32 = jnp.float32
BF16 = jnp.bfloat16
I32 = jnp.int32

D_MODEL = 1024
DEPTH = 4
N_A_LAYERS = DEPTH // 2
RET_HEADS = 4
RET_DK = D_MODEL // RET_HEADS
RET_DV = 2 * D_MODEL // RET_HEADS
RET_CHUNK = 128
NSA_HEADS = 16
NSA_GROUPS = 4
HEADS_PER_GROUP = NSA_HEADS // NSA_GROUPS
HEAD_DIM = D_MODEL // NSA_HEADS
CMP_BLOCK = 32
CMP_STRIDE = 16
SEL_BLOCK = 64
N_SELECT = 16
N_LOCAL = 2
WINDOW = 512
D_FF = 2816
ROPE_THETA = 10000.0
EPS = 1e-6
NEG = -1e30
BIG = 1e9
N_GATES = 3 * HEADS_PER_GROUP
GROUP_LANES = NSA_GROUPS * HEAD_DIM
WIN_SPAN = WINDOW + 128

VMEM_LIMIT_BYTES = 56 * 1024 * 1024
LANES = 128

NT_DIMS = (((1,), (1,)), ((), ()))
TN_DIMS = (((0,), (0,)), ((), ()))


def _params(*semantics):
    return pltpu.CompilerParams(dimension_semantics=semantics, vmem_limit_bytes=VMEM_LIMIT_BYTES)


def _rms(x, gain):
    ms = jnp.mean(x * x, axis=-1, keepdims=True)
    return x * lax.rsqrt(ms + EPS) * gain


def _seg_rms(y, ones_bd, gain):
    sq = y * y
    hi = sq.astype(BF16)
    lo = (sq - hi.astype(F32)).astype(BF16)
    ss = jnp.dot(hi, ones_bd, preferred_element_type=F32) + jnp.dot(lo, ones_bd, preferred_element_type=F32)
    return y * lax.rsqrt(ss * (1.0 / HEAD_DIM) + EPS) * gain


def _rope64(x, cos, sin_signed):
    lane = lax.broadcasted_iota(I32, x.shape, 1)
    first_half = (lane % HEAD_DIM) < (HEAD_DIM // 2)
    rot = jnp.where(first_half, pltpu.roll(x, LANES - HEAD_DIM // 2, 1), pltpu.roll(x, HEAD_DIM // 2, 1))
    return x * cos + rot * sin_signed


def _softmax_rows(s, ok):
    s = jnp.where(ok, s, NEG)
    m = jnp.max(s, axis=-1, keepdims=True)
    e = jnp.exp(s - m)
    return e / jnp.sum(e, axis=-1, keepdims=True)


def _ffn_kernel(x_ref, g_ref, wa_ref, wb_ref, wo_ref, o_ref, xn_ref, acc_ref):
    j = pl.program_id(1)

    @pl.when(j == 0)
    def _():
        xn_ref[...] = _rms(x_ref[...], g_ref[...]).astype(BF16)
        acc_ref[...] = jnp.zeros_like(acc_ref)

    xn = xn_ref[...]
    a = jnp.dot(xn, wa_ref[...], preferred_element_type=F32)
    b = jnp.dot(xn, wb_ref[...], preferred_element_type=F32)
    h = (a * jax.nn.sigmoid(a) * b).astype(BF16)
    acc_ref[...] += jnp.dot(h, wo_ref[...], preferred_element_type=F32)

    @pl.when(j == pl.num_programs(1) - 1)
    def _():
        o_ref[...] = x_ref[...] + 0.5 * acc_ref[...]


def ffn_half(x, gain, w_in, w_out):
    n = x.shape[0]
    tm = min(n, 1024)
    tf = 256
    nf = D_FF // tf
    return pl.pallas_call(
        _ffn_kernel,
        grid=(n // tm, nf),
        in_specs=[
            pl.BlockSpec((tm, D_MODEL), lambda i, j: (i, 0)),
            pl.BlockSpec((1, D_MODEL), lambda i, j: (0, 0)),
            pl.BlockSpec((D_MODEL, tf), lambda i, j: (0, j)),
            pl.BlockSpec((D_MODEL, tf), lambda i, j: (0, nf + j)),
            pl.BlockSpec((tf, D_MODEL), lambda i, j: (j, 0)),
        ],
        out_specs=pl.BlockSpec((tm, D_MODEL), lambda i, j: (i, 0)),
        out_shape=jax.ShapeDtypeStruct((n, D_MODEL), F32),
        scratch_shapes=[pltpu.VMEM((tm, D_MODEL), BF16), pltpu.VMEM((tm, D_MODEL), F32)],
        compiler_params=_params("parallel", "arbitrary"),
        name="ffn_half",
    )(x, gain.reshape(1, D_MODEL), w_in, w_in, w_out)


def _mm_res_kernel(a_ref, w_ref, x_ref, o_ref):
    o_ref[...] = x_ref[...] + jnp.dot(a_ref[...], w_ref[...], preferred_element_type=F32)


def mm_residual(a, w, x):
    n, k = a.shape
    tm = min(n, 512)
    return pl.pallas_call(
        _mm_res_kernel,
        grid=(n // tm,),
        in_specs=[
            pl.BlockSpec((tm, k), lambda i: (i, 0)),
            pl.BlockSpec((k, D_MODEL), lambda i: (0, 0)),
            pl.BlockSpec((tm, D_MODEL), lambda i: (i, 0)),
        ],
        out_specs=pl.BlockSpec((tm, D_MODEL), lambda i: (i, 0)),
        out_shape=jax.ShapeDtypeStruct((n, D_MODEL), F32),
        compiler_params=_params("parallel"),
        name="mm_residual",
    )(a, w, x)


RET_TN = 512
RET_QK_TILES = 2 * D_MODEL // RET_TN


def _ret_inproj_kernel(x_ref, g_ref, w_ref, cos_ref, sin_ref, o_ref, xn_ref):
    j = pl.program_id(1)

    @pl.when(j == 0)
    def _():
        xn_ref[...] = _rms(x_ref[...], g_ref[...]).astype(BF16)

    y = jnp.dot(xn_ref[...], w_ref[...], preferred_element_type=F32)

    @pl.when(j < RET_QK_TILES)
    def _():
        scale = jnp.where(j < RET_QK_TILES // 2, 1.0, RET_DK ** -0.5).astype(F32)
        c = cos_ref[...] * scale
        s = sin_ref[...] * scale
        half = RET_DK // 2
        for h in range(RET_TN // RET_DK):
            x1 = y[:, h * RET_DK:h * RET_DK + half]
            x2 = y[:, h * RET_DK + half:(h + 1) * RET_DK]
            o_ref[:, h * RET_DK:h * RET_DK + half] = (x1 * c - x2 * s).astype(BF16)
            o_ref[:, h * RET_DK + half:(h + 1) * RET_DK] = (x1 * s + x2 * c).astype(BF16)

    @pl.when(j >= RET_QK_TILES)
    def _():
        o_ref[...] = y.astype(BF16)


def ret_inproj(x, gain, w, cos, sin):
    n = x.shape[0]
    p = cos.shape[0]
    tm = min(n, 1024, p)
    n_out = w.shape[1]
    tab_blocks = p // tm
    return pl.pallas_call(
        _ret_inproj_kernel,
        grid=(n // tm, n_out // RET_TN),
        in_specs=[
            pl.BlockSpec((tm, D_MODEL), lambda i, j: (i, 0)),
            pl.BlockSpec((1, D_MODEL), lambda i, j: (0, 0)),
            pl.BlockSpec((D_MODEL, RET_TN), lambda i, j: (0, j)),
            pl.BlockSpec((tm, RET_DK // 2), lambda i, j: (i % tab_blocks, 0)),
            pl.BlockSpec((tm, RET_DK // 2), lambda i, j: (i % tab_blocks, 0)),
        ],
        out_specs=pl.BlockSpec((tm, RET_TN), lambda i, j: (i, j)),
        out_shape=jax.ShapeDtypeStruct((n, n_out), BF16),
        scratch_shapes=[pltpu.VMEM((tm, D_MODEL), BF16)],
        compiler_params=_params("parallel", "arbitrary"),
        name="ret_inproj",
    )(x, gain.reshape(1, D_MODEL), w, cos, sin)


def _ret_core_kernel(*refs, chunk, n_inner, has_s0):
    if has_s0:
        (q_ref, k_ref, v_ref, g_ref, dm_ref, qd_ref, kd_ref, sd_ref, s0_ref, o_ref, so_ref, s_scr) = refs
    else:
        (q_ref, k_ref, v_ref, g_ref, dm_ref, qd_ref, kd_ref, sd_ref, o_ref, so_ref, s_scr) = refs
    t = pl.program_id(2)

    @pl.when(t == 0)
    def _():
        if has_s0:
            s_scr[...] = s0_ref[0, 0]
        else:
            s_scr[...] = jnp.zeros_like(s_scr)

    dm = dm_ref[0]
    qd = qd_ref[0]
    kd = kd_ref[0]
    sd = sd_ref[0, 0:1, 0:1]
    for c in range(n_inner):
        rows = slice(c * chunk, (c + 1) * chunk)
        q = q_ref[0, rows, :]
        k = k_ref[0, rows, :]
        v = v_ref[0, rows, :]
        g = g_ref[0, rows, :].astype(F32)
        s = s_scr[...]
        scores = lax.dot_general(q, k, NT_DIMS, preferred_element_type=F32) * dm
        intra = jnp.dot(scores.astype(BF16), v, preferred_element_type=F32)
        cross = jnp.dot((q.astype(F32) * qd).astype(BF16), s.astype(BF16), preferred_element_type=F32)
        o = intra + cross
        kv = lax.dot_general((k.astype(F32) * kd).astype(BF16), v, TN_DIMS, preferred_element_type=F32)
        s_scr[...] = sd * s + kv
        mu = jnp.mean(o, axis=-1, keepdims=True)
        d = o - mu
        var = jnp.mean(d * d, axis=-1, keepdims=True)
        on = d * lax.rsqrt(var + EPS)
        o_ref[0, rows, :] = (g * jax.nn.sigmoid(g) * on).astype(BF16)

    @pl.when(t == pl.num_programs(2) - 1)
    def _():
        so_ref[0, 0] = s_scr[...]


def _decay_tables(chunk):
    lg = jnp.log(1.0 - 2.0 ** (-5.0 - jnp.arange(RET_HEADS, dtype=F32)))
    idx = jnp.arange(chunk, dtype=F32)
    rel = idx[:, None] - idx[None, :]
    dmat = jnp.where(rel >= 0, jnp.exp(jnp.maximum(rel, 0.0)[None] * lg[:, None, None]), 0.0)
    qdec = jnp.exp((idx + 1.0)[None, :] * lg[:, None])
    kdec = jnp.exp((chunk - 1.0 - idx)[None, :] * lg[:, None])
    sdec = jnp.exp(chunk * lg)
    qdec = jnp.broadcast_to(qdec[:, :, None], (RET_HEADS, chunk, RET_DK))
    kdec = jnp.broadcast_to(kdec[:, :, None], (RET_HEADS, chunk, RET_DK))
    sdec = jnp.broadcast_to(sdec[:, None, None], (RET_HEADS, 8, LANES))
    return dmat, qdec, kdec, sdec


def ret_core(qkvg, s0):
    b, t, _ = qkvg.shape
    chunk = RET_CHUNK if t % RET_CHUNK == 0 else t
    tb = min(t, 4 * chunk)
    n_inner = tb // chunk
    dmat, qdec, kdec, sdec = _decay_tables(chunk)
    k_off = D_MODEL // RET_DK
    v_off = 2 * D_MODEL // RET_DV
    g_off = 4 * D_MODEL // RET_DV
    in_specs = [
        pl.BlockSpec((1, tb, RET_DK), lambda bi, h, ti: (bi, ti, h)),
        pl.BlockSpec((1, tb, RET_DK), lambda bi, h, ti: (bi, ti, k_off + h)),
        pl.BlockSpec((1, tb, RET_DV), lambda bi, h, ti: (bi, ti, v_off + h)),
        pl.BlockSpec((1, tb, RET_DV), lambda bi, h, ti: (bi, ti, g_off + h)),
        pl.BlockSpec((1, chunk, chunk), lambda bi, h, ti: (h, 0, 0)),
        pl.BlockSpec((1, chunk, RET_DK), lambda bi, h, ti: (h, 0, 0)),
        pl.BlockSpec((1, chunk, RET_DK), lambda bi, h, ti: (h, 0, 0)),
        pl.BlockSpec((1, 8, LANES), lambda bi, h, ti: (h, 0, 0)),
    ]
    args = [qkvg, qkvg, qkvg, qkvg, dmat, qdec, kdec, sdec]
    if s0 is not None:
        in_specs.append(pl.BlockSpec((1, 1, RET_DK, RET_DV), lambda bi, h, ti: (bi, h, 0, 0)))
        args.append(s0)
    return pl.pallas_call(
        functools.partial(_ret_core_kernel, chunk=chunk, n_inner=n_inner, has_s0=s0 is not None),
        grid=(b, RET_HEADS, t // tb),
        in_specs=in_specs,
        out_specs=[
            pl.BlockSpec((1, tb, RET_DV), lambda bi, h, ti: (bi, ti, h)),
            pl.BlockSpec((1, 1, RET_DK, RET_DV), lambda bi, h, ti: (bi, h, 0, 0)),
        ],
        out_shape=[
            jax.ShapeDtypeStruct((b, t, 2 * D_MODEL), BF16),
            jax.ShapeDtypeStruct((b, RET_HEADS, RET_DK, RET_DV), F32),
        ],
        scratch_shapes=[pltpu.VMEM((RET_DK, RET_DV), F32)],
        compiler_params=_params("parallel", "parallel", "arbitrary"),
        name="ret_core",
    )(*args)


def _kv_rows_kernel(x_ref, g_ref, w_ref, kn_ref, cos_ref, sin_ref, ones_ref, rows_ref, win_ref, *aux_refs):
    xn = _rms(x_ref[0], g_ref[...]).astype(BF16)
    y = jnp.dot(xn, w_ref[...], preferred_element_type=F32)
    cos = cos_ref[...]
    sin = sin_ref[...]
    ones_bd = ones_ref[...]
    slot = lambda s: y[:, s * GROUP_LANES:(s + 1) * GROUP_LANES]

    def norm_rope(v, gain):
        vn = _seg_rms(v, ones_bd, gain)
        return jnp.concatenate([_rope64(vn[:, :LANES], cos, sin), _rope64(vn[:, LANES:], cos, sin)], axis=1)

    k_slc = norm_rope(slot(2), kn_ref[1:2, :])
    k_win = norm_rope(slot(4), kn_ref[2:3, :])
    rows_ref[0, :, 0:2 * GROUP_LANES] = y[:, 0:2 * GROUP_LANES]
    rows_ref[0, :, 2 * GROUP_LANES:3 * GROUP_LANES] = k_slc
    rows_ref[0, :, 3 * GROUP_LANES:4 * GROUP_LANES] = slot(3)
    win_ref[0, :, 0:GROUP_LANES] = k_win
    win_ref[0, :, GROUP_LANES:2 * GROUP_LANES] = slot(5)
    if aux_refs:
        cmp_ref, ksel_ref, vsel_ref, kwin_ref, vwin_ref = aux_refs
        cmp_ref[0] = y[:, 0:2 * GROUP_LANES].astype(BF16)
        for ref, val in ((ksel_ref, k_slc), (vsel_ref, slot(3)), (kwin_ref, k_win), (vwin_ref, slot(5))):
            for g in range(NSA_GROUPS):
                ref[0, g] = val[:, g * HEAD_DIM:(g + 1) * HEAD_DIM].astype(BF16)


def kv_rows(x, gain, w, k_norm_tiled, cos, sin, ones_bd, aux):
    b, t, _ = x.shape
    tm = min(t, 512)
    n_kv = w.shape[1]
    out_specs = [
        pl.BlockSpec((1, tm, 4 * GROUP_LANES), lambda bi, ti: (bi, ti, 0)),
        pl.BlockSpec((1, tm, 2 * GROUP_LANES), lambda bi, ti: (bi, ti, 0)),
    ]
    out_shape = [
        jax.ShapeDtypeStruct((b, t, 4 * GROUP_LANES), F32),
        jax.ShapeDtypeStruct((b, t, 2 * GROUP_LANES), F32),
    ]
    if aux:
        out_specs.append(pl.BlockSpec((1, tm, 2 * GROUP_LANES), lambda bi, ti: (bi, ti, 0)))
        out_shape.append(jax.ShapeDtypeStruct((b, t, 2 * GROUP_LANES), BF16))
        for _ in range(4):
            out_specs.append(pl.BlockSpec((1, NSA_GROUPS, tm, HEAD_DIM), lambda bi, ti: (bi, 0, ti, 0)))
            out_shape.append(jax.ShapeDtypeStruct((b, NSA_GROUPS, t, HEAD_DIM), BF16))
    return pl.pallas_call(
        _kv_rows_kernel,
        grid=(b, t // tm),
        in_specs=[
            pl.BlockSpec((1, tm, D_MODEL), lambda bi, ti: (bi, ti, 0)),
            pl.BlockSpec((1, D_MODEL), lambda bi, ti: (0, 0)),
            pl.BlockSpec((D_MODEL, n_kv), lambda bi, ti: (0, 0)),
            pl.BlockSpec((3, GROUP_LANES), lambda bi, ti: (0, 0)),
            pl.BlockSpec((tm, LANES), lambda bi, ti: (ti, 0)),
            pl.BlockSpec((tm, LANES), lambda bi, ti: (ti, 0)),
            pl.BlockSpec((GROUP_LANES, GROUP_LANES), lambda bi, ti: (0, 0)),
        ],
        out_specs=out_specs,
        out_shape=out_shape,
        compiler_params=_params("parallel", "parallel"),
        name="kv_rows",
    )(x, gain.reshape(1, D_MODEL), w, k_norm_tiled, cos, sin, ones_bd)


def _cmp_partial_kernel(x_ref, wk_ref, wv_ref, o_ref):
    p = pl.program_id(1)

    @pl.when(p == 0)
    def _():
        o_ref[...] = jnp.zeros_like(o_ref)

    x = x_ref[...]
    o_ref[:, 0:2 * GROUP_LANES] += jnp.dot(x[:, 0:GROUP_LANES], wk_ref[0], preferred_element_type=F32)
    o_ref[:, 2 * GROUP_LANES:4 * GROUP_LANES] += jnp.dot(
        x[:, GROUP_LANES:2 * GROUP_LANES], wv_ref[0], preferred_element_type=F32)


def cmp_partial(tok_chunks, wk_bd, wv_bd):
    n = tok_chunks.shape[0]
    tm = min(n, 512)
    return pl.pallas_call(
        _cmp_partial_kernel,
        grid=(n // tm, CMP_STRIDE),
        in_specs=[
            pl.BlockSpec((tm, 2 * GROUP_LANES), lambda i, p: (i, p)),
            pl.BlockSpec((1, GROUP_LANES, 2 * GROUP_LANES), lambda i, p: (p, 0, 0)),
            pl.BlockSpec((1, GROUP_LANES, 2 * GROUP_LANES), lambda i, p: (p, 0, 0)),
        ],
        out_specs=pl.BlockSpec((tm, 4 * GROUP_LANES), lambda i, p: (i, 0)),
        out_shape=jax.ShapeDtypeStruct((n, 4 * GROUP_LANES), F32),
        compiler_params=_params("parallel", "arbitrary"),
        name="cmp_partial",
    )(tok_chunks, wk_bd, wv_bd)


def _cmp_combine_kernel(a_ref, pe_ref, w1_ref, b1_ref, w2_ref, kn_ref, cos_ref, sin_ref, ones_ref, kc_ref, vc_ref):
    a = a_ref[0]
    n = a.shape[0]
    for t, out_ref in enumerate((kc_ref, vc_ref)):
        first = a[:, 2 * t * GROUP_LANES:(2 * t + 1) * GROUP_LANES]
        second = a[:, (2 * t + 1) * GROUP_LANES:(2 * t + 2) * GROUP_LANES]
        second = pltpu.roll(second, n - 1, 0)
        pe_term = jnp.dot(pe_ref[t], w1_ref[t], preferred_element_type=F32)[0:1]
        h = b1_ref[t] + pe_term + first + second
        y = jnp.dot(jax.nn.gelu(h).astype(BF16), w2_ref[t], preferred_element_type=F32)
        if t == 0:
            y = _seg_rms(y, ones_ref[...], kn_ref[0:1, :])
            y = jnp.concatenate(
                [_rope64(y[:, :LANES], cos_ref[...], sin_ref[...]), _rope64(y[:, LANES:], cos_ref[...], sin_ref[...])],
                axis=1)
        for g in range(NSA_GROUPS):
            out_ref[0, g] = y[:, g * HEAD_DIM:(g + 1) * HEAD_DIM].astype(BF16)


def cmp_combine(partial, pe_rows, w1_tiled, b1_tiled, w2_bd, k_norm_tiled, cos, sin, ones_bd):
    b, ncp, _ = partial.shape
    full = lambda *shape: pl.BlockSpec(shape, lambda bi: (0,) * len(shape))
    return pl.pallas_call(
        _cmp_combine_kernel,
        grid=(b,),
        in_specs=[
            pl.BlockSpec((1, ncp, 4 * GROUP_LANES), lambda bi: (bi, 0, 0)),
            full(*pe_rows.shape), full(*w1_tiled.shape), full(*b1_tiled.shape), full(*w2_bd.shape),
            full(3, GROUP_LANES), full(ncp, LANES), full(ncp, LANES), full(GROUP_LANES, GROUP_LANES),
        ],
        out_specs=[pl.BlockSpec((1, NSA_GROUPS, ncp, HEAD_DIM), lambda bi: (bi, 0, 0, 0))] * 2,
        out_shape=[jax.ShapeDtypeStruct((b, NSA_GROUPS, ncp, HEAD_DIM), BF16)] * 2,
        compiler_params=_params("parallel"),
        name="cmp_combine",
    )(partial, pe_rows, w1_tiled, b1_tiled, w2_bd, k_norm_tiled, cos, sin, ones_bd)


def _nsa_q_kernel(x_ref, g_ref, wq_ref, wg_ref, qn_ref, cos_ref, sin_ref, ones_ref, q_ref, gate_ref):
    xn = _rms(x_ref[0], g_ref[...]).astype(BF16)
    y = jnp.dot(xn, wq_ref[...], preferred_element_type=F32)
    cos = cos_ref[...]
    sin = sin_ref[...]
    scale = HEAD_DIM ** -0.5
    for g in range(NSA_GROUPS):
        yg = _seg_rms(y[:, g * GROUP_LANES:(g + 1) * GROUP_LANES], ones_ref[...], qn_ref[...])
        for half in range(2):
            r = _rope64(yg[:, half * LANES:(half + 1) * LANES], cos, sin) * scale
            for hh in range(2):
                q_ref[0, g, 2 * half + hh] = r[:, hh * HEAD_DIM:(hh + 1) * HEAD_DIM]
    gates = jnp.dot(xn, wg_ref[...], preferred_element_type=F32)
    gate_ref[0] = jax.nn.sigmoid(gates)


def nsa_q(x, gain, wq, wg, q_norm_tiled, cos, sin, ones_bd):
    b, t, _ = x.shape
    tm = min(t, 512)
    return pl.pallas_call(
        _nsa_q_kernel,
        grid=(b, t // tm),
        in_specs=[
            pl.BlockSpec((1, tm, D_MODEL), lambda bi, ti: (bi, ti, 0)),
            pl.BlockSpec((1, D_MODEL), lambda bi, ti: (0, 0)),
            pl.BlockSpec(wq.shape, lambda bi, ti: (0, 0)),
            pl.BlockSpec(wg.shape, lambda bi, ti: (0, 0)),
            pl.BlockSpec((1, GROUP_LANES), lambda bi, ti: (0, 0)),
            pl.BlockSpec((tm, LANES), lambda bi, ti: (ti, 0)),
            pl.BlockSpec((tm, LANES), lambda bi, ti: (ti, 0)),
            pl.BlockSpec((GROUP_LANES, GROUP_LANES), lambda bi, ti: (0, 0)),
        ],
        out_specs=[
            pl.BlockSpec((1, NSA_GROUPS, HEADS_PER_GROUP, tm, HEAD_DIM), lambda bi, ti: (bi, 0, 0, ti, 0)),
            pl.BlockSpec((1, tm, NSA_GROUPS * LANES), lambda bi, ti: (bi, ti, 0)),
        ],
        out_shape=[
            jax.ShapeDtypeStruct((b, NSA_GROUPS, HEADS_PER_GROUP, t, HEAD_DIM), F32),
            jax.ShapeDtypeStruct((b, t, NSA_GROUPS * LANES), F32),
        ],
        compiler_params=_params("parallel", "parallel"),
        name="nsa_q",
    )(x, gain.reshape(1, D_MODEL), wq, wg, q_norm_tiled, cos, sin, ones_bd)


def _nsa_attend_kernel(q_ref, gate_ref, kc_ref, vc_ref, ovt_ref, ind_ref, ksel_ref, vsel_ref, kwin_ref, vwin_ref,
                       o_ref, score_scr, *, nq, nc, ns, kc_keys, n_chunks, causal, q_pos0, win_pos0):
    i = pl.program_id(2)
    hpg = HEADS_PER_GROUP
    r = hpg * nq
    ncp = kc_ref.shape[2]
    nsp = ovt_ref.shape[0]
    q_first = q_pos0 + i * nq
    q = q_ref[0, 0].reshape(r, HEAD_DIM).astype(BF16)
    qpos = q_first + lax.broadcasted_iota(I32, (r, 1), 0) % nq

    s_c = lax.dot_general(q, kc_ref[0, 0], NT_DIMS, preferred_element_type=F32)
    cidx = lax.broadcasted_iota(I32, (1, ncp), 1)
    valid_c = (cidx * CMP_STRIDE + (CMP_BLOCK - 1) <= qpos) & (cidx < nc)
    p_c = _softmax_rows(s_c, valid_c) * valid_c.astype(F32)
    o_c = jnp.dot(p_c.astype(BF16), vc_ref[0, 0], preferred_element_type=F32)

    p_sum = p_c[0:nq]
    for hh in range(1, hpg):
        p_sum = p_sum + p_c[hh * nq:(hh + 1) * nq]
    p_hi = p_sum.astype(BF16)
    p_lo = (p_sum - p_hi.astype(F32)).astype(BF16)
    ovt = ovt_ref[...]
    imp_t = (lax.dot_general(ovt, p_hi, NT_DIMS, preferred_element_type=F32)
             + lax.dot_general(ovt, p_lo, NT_DIMS, preferred_element_type=F32))
    sidx = lax.broadcasted_iota(I32, (nsp, 1), 0)
    q_blk = (q_first + lax.broadcasted_iota(I32, (1, nq), 1)) // SEL_BLOCK
    valid_s = (sidx <= q_blk) & (sidx < ns)
    forced = (sidx == 0) | (valid_s & (q_blk - sidx < N_LOCAL))
    score_t = jnp.where(forced, BIG, jnp.where(valid_s, imp_t, NEG))

    score_scr[...] = score_t
    n_live = jnp.minimum((q_first + nq - 1) // SEL_BLOCK + 1, ns)

    def rank_body(sp, rank):
        row = score_scr[pl.ds(sp, 1), :]
        ahead = (row > score_t) | ((row == score_t) & (sidx > sp))
        return rank + ahead.astype(F32)

    rank = lax.fori_loop(0, n_live, rank_body, jnp.zeros((nsp, nq), F32))
    sel_t = ((rank < float(min(N_SELECT, ns))) & valid_s).astype(F32)
    sel = jnp.transpose(sel_t)
    sel_rows = jnp.concatenate([sel] * hpg, axis=0).astype(BF16)

    def chunk_body(c, carry):
        m, l, acc = carry
        start = pl.multiple_of(c * kc_keys, kc_keys)
        kk = ksel_ref[0, 0, pl.ds(start, kc_keys), :]
        vv = vsel_ref[0, 0, pl.ds(start, kc_keys), :]
        s = lax.dot_general(q, kk, NT_DIMS, preferred_element_type=F32)
        picked = jnp.dot(sel_rows, ind_ref[c], preferred_element_type=F32)
        kpos = c * kc_keys + lax.broadcasted_iota(I32, (1, kc_keys), 1)
        ok = (picked > 0.5) & (kpos <= qpos)
        s = jnp.where(ok, s, NEG)
        m_new = jnp.maximum(m, jnp.max(s, axis=-1, keepdims=True))
        alpha = jnp.exp(m - m_new)
        p = jnp.exp(s - m_new)
        l = alpha * l + jnp.sum(p, axis=-1, keepdims=True)
        acc = alpha * acc + jnp.dot(p.astype(BF16), vv, preferred_element_type=F32)
        return m_new, l, acc

    n_used = (q_first + nq + kc_keys - 1) // kc_keys if causal else n_chunks
    init = (jnp.full((r, 1), NEG, F32), jnp.zeros((r, 1), F32), jnp.zeros((r, HEAD_DIM), F32))
    _, l_s, acc_s = lax.fori_loop(0, n_used, chunk_body, init)
    o_s = acc_s / l_s

    lw = kwin_ref.shape[2]
    if lw == WIN_SPAN:
        w0 = 0
    else:
        w0 = pl.multiple_of(jnp.clip(q_first + nq - WIN_SPAN - win_pos0, 0, lw - WIN_SPAN), nq)
    kw = kwin_ref[0, 0, pl.ds(w0, WIN_SPAN), :]
    vw = vwin_ref[0, 0, pl.ds(w0, WIN_SPAN), :]
    s_w = lax.dot_general(q, kw, NT_DIMS, preferred_element_type=F32)
    kpos_w = win_pos0 + w0 + lax.broadcasted_iota(I32, (1, WIN_SPAN), 1)
    valid_w = (kpos_w <= qpos) & (kpos_w > qpos - WINDOW) & (kpos_w >= 0)
    p_w = _softmax_rows(s_w, valid_w)
    o_w = jnp.dot(p_w.astype(BF16), vw, preferred_element_type=F32)

    gates = gate_ref[0]
    for hh in range(hpg):
        rows = slice(hh * nq, (hh + 1) * nq)
        o_h = (gates[:, 3 * hh:3 * hh + 1] * o_c[rows] + gates[:, 3 * hh + 1:3 * hh + 2] * o_s[rows]
               + gates[:, 3 * hh + 2:3 * hh + 3] * o_w[rows])
        o_ref[0, :, hh * HEAD_DIM:(hh + 1) * HEAD_DIM] = o_h.astype(BF16)


def nsa_attend(q, gates, k_c, v_c, ovt, ind, ksel, vsel, kwin, vwin, *, nq, nc, ns, causal, q_pos0, win_pos0):
    b, _, _, t, _ = q.shape
    n_chunks, nsp, kc_keys = ind.shape
    ncp = k_c.shape[2]
    l_keys = ksel.shape[2]
    lw = kwin.shape[2]
    per_bg = lambda rows: pl.BlockSpec((1, 1, rows, HEAD_DIM), lambda bi, g, ti: (bi, g, 0, 0))
    kernel = functools.partial(
        _nsa_attend_kernel, nq=nq, nc=nc, ns=ns, kc_keys=kc_keys, n_chunks=n_chunks, causal=causal,
        q_pos0=q_pos0, win_pos0=win_pos0)
    return pl.pallas_call(
        kernel,
        grid=(b, NSA_GROUPS, t // nq),
        in_specs=[
            pl.BlockSpec((1, 1, HEADS_PER_GROUP, nq, HEAD_DIM), lambda bi, g, ti: (bi, g, 0, ti, 0)),
            pl.BlockSpec((1, nq, LANES), lambda bi, g, ti: (bi, ti, g)),
            per_bg(ncp), per_bg(ncp),
            pl.BlockSpec((nsp, ncp), lambda bi, g, ti: (0, 0)),
            pl.BlockSpec((n_chunks, nsp, kc_keys), lambda bi, g, ti: (0, 0, 0)),
            per_bg(l_keys), per_bg(l_keys), per_bg(lw), per_bg(lw),
        ],
        out_specs=pl.BlockSpec((1, nq, GROUP_LANES), lambda bi, g, ti: (bi, ti, g)),
        out_shape=jax.ShapeDtypeStruct((b, t, NSA_HEADS * HEAD_DIM), BF16),
        scratch_shapes=[pltpu.VMEM((nsp, nq), F32)],
        compiler_params=_params("parallel", "parallel", "arbitrary"),
        name="nsa_attend",
    )(q, gates, k_c, v_c, ovt, ind, ksel, vsel, kwin, vwin)


def _gather_kernel(pt_ref, page_ref, new_ref, cmp_ref, ksel_ref, vsel_ref, *, n_pages):
    p = pl.program_id(1)
    x = page_ref[0]
    cmp_ref[0] = x[:, 0:2 * GROUP_LANES].astype(BF16)

    @pl.when(p < n_pages)
    def _():
        for g in range(NSA_GROUPS):
            ksel_ref[0, g] = x[:, 2 * GROUP_LANES + g * HEAD_DIM:2 * GROUP_LANES + (g + 1) * HEAD_DIM].astype(BF16)
            vsel_ref[0, g] = x[:, 3 * GROUP_LANES + g * HEAD_DIM:3 * GROUP_LANES + (g + 1) * HEAD_DIM].astype(BF16)

    @pl.when(p == n_pages)
    def _():
        new = new_ref[0]
        pad = jnp.zeros((x.shape[0] - new.shape[0], HEAD_DIM), F32)
        for g in range(NSA_GROUPS):
            k_new = new[:, 2 * GROUP_LANES + g * HEAD_DIM:2 * GROUP_LANES + (g + 1) * HEAD_DIM]
            v_new = new[:, 3 * GROUP_LANES + g * HEAD_DIM:3 * GROUP_LANES + (g + 1) * HEAD_DIM]
            ksel_ref[0, g] = jnp.concatenate([k_new, pad], axis=0).astype(BF16)
            vsel_ref[0, g] = jnp.concatenate([v_new, pad], axis=0).astype(BF16)


def gather_past(page_table, cache, new_rows):
    db, n_pages = page_table.shape
    page = cache.shape[1]
    dq = new_rows.shape[1]
    past = n_pages * page
    grid_spec = pltpu.PrefetchScalarGridSpec(
        num_scalar_prefetch=1,
        grid=(db, n_pages + 1),
        in_specs=[
            pl.BlockSpec((1, page, 4 * GROUP_LANES),
                         lambda b, p, pt: (pt[b * n_pages + jnp.minimum(p, n_pages - 1)], 0, 0)),
            pl.BlockSpec((1, dq, 4 * GROUP_LANES), lambda b, p, pt: (b, 0, 0)),
        ],
        out_specs=[
            pl.BlockSpec((1, page, 2 * GROUP_LANES), lambda b, p, pt: (b, jnp.minimum(p, n_pages - 1), 0)),
            pl.BlockSpec((1, NSA_GROUPS, page, HEAD_DIM), lambda b, p, pt: (b, 0, p, 0)),
            pl.BlockSpec((1, NSA_GROUPS, page, HEAD_DIM), lambda b, p, pt: (b, 0, p, 0)),
        ],
    )
    return pl.pallas_call(
        functools.partial(_gather_kernel, n_pages=n_pages),
        grid_spec=grid_spec,
        out_shape=[
            jax.ShapeDtypeStruct((db, past, 2 * GROUP_LANES), BF16),
            jax.ShapeDtypeStruct((db, NSA_GROUPS, past + page, HEAD_DIM), BF16),
            jax.ShapeDtypeStruct((db, NSA_GROUPS, past + page, HEAD_DIM), BF16),
        ],
        compiler_params=_params("parallel", "arbitrary"),
        name="gather_past",
    )(page_table.reshape(-1), cache, new_rows)


def _win_assemble_kernel(cache_ref, new_ref, win_ref, kwin_ref, vwin_ref):
    old = cache_ref[0]
    new = new_ref[0]
    buf = old.shape[0]
    dq = new.shape[0]
    win_ref[0, 0:buf - dq, :] = old[dq:, :]
    win_ref[0, buf - dq:buf, :] = new
    pad = jnp.zeros((WIN_SPAN - buf - dq, HEAD_DIM), F32)
    for ref, off in ((kwin_ref, 0), (vwin_ref, GROUP_LANES)):
        for g in range(NSA_GROUPS):
            lanes = slice(off + g * HEAD_DIM, off + (g + 1) * HEAD_DIM)
            ref[0, g] = jnp.concatenate([old[:, lanes], new[:, lanes], pad], axis=0).astype(BF16)


def win_assemble(cache_win, new_win):
    db, buf, width = cache_win.shape
    dq = new_win.shape[1]
    return pl.pallas_call(
        _win_assemble_kernel,
        grid=(db,),
        in_specs=[
            pl.BlockSpec((1, buf, width), lambda b: (b, 0, 0)),
            pl.BlockSpec((1, dq, width), lambda b: (b, 0, 0)),
        ],
        out_specs=[
            pl.BlockSpec((1, buf, width), lambda b: (b, 0, 0)),
            pl.BlockSpec((1, NSA_GROUPS, WIN_SPAN, HEAD_DIM), lambda b: (b, 0, 0, 0)),
            pl.BlockSpec((1, NSA_GROUPS, WIN_SPAN, HEAD_DIM), lambda b: (b, 0, 0, 0)),
        ],
        out_shape=[
            jax.ShapeDtypeStruct((db, buf, width), F32),
            jax.ShapeDtypeStruct((db, NSA_GROUPS, WIN_SPAN, HEAD_DIM), BF16),
            jax.ShapeDtypeStruct((db, NSA_GROUPS, WIN_SPAN, HEAD_DIM), BF16),
        ],
        compiler_params=_params("parallel"),
        name="win_assemble",
    )(cache_win, new_win)


def _rope_angles(pos, half):
    inv = ROPE_THETA ** (-jnp.arange(half, dtype=F32) / half)
    ang = pos.astype(F32)[:, None] * inv[None, :]
    return jnp.cos(ang), jnp.sin(ang)


def _rope_tables_head64(pos):
    cos, sin = _rope_angles(pos, HEAD_DIM // 2)
    return jnp.concatenate([cos] * 4, axis=1), jnp.concatenate([-sin, sin] * 2, axis=1)


def _block_diag_groups(w):
    eye = jnp.eye(NSA_GROUPS, dtype=w.dtype)
    out = jnp.einsum("gh,...dn->...gdhn", eye, w)
    return out.reshape(*w.shape[:-2], GROUP_LANES, NSA_GROUPS * w.shape[-1])


def _compress_weights(cmp_w1, cmp_b1, cmp_w2, cmp_pe):
    r = CMP_BLOCK // CMP_STRIDE
    w1 = cmp_w1.reshape(2, r, CMP_STRIDE, HEAD_DIM, HEAD_DIM)
    bd = _block_diag_groups(w1)
    bd = jnp.concatenate([bd[:, 0], bd[:, 1]], axis=-1).astype(BF16)
    pe_rows = jnp.broadcast_to(cmp_pe.reshape(2, 1, CMP_BLOCK * HEAD_DIM), (2, 8, CMP_BLOCK * HEAD_DIM)).astype(BF16)
    w1_tiled = jnp.tile(cmp_w1, (1, 1, NSA_GROUPS)).astype(BF16)
    b1_tiled = jnp.tile(cmp_b1, (1, NSA_GROUPS)).reshape(2, 1, GROUP_LANES)
    w2_bd = _block_diag_groups(cmp_w2).astype(BF16)
    return bd[0], bd[1], pe_rows, w1_tiled, b1_tiled, w2_bd


def _selection_tables(length, nc, ncp, ns, nsp, kc_keys):
    ci = np.arange(ncp)[None, :]
    sj = np.arange(nsp)[:, None]
    overlap_t = ((ci * CMP_STRIDE < (sj + 1) * SEL_BLOCK) & (ci * CMP_STRIDE + CMP_BLOCK > sj * SEL_BLOCK)
                 & (ci < nc) & (sj < ns))
    key = np.arange(length).reshape(length // kc_keys, 1, kc_keys)
    ind = (key // SEL_BLOCK) == np.arange(nsp).reshape(1, nsp, 1)
    return jnp.asarray(overlap_t, BF16), jnp.asarray(ind, BF16)


def _round_up(x, m):
    return -(-x // m) * m


def _trunk(x, pos, ret_s0, past, w):
    (ffn_norm, ffn_w_in, ffn_w_out, ret_norm, ret_w_in, ret_w_out, kv_norm, kv_w, k_norm_tiled,
     cmp_weights, nsa_norm, nsa_wq, nsa_wg, q_norm_tiled, nsa_w_out, ones_bd) = w
    b, t, _ = x.shape
    n = b * t
    prompt = past is None
    xf = x.reshape(n, D_MODEL)
    pos_rows = jnp.tile(pos, b) if not prompt else pos
    ret_cos, ret_sin = _rope_angles(pos_rows, RET_DK // 2)
    cos64, sin64 = _rope_tables_head64(pos)
    ret_states = []
    rows = win = None
    attn_args = attn_kw = None
    for layer in range(DEPTH):
        if layer == N_A_LAYERS:
            xs = xf.reshape(b, t, D_MODEL)
            if prompt:
                rows, win, cmp_tok, ksel, vsel, kwin, vwin = kv_rows(
                    xs, kv_norm, kv_w, k_norm_tiled, cos64, sin64, ones_bd, aux=True)
                length = t
                win_pos0 = 0
                nq = SEL_BLOCK
                kc_keys = min(512, t)
                new_win = win[:, t - min(WINDOW, t):]
            else:
                page_table, cache, cache_win = past
                rows, win = kv_rows(xs, kv_norm, kv_w, k_norm_tiled, cos64, sin64, ones_bd, aux=False)
                cmp_tok, ksel, vsel = gather_past(page_table, cache, rows)
                new_win, kwin, vwin = win_assemble(cache_win, win)
                past_len = cmp_tok.shape[1]
                length = past_len + t
                win_pos0 = past_len - cache_win.shape[1]
                nq = t
                kc_keys = 5 * cache.shape[1]
            nc = (length - CMP_BLOCK) // CMP_STRIDE + 1
            n_chunk_rows = nc + CMP_BLOCK // CMP_STRIDE - 1
            ns = -(-length // SEL_BLOCK)
            nsp = _round_up(ns, 16)
            wk_bd, wv_bd, pe_rows, w1_tiled, b1_tiled, w2_bd = cmp_weights
            tok_chunks = cmp_tok[:, :n_chunk_rows * CMP_STRIDE].reshape(
                b * n_chunk_rows, CMP_STRIDE * 2 * GROUP_LANES)
            partial = cmp_partial(tok_chunks, wk_bd, wv_bd).reshape(b, n_chunk_rows, 4 * GROUP_LANES)
            c_end = jnp.arange(n_chunk_rows, dtype=I32) * CMP_STRIDE + (CMP_BLOCK - 1)
            cos_c, sin_c = _rope_tables_head64(c_end)
            k_c, v_c = cmp_combine(partial, pe_rows, w1_tiled, b1_tiled, w2_bd, k_norm_tiled, cos_c, sin_c, ones_bd)
            ovt, ind = _selection_tables(ksel.shape[2], nc, n_chunk_rows, ns, nsp, kc_keys)
            attn_args = (k_c, v_c, ovt, ind, ksel, vsel, kwin, vwin)
            attn_kw = dict(nq=nq, nc=nc, ns=ns, causal=prompt, q_pos0=int(length - t), win_pos0=int(win_pos0))
        xf = ffn_half(xf, ffn_norm[layer, 0], ffn_w_in[layer, 0], ffn_w_out[layer, 0])
        if layer < N_A_LAYERS:
            qkvg = ret_inproj(xf, ret_norm[layer], ret_w_in[layer], ret_cos, ret_sin)
            s0 = None if ret_s0 is None else ret_s0[layer]
            gated, s_fin = ret_core(qkvg.reshape(b, t, 6 * D_MODEL), s0)
            ret_states.append(s_fin)
            xf = mm_residual(gated.reshape(n, 2 * D_MODEL), ret_w_out[layer], xf)
        else:
            j = layer - N_A_LAYERS
            q, gates = nsa_q(xf.reshape(b, t, D_MODEL), nsa_norm[j], nsa_wq[j], nsa_wg[j], q_norm_tiled[j],
                             cos64, sin64, ones_bd)
            o = nsa_attend(q, gates, *attn_args, **attn_kw)
            xf = mm_residual(o.reshape(n, NSA_HEADS * HEAD_DIM), nsa_w_out[j], xf)
        xf = ffn_half(xf, ffn_norm[layer, 1], ffn_w_in[layer, 1], ffn_w_out[layer, 1])
    return xf.reshape(b, t, D_MODEL), jnp.stack(ret_states), rows, new_win


def kernel(x_prompt, x_sample, state_ret, cache_kv, cache_win, page_table, ffn_norm, ffn_w_in, ffn_w_out, ret_norm,
           ret_w_in, ret_w_out, kv_norm, kv_w, k_norm, cmp_w1, cmp_b1, cmp_w2, cmp_pe, nsa_norm, nsa_w_in, q_norm,
           nsa_w_out):
    b, t, _ = x_prompt.shape
    db, dq, _ = x_sample.shape
    n_phys, page = cache_kv.shape[:2]
    past_len = page_table.shape[1] * page
    n_q_cols = NSA_HEADS * HEAD_DIM

    gate_w = nsa_w_in[:, :, n_q_cols:].reshape(-1, D_MODEL, NSA_GROUPS, N_GATES)
    gate_w = jnp.pad(gate_w, ((0, 0), (0, 0), (0, 0), (0, LANES - N_GATES))).reshape(-1, D_MODEL, NSA_GROUPS * LANES)
    eye = np.arange(GROUP_LANES)
    ones_bd = jnp.asarray((eye[:, None] // HEAD_DIM) == (eye[None, :] // HEAD_DIM), BF16)
    w = (ffn_norm, ffn_w_in.astype(BF16), ffn_w_out.astype(BF16), ret_norm, ret_w_in.astype(BF16),
         ret_w_out.astype(BF16), kv_norm, kv_w.astype(BF16), jnp.tile(k_norm, (1, NSA_GROUPS)),
         _compress_weights(cmp_w1, cmp_b1, cmp_w2, cmp_pe), nsa_norm, nsa_w_in[:, :, :n_q_cols].astype(BF16),
         gate_w.astype(BF16), jnp.tile(q_norm, (1, NSA_GROUPS)).reshape(-1, 1, GROUP_LANES),
         nsa_w_out.astype(BF16), ones_bd)

    pos_p = jnp.arange(t, dtype=I32)
    pos_s = past_len + jnp.arange(dq, dtype=I32)
    y_p, ret_p, rows_p, win_p = _trunk(x_prompt, pos_p, None, None, w)
    cache = cache_kv.reshape(n_phys, page, 4 * GROUP_LANES)
    cwin = cache_win.reshape(db, cache_win.shape[1], 2 * GROUP_LANES)
    y_s, ret_s, rows_s, win_s = _trunk(x_sample, pos_s, state_ret, (page_table, cache, cwin), w)
    kv_shape = (4, NSA_GROUPS, HEAD_DIM)
    win_shape = (2, NSA_GROUPS, HEAD_DIM)
    return (y_p, y_s, ret_p.astype(state_ret.dtype), ret_s.astype(state_ret.dtype),
            rows_p.reshape(b, t, *kv_shape), rows_s.reshape(db, dq, *kv_shape),
            win_p.reshape(b, win_p.shape[1], *win_shape), win_s.reshape(db, win_s.shape[1], *win_shape))
```

```python
import functools

import jax
import jax.numpy as jnp
import numpy as np
from jax import lax
from jax.experimental import pallas as pl
from jax.experimental.pallas import tpu as pltpu

F32 = jnp.float32
BF16 = jnp.bfloat16
I32 = jnp.int32

D_MODEL = 1024
DEPTH = 4
N_A_LAYERS = DEPTH // 2
RET_HEADS = 4
RET_DK = D_MODEL // RET_HEADS
RET_DV = 2 * D_MODEL // RET_HEADS
RET_CHUNK = 128
NSA_HEADS = 16
NSA_GROUPS = 4
HEADS_PER_GROUP = NSA_HEADS // NSA_GROUPS
HEAD_DIM = D_MODEL // NSA_HEADS
CMP_BLOCK = 32
CMP_STRIDE = 16
SEL_BLOCK = 64
N_SELECT = 16
N_LOCAL = 2
WINDOW = 512
D_FF = 2816
ROPE_THETA = 10000.0
EPS = 1e-6
NEG = -1e30
BIG = 1e9
N_GATES = 3 * HEADS_PER_GROUP
GROUP_LANES = NSA_GROUPS * HEAD_DIM
PROMPT_NQ = 128
WIN_SPAN = WINDOW + PROMPT_NQ
MASK_ROWS = 64
MASK_BIAS = -(2.0 ** 100)
GATHER_PAGES = 4

VMEM_LIMIT_BYTES = 56 * 1024 * 1024
LANES = 128

NT_DIMS = (((1,), (1,)), ((), ()))
TN_DIMS = (((0,), (0,)), ((), ()))


def _params(*semantics):
    return pltpu.CompilerParams(dimension_semantics=semantics, vmem_limit_bytes=VMEM_LIMIT_BYTES)


def _rms(x, gain):
    ms = jnp.mean(x * x, axis=-1, keepdims=True)
    return x * lax.rsqrt(ms + EPS) * gain


def _seg_rms(y, ones_bd, gain):
    sq = y * y
    hi = sq.astype(BF16)
    lo = (sq - hi.astype(F32)).astype(BF16)
    ss = jnp.dot(hi, ones_bd, preferred_element_type=F32) + jnp.dot(lo, ones_bd, preferred_element_type=F32)
    return y * lax.rsqrt(ss * (1.0 / HEAD_DIM) + EPS) * gain


def _rope64(x, cos, sin_signed):
    lane = lax.broadcasted_iota(I32, x.shape, 1)
    first_half = (lane % HEAD_DIM) < (HEAD_DIM // 2)
    rot = jnp.where(first_half, pltpu.roll(x, LANES - HEAD_DIM // 2, 1), pltpu.roll(x, HEAD_DIM // 2, 1))
    return x * cos + rot * sin_signed


def _exp_rows(s, ok):
    s = jnp.where(ok, s, NEG)
    m = jnp.max(s, axis=-1, keepdims=True)
    e = jnp.exp(s - m)
    return e, jnp.sum(e, axis=-1, keepdims=True)


def _ffn_kernel(x_ref, g_ref, wi_ref, wo_ref, o_ref):
    x = x_ref[...]
    xn = _rms(x, g_ref[...]).astype(BF16)
    a = jnp.dot(xn, wi_ref[:, 0:D_FF], preferred_element_type=F32)
    b = jnp.dot(xn, wi_ref[:, D_FF:2 * D_FF], preferred_element_type=F32)
    h = (a * jax.nn.sigmoid(a) * b).astype(BF16)
    o_ref[...] = x + 0.5 * jnp.dot(h, wo_ref[...], preferred_element_type=F32)


def ffn_half(x, gain, w_in, w_out):
    n = x.shape[0]
    tm = min(n, 512)
    resident = pl.Buffered(1)
    return pl.pallas_call(
        _ffn_kernel,
        grid=(n // tm,),
        in_specs=[
            pl.BlockSpec((tm, D_MODEL), lambda i: (i, 0)),
            pl.BlockSpec((1, D_MODEL), lambda i: (0, 0)),
            pl.BlockSpec((D_MODEL, 2 * D_FF), lambda i: (0, 0), pipeline_mode=resident),
            pl.BlockSpec((D_FF, D_MODEL), lambda i: (0, 0), pipeline_mode=resident),
        ],
        out_specs=pl.BlockSpec((tm, D_MODEL), lambda i: (i, 0)),
        out_shape=jax.ShapeDtypeStruct((n, D_MODEL), F32),
        compiler_params=_params("parallel"),
        name="ffn_half",
    )(x, gain.reshape(1, D_MODEL), w_in, w_out)


def _mm_res_kernel(a_ref, w_ref, x_ref, o_ref):
    o_ref[...] = x_ref[...] + jnp.dot(a_ref[...], w_ref[...], preferred_element_type=F32)


def mm_residual(a, w, x):
    n, k = a.shape
    tm = min(n, 512)
    return pl.pallas_call(
        _mm_res_kernel,
        grid=(n // tm,),
        in_specs=[
            pl.BlockSpec((tm, k), lambda i: (i, 0)),
            pl.BlockSpec((k, D_MODEL), lambda i: (0, 0)),
            pl.BlockSpec((tm, D_MODEL), lambda i: (i, 0)),
        ],
        out_specs=pl.BlockSpec((tm, D_MODEL), lambda i: (i, 0)),
        out_shape=jax.ShapeDtypeStruct((n, D_MODEL), F32),
        compiler_params=_params("parallel"),
        name="mm_residual",
    )(a, w, x)


def _ret_inproj_kernel(x_ref, g_ref, w_ref, cos_ref, sin_ref, o_ref):
    xn = _rms(x_ref[...], g_ref[...]).astype(BF16)
    c = cos_ref[...]
    s = sin_ref[...]
    half = RET_DK // 2
    for h in range(2 * RET_HEADS):
        lo = h * RET_DK
        y = jnp.dot(xn, w_ref[:, lo:lo + RET_DK], preferred_element_type=F32)
        scale = 1.0 if h < RET_HEADS else RET_DK ** -0.5
        x1 = y[:, :half]
        x2 = y[:, half:]
        o_ref[:, lo:lo + half] = ((x1 * c - x2 * s) * scale).astype(BF16)
        o_ref[:, lo + half:lo + RET_DK] = ((x1 * s + x2 * c) * scale).astype(BF16)
    for h in range(2 * RET_HEADS):
        lo = 2 * D_MODEL + h * RET_DV
        o_ref[:, lo:lo + RET_DV] = jnp.dot(xn, w_ref[:, lo:lo + RET_DV], preferred_element_type=F32).astype(BF16)


def ret_inproj(x, gain, w, cos, sin):
    n = x.shape[0]
    p = cos.shape[0]
    tm = min(n, 512, p)
    n_out = w.shape[1]
    tab_blocks = p // tm
    return pl.pallas_call(
        _ret_inproj_kernel,
        grid=(n // tm,),
        in_specs=[
            pl.BlockSpec((tm, D_MODEL), lambda i: (i, 0)),
            pl.BlockSpec((1, D_MODEL), lambda i: (0, 0)),
            pl.BlockSpec((D_MODEL, n_out), lambda i: (0, 0), pipeline_mode=pl.Buffered(1)),
            pl.BlockSpec((tm, RET_DK // 2), lambda i: (i % tab_blocks, 0)),
            pl.BlockSpec((tm, RET_DK // 2), lambda i: (i % tab_blocks, 0)),
        ],
        out_specs=pl.BlockSpec((tm, n_out), lambda i: (i, 0)),
        out_shape=jax.ShapeDtypeStruct((n, n_out), BF16),
        compiler_params=_params("parallel"),
        name="ret_inproj",
    )(x, gain.reshape(1, D_MODEL), w, cos, sin)


def _ret_core_kernel(*refs, chunk, n_inner, has_s0):
    if has_s0:
        (q_ref, k_ref, v_ref, g_ref, dm_ref, qd_ref, kd_ref, sd_ref, s0_ref, o_ref, so_ref, s_scr) = refs
    else:
        (q_ref, k_ref, v_ref, g_ref, dm_ref, qd_ref, kd_ref, sd_ref, o_ref, so_ref, s_scr) = refs
    t = pl.program_id(2)

    @pl.when(t == 0)
    def _():
        if has_s0:
            s_scr[...] = s0_ref[0, 0]
        else:
            s_scr[...] = jnp.zeros_like(s_scr)

    dm = dm_ref[0]
    qd = qd_ref[0]
    kd = kd_ref[0]
    sd = sd_ref[0, 0:1, 0:1]
    for c in range(n_inner):
        rows = slice(c * chunk, (c + 1) * chunk)
        q = q_ref[0, rows, :]
        k = k_ref[0, rows, :]
        v = v_ref[0, rows, :]
        g = g_ref[0, rows, :].astype(F32)
        s = s_scr[...]
        scores = lax.dot_general(q, k, NT_DIMS, preferred_element_type=F32) * dm
        intra = jnp.dot(scores.astype(BF16), v, preferred_element_type=F32)
        cross = jnp.dot((q.astype(F32) * qd).astype(BF16), s.astype(BF16), preferred_element_type=F32)
        o = intra + cross
        kv = lax.dot_general((k.astype(F32) * kd).astype(BF16), v, TN_DIMS, preferred_element_type=F32)
        s_scr[...] = sd * s + kv
        mu = jnp.mean(o, axis=-1, keepdims=True)
        d = o - mu
        var = jnp.mean(d * d, axis=-1, keepdims=True)
        on = d * lax.rsqrt(var + EPS)
        o_ref[0, rows, :] = (g * jax.nn.sigmoid(g) * on).astype(BF16)

    @pl.when(t == pl.num_programs(2) - 1)
    def _():
        so_ref[0, 0] = s_scr[...]


def _decay_tables(chunk):
    lg = jnp.log(1.0 - 2.0 ** (-5.0 - jnp.arange(RET_HEADS, dtype=F32)))
    idx = jnp.arange(chunk, dtype=F32)
    rel = idx[:, None] - idx[None, :]
    dmat = jnp.where(rel >= 0, jnp.exp(jnp.maximum(rel, 0.0)[None] * lg[:, None, None]), 0.0)
    qdec = jnp.exp((idx + 1.0)[None, :] * lg[:, None])
    kdec = jnp.exp((chunk - 1.0 - idx)[None, :] * lg[:, None])
    sdec = jnp.exp(chunk * lg)
    qdec = jnp.broadcast_to(qdec[:, :, None], (RET_HEADS, chunk, RET_DK))
    kdec = jnp.broadcast_to(kdec[:, :, None], (RET_HEADS, chunk, RET_DK))
    sdec = jnp.broadcast_to(sdec[:, None, None], (RET_HEADS, 8, LANES))
    return dmat, qdec, kdec, sdec


def ret_core(qkvg, s0):
    b, t, _ = qkvg.shape
    chunk = RET_CHUNK if t % RET_CHUNK == 0 else t
    tb = min(t, 4 * chunk)
    n_inner = tb // chunk
    dmat, qdec, kdec, sdec = _decay_tables(chunk)
    k_off = D_MODEL // RET_DK
    v_off = 2 * D_MODEL // RET_DV
    g_off = 4 * D_MODEL // RET_DV
    in_specs = [
        pl.BlockSpec((1, tb, RET_DK), lambda bi, h, ti: (bi, ti, h)),
        pl.BlockSpec((1, tb, RET_DK), lambda bi, h, ti: (bi, ti, k_off + h)),
        pl.BlockSpec((1, tb, RET_DV), lambda bi, h, ti: (bi, ti, v_off + h)),
        pl.BlockSpec((1, tb, RET_DV), lambda bi, h, ti: (bi, ti, g_off + h)),
        pl.BlockSpec((1, chunk, chunk), lambda bi, h, ti: (h, 0, 0)),
        pl.BlockSpec((1, chunk, RET_DK), lambda bi, h, ti: (h, 0, 0)),
        pl.BlockSpec((1, chunk, RET_DK), lambda bi, h, ti: (h, 0, 0)),
        pl.BlockSpec((1, 8, LANES), lambda bi, h, ti: (h, 0, 0)),
    ]
    args = [qkvg, qkvg, qkvg, qkvg, dmat, qdec, kdec, sdec]
    if s0 is not None:
        in_specs.append(pl.BlockSpec((1, 1, RET_DK, RET_DV), lambda bi, h, ti: (bi, h, 0, 0)))
        args.append(s0)
    return pl.pallas_call(
        functools.partial(_ret_core_kernel, chunk=chunk, n_inner=n_inner, has_s0=s0 is not None),
        grid=(b, RET_HEADS, t // tb),
        in_specs=in_specs,
        out_specs=[
            pl.BlockSpec((1, tb, RET_DV), lambda bi, h, ti: (bi, ti, h)),
            pl.BlockSpec((1, 1, RET_DK, RET_DV), lambda bi, h, ti: (bi, h, 0, 0)),
        ],
        out_shape=[
            jax.ShapeDtypeStruct((b, t, 2 * D_MODEL), BF16),
            jax.ShapeDtypeStruct((b, RET_HEADS, RET_DK, RET_DV), F32),
        ],
        scratch_shapes=[pltpu.VMEM((RET_DK, RET_DV), F32)],
        compiler_params=_params("parallel", "parallel", "arbitrary"),
        name="ret_core",
    )(*args)


def _store_chunk_major(cmp_ref, scr, n_rows):
    n_planes = scr.shape[0]
    for p in range(CMP_STRIDE):
        for c in range(n_planes):
            lo = p * n_planes * LANES + c * LANES
            cmp_ref[:, lo:lo + LANES] = scr[c, pl.ds(p, n_rows // CMP_STRIDE, stride=CMP_STRIDE), :].astype(BF16)


def _kv_rows_kernel(x_ref, g_ref, w_ref, kn_ref, cos_ref, sin_ref, ones_ref, rows_ref, win_ref, *aux_refs):
    xn = _rms(x_ref[0], g_ref[...]).astype(BF16)
    y = jnp.dot(xn, w_ref[...], preferred_element_type=F32)
    tm = y.shape[0]
    cos = cos_ref[...]
    sin = sin_ref[...]
    ones_bd = ones_ref[...]
    slot = lambda s: y[:, s * GROUP_LANES:(s + 1) * GROUP_LANES]

    def norm_rope(v, gain):
        vn = _seg_rms(v, ones_bd, gain)
        return jnp.concatenate([_rope64(vn[:, :LANES], cos, sin), _rope64(vn[:, LANES:], cos, sin)], axis=1)

    k_slc = norm_rope(slot(2), kn_ref[1:2, :])
    k_win = norm_rope(slot(4), kn_ref[2:3, :])
    rows_ref[0, :, 0:2 * GROUP_LANES] = y[:, 0:2 * GROUP_LANES]
    rows_ref[0, :, 2 * GROUP_LANES:3 * GROUP_LANES] = k_slc
    rows_ref[0, :, 3 * GROUP_LANES:4 * GROUP_LANES] = slot(3)
    win_ref[0, :, 0:GROUP_LANES] = k_win
    win_ref[0, :, GROUP_LANES:2 * GROUP_LANES] = slot(5)
    if aux_refs:
        cmp_ref, ksel_ref, vsel_ref, kwin_ref, vwin_ref, scr = aux_refs
        for c in range(scr.shape[0]):
            scr[c] = y[:, c * LANES:(c + 1) * LANES]
        _store_chunk_major(cmp_ref, scr, tm)
        for ref, val in ((ksel_ref, k_slc), (vsel_ref, slot(3)), (kwin_ref, k_win), (vwin_ref, slot(5))):
            vt = jnp.transpose(val)
            for g in range(NSA_GROUPS):
                ref[0, g, 0:HEAD_DIM, :] = vt[g * HEAD_DIM:(g + 1) * HEAD_DIM, :].astype(BF16)
        key_blk = (pl.program_id(1) * tm + lax.broadcasted_iota(I32, (MASK_ROWS, tm), 1)) // SEL_BLOCK
        mask_rows = jnp.where(key_blk == lax.broadcasted_iota(I32, (MASK_ROWS, tm), 0), MASK_BIAS, 0.0).astype(BF16)
        for g in range(NSA_GROUPS):
            ksel_ref[0, g, HEAD_DIM:HEAD_DIM + MASK_ROWS, :] = mask_rows


def kv_rows(x, gain, w, k_norm_tiled, cos, sin, ones_bd, aux):
    b, t, _ = x.shape
    tm = min(t, 512)
    n_kv = w.shape[1]
    nt = t // tm
    out_specs = [
        pl.BlockSpec((1, tm, 4 * GROUP_LANES), lambda bi, ti: (bi, ti, 0)),
        pl.BlockSpec((1, tm, 2 * GROUP_LANES), lambda bi, ti: (bi, ti, 0)),
    ]
    out_shape = [
        jax.ShapeDtypeStruct((b, t, 4 * GROUP_LANES), F32),
        jax.ShapeDtypeStruct((b, t, 2 * GROUP_LANES), F32),
    ]
    scratch = []
    if aux:
        chunk_lanes = CMP_STRIDE * 2 * GROUP_LANES
        out_specs.append(pl.BlockSpec((tm // CMP_STRIDE, chunk_lanes), lambda bi, ti: (bi * nt + ti, 0)))
        out_shape.append(jax.ShapeDtypeStruct((b * t // CMP_STRIDE, chunk_lanes), BF16))
        for rows in (HEAD_DIM + MASK_ROWS, HEAD_DIM, HEAD_DIM, HEAD_DIM):
            out_specs.append(pl.BlockSpec((1, NSA_GROUPS, rows, tm), lambda bi, ti: (bi, 0, 0, ti)))
            out_shape.append(jax.ShapeDtypeStruct((b, NSA_GROUPS, rows, t), BF16))
        scratch.append(pltpu.VMEM((2 * GROUP_LANES // LANES, tm, LANES), F32))
    return pl.pallas_call(
        _kv_rows_kernel,
        grid=(b, nt),
        in_specs=[
            pl.BlockSpec((1, tm, D_MODEL), lambda bi, ti: (bi, ti, 0)),
            pl.BlockSpec((1, D_MODEL), lambda bi, ti: (0, 0)),
            pl.BlockSpec((D_MODEL, n_kv), lambda bi, ti: (0, 0)),
            pl.BlockSpec((3, GROUP_LANES), lambda bi, ti: (0, 0)),
            pl.BlockSpec((tm, LANES), lambda bi, ti: (ti, 0)),
            pl.BlockSpec((tm, LANES), lambda bi, ti: (ti, 0)),
            pl.BlockSpec((GROUP_LANES, GROUP_LANES), lambda bi, ti: (0, 0)),
        ],
        out_specs=out_specs,
        out_shape=out_shape,
        scratch_shapes=scratch,
        compiler_params=_params("parallel", "parallel"),
        name="kv_rows",
    )(x, gain.reshape(1, D_MODEL), w, k_norm_tiled, cos, sin, ones_bd)


def _cmp_partial_kernel(x_ref, wk_ref, wv_ref, o_ref):
    acc_k = jnp.zeros((x_ref.shape[0], 2 * GROUP_LANES), F32)
    acc_v = jnp.zeros((x_ref.shape[0], 2 * GROUP_LANES), F32)
    for p in range(CMP_STRIDE):
        lo = p * 2 * GROUP_LANES
        acc_k += jnp.dot(x_ref[:, lo:lo + GROUP_LANES], wk_ref[p], preferred_element_type=F32)
        acc_v += jnp.dot(x_ref[:, lo + GROUP_LANES:lo + 2 * GROUP_LANES], wv_ref[p], preferred_element_type=F32)
    o_ref[:, 0:2 * GROUP_LANES] = acc_k
    o_ref[:, 2 * GROUP_LANES:4 * GROUP_LANES] = acc_v


def cmp_partial(tok_chunks, wk_bd, wv_bd):
    n, width = tok_chunks.shape
    tm = min(n, 512)
    return pl.pallas_call(
        _cmp_partial_kernel,
        grid=(n // tm,),
        in_specs=[
            pl.BlockSpec((tm, width), lambda i: (i, 0)),
            pl.BlockSpec(wk_bd.shape, lambda i: (0, 0, 0)),
            pl.BlockSpec(wv_bd.shape, lambda i: (0, 0, 0)),
        ],
        out_specs=pl.BlockSpec((tm, 4 * GROUP_LANES), lambda i: (i, 0)),
        out_shape=jax.ShapeDtypeStruct((n, 4 * GROUP_LANES), F32),
        compiler_params=_params("parallel"),
        name="cmp_partial",
    )(tok_chunks, wk_bd, wv_bd)


def _cmp_combine_kernel(a_ref, pe_ref, w1_ref, b1_ref, w2_ref, kn_ref, cos_ref, sin_ref, ones_ref, kc_ref, vc_ref):
    a = a_ref[0]
    n = a.shape[0]
    for t, out_ref in enumerate((kc_ref, vc_ref)):
        first = a[:, 2 * t * GROUP_LANES:(2 * t + 1) * GROUP_LANES]
        second = a[:, (2 * t + 1) * GROUP_LANES:(2 * t + 2) * GROUP_LANES]
        second = pltpu.roll(second, n - 1, 0)
        pe_term = jnp.dot(pe_ref[t], w1_ref[t], preferred_element_type=F32)[0:1]
        h = b1_ref[t] + pe_term + first + second
        y = jnp.dot(jax.nn.gelu(h).astype(BF16), w2_ref[t], preferred_element_type=F32)
        if t == 0:
            y = _seg_rms(y, ones_ref[...], kn_ref[0:1, :])
            y = jnp.concatenate(
                [_rope64(y[:, :LANES], cos_ref[...], sin_ref[...]), _rope64(y[:, LANES:], cos_ref[...], sin_ref[...])],
                axis=1)
        yt = jnp.transpose(y)
        for g in range(NSA_GROUPS):
            out_ref[0, g] = yt[g * HEAD_DIM:(g + 1) * HEAD_DIM, :].astype(BF16)


def cmp_combine(partial, pe_rows, w1_tiled, b1_tiled, w2_bd, k_norm_tiled, cos, sin, ones_bd):
    b, ncp, _ = partial.shape
    full = lambda *shape: pl.BlockSpec(shape, lambda bi: (0,) * len(shape))
    return pl.pallas_call(
        _cmp_combine_kernel,
        grid=(b,),
        in_specs=[
            pl.BlockSpec((1, ncp, 4 * GROUP_LANES), lambda bi: (bi, 0, 0)),
            full(*pe_rows.shape), full(*w1_tiled.shape), full(*b1_tiled.shape), full(*w2_bd.shape),
            full(3, GROUP_LANES), full(ncp, LANES), full(ncp, LANES), full(GROUP_LANES, GROUP_LANES),
        ],
        out_specs=[pl.BlockSpec((1, NSA_GROUPS, HEAD_DIM, ncp), lambda bi: (bi, 0, 0, 0))] * 2,
        out_shape=[jax.ShapeDtypeStruct((b, NSA_GROUPS, HEAD_DIM, ncp), BF16)] * 2,
        compiler_params=_params("parallel"),
        name="cmp_combine",
    )(partial, pe_rows, w1_tiled, b1_tiled, w2_bd, k_norm_tiled, cos, sin, ones_bd)


def _nsa_q_kernel(x_ref, g_ref, wq_ref, wg_ref, qn_ref, cos_ref, sin_ref, ones_ref, q_ref, gate_ref):
    xn = _rms(x_ref[0], g_ref[...]).astype(BF16)
    y = jnp.dot(xn, wq_ref[...], preferred_element_type=F32)
    cos = cos_ref[...]
    sin = sin_ref[...]
    scale = HEAD_DIM ** -0.5
    for g in range(NSA_GROUPS):
        yg = _seg_rms(y[:, g * GROUP_LANES:(g + 1) * GROUP_LANES], ones_ref[...], qn_ref[...])
        for half in range(2):
            r = _rope64(yg[:, half * LANES:(half + 1) * LANES], cos, sin) * scale
            for hh in range(2):
                q_ref[0, g, 2 * half + hh] = r[:, hh * HEAD_DIM:(hh + 1) * HEAD_DIM]
    gates = jnp.dot(xn, wg_ref[...], preferred_element_type=F32)
    gate_ref[0] = jax.nn.sigmoid(gates)


def nsa_q(x, gain, wq, wg, q_norm_tiled, cos, sin, ones_bd):
    b, t, _ = x.shape
    tm = min(t, 512)
    return pl.pallas_call(
        _nsa_q_kernel,
        grid=(b, t // tm),
        in_specs=[
            pl.BlockSpec((1, tm, D_MODEL), lambda bi, ti: (bi, ti, 0)),
            pl.BlockSpec((1, D_MODEL), lambda bi, ti: (0, 0)),
            pl.BlockSpec(wq.shape, lambda bi, ti: (0, 0)),
            pl.BlockSpec(wg.shape, lambda bi, ti: (0, 0)),
            pl.BlockSpec((1, GROUP_LANES), lambda bi, ti: (0, 0)),
            pl.BlockSpec((tm, LANES), lambda bi, ti: (ti, 0)),
            pl.BlockSpec((tm, LANES), lambda bi, ti: (ti, 0)),
            pl.BlockSpec((GROUP_LANES, GROUP_LANES), lambda bi, ti: (0, 0)),
        ],
        out_specs=[
            pl.BlockSpec((1, NSA_GROUPS, HEADS_PER_GROUP, tm, HEAD_DIM), lambda bi, ti: (bi, 0, 0, ti, 0)),
            pl.BlockSpec((1, tm, NSA_GROUPS * LANES), lambda bi, ti: (bi, ti, 0)),
        ],
        out_shape=[
            jax.ShapeDtypeStruct((b, NSA_GROUPS, HEADS_PER_GROUP, t, HEAD_DIM), F32),
            jax.ShapeDtypeStruct((b, t, NSA_GROUPS * LANES), F32),
        ],
        compiler_params=_params("parallel", "parallel"),
        name="nsa_q",
    )(x, gain.reshape(1, D_MODEL), wq, wg, q_norm_tiled, cos, sin, ones_bd)


def _compressed_branch(q, qpos, q_first, kc_t, vc_t, ovt, nq, nc, ns):
    hpg = HEADS_PER_GROUP
    ncp = kc_t.shape[1]
    nsp = ovt.shape[0]
    s_c = jnp.dot(q, kc_t, preferred_element_type=F32)
    cidx = lax.broadcasted_iota(I32, (1, ncp), 1)
    valid_c = (cidx * CMP_STRIDE + (CMP_BLOCK - 1) <= qpos) & (cidx < nc)
    e_c, l_c = _exp_rows(s_c, valid_c)
    p_c = jnp.where(valid_c, e_c * (1.0 / l_c), 0.0)
    o_c = lax.dot_general(p_c.astype(BF16), vc_t, NT_DIMS, preferred_element_type=F32)
    p_sum = p_c[0:nq]
    for hh in range(1, hpg):
        p_sum = p_sum + p_c[hh * nq:(hh + 1) * nq]
    p_hi = p_sum.astype(BF16)
    p_lo = (p_sum - p_hi.astype(F32)).astype(BF16)
    imp_t = (lax.dot_general(ovt, p_hi, NT_DIMS, preferred_element_type=F32)
             + lax.dot_general(ovt, p_lo, NT_DIMS, preferred_element_type=F32))
    sidx = lax.broadcasted_iota(I32, (nsp, 1), 0)
    q_blk = (q_first + lax.broadcasted_iota(I32, (1, nq), 1)) // SEL_BLOCK
    valid_s = (sidx <= q_blk) & (sidx < ns)
    forced = (sidx == 0) | (valid_s & (q_blk - sidx < N_LOCAL))
    score_t = jnp.where(forced, BIG, jnp.where(valid_s, imp_t, NEG))
    return o_c, score_t, valid_s


def _window_branch(q, qpos, kw_t, vw_t, kpos0):
    s_w = jnp.dot(q, kw_t, preferred_element_type=F32)
    kpos_w = kpos0 + lax.broadcasted_iota(I32, (1, kw_t.shape[1]), 1)
    kpos_w = jnp.where(kpos_w >= 0, kpos_w, -(2 ** 30))
    behind = lax.bitcast_convert_type(qpos - kpos_w, jnp.uint32)
    e_w, l_w = _exp_rows(s_w, behind < jnp.uint32(WINDOW))
    return lax.dot_general(e_w.astype(BF16), vw_t, NT_DIMS, preferred_element_type=F32), l_w


def _gate_and_store(o_ref, gates, o_c, acc_s, l_s, acc_w, l_w, nq):
    inv_s = 1.0 / l_s
    inv_w = 1.0 / l_w
    for hh in range(HEADS_PER_GROUP):
        rows = slice(hh * nq, (hh + 1) * nq)
        o_h = (gates[:, 3 * hh:3 * hh + 1] * o_c[rows] + (gates[:, 3 * hh + 1:3 * hh + 2] * inv_s[rows]) * acc_s[rows]
               + (gates[:, 3 * hh + 2:3 * hh + 3] * inv_w[rows]) * acc_w[rows])
        o_ref[0, :, hh * HEAD_DIM:(hh + 1) * HEAD_DIM] = o_h.astype(BF16)


def _nsa_prompt_kernel(q_ref, gate_ref, kc_ref, vc_ref, ovt_ref, ksel_ref, vsel_ref, kwin_ref, vwin_ref,
                       o_ref, score_scr, *, nq, nc, ns, kc_keys):
    i = pl.program_id(2)
    hpg = HEADS_PER_GROUP
    r = hpg * nq
    nsp = ovt_ref.shape[0]
    t_len = kwin_ref.shape[3]
    q_first = i * nq
    q32 = q_ref[0, 0].reshape(r, HEAD_DIM)
    q = q32.astype(BF16)
    qpos = q_first + lax.broadcasted_iota(I32, (r, 1), 0) % nq
    o_c, score_t, valid_s = _compressed_branch(q, qpos, q_first, kc_ref[0, 0], vc_ref[0, 0], ovt_ref[...], nq, nc, ns)

    score_scr[...] = score_t
    sidx = lax.broadcasted_iota(I32, (nsp, 1), 0)
    n_live = jnp.minimum((q_first + nq - 1) // SEL_BLOCK + 1, ns)

    def rank_body(sp, rank):
        row = score_scr[pl.ds(sp, 1), :]
        ahead = (row > score_t) | ((row == score_t) & (sidx > sp))
        return rank + ahead.astype(F32)

    rank = lax.fori_loop(0, n_live, rank_body, jnp.zeros((nsp, nq), F32))
    dropped_t = 1.0 - ((rank < float(min(N_SELECT, ns))) & valid_s).astype(F32)
    dropped = jnp.transpose(dropped_t)
    qa = jnp.concatenate([q32, jnp.concatenate([dropped] * hpg, axis=0)], axis=1).astype(BF16)

    def chunk_body(c, carry, causal):
        m, l, acc = carry
        start = pl.multiple_of(c * kc_keys, kc_keys)
        s = jnp.dot(qa, ksel_ref[0, 0, :, pl.ds(start, kc_keys)], preferred_element_type=F32)
        if causal:
            kpos = c * kc_keys + lax.broadcasted_iota(I32, (1, kc_keys), 1)
            s = jnp.where(kpos <= qpos, s, NEG)
        m_new = jnp.maximum(m, jnp.max(s, axis=-1, keepdims=True))
        alpha = jnp.exp(m - m_new)
        p = jnp.exp(s - m_new)
        l = alpha * l + jnp.sum(p, axis=-1, keepdims=True)
        pv = lax.dot_general(p.astype(BF16), vsel_ref[0, 0, :, pl.ds(start, kc_keys)], NT_DIMS,
                             preferred_element_type=F32)
        return m_new, l, alpha * acc + pv

    init = (jnp.full((r, 1), NEG, F32), jnp.zeros((r, 1), F32), jnp.zeros((r, HEAD_DIM), F32))
    n_past = q_first // kc_keys
    carry = lax.fori_loop(0, n_past, functools.partial(chunk_body, causal=False), init)
    _, l_s, acc_s = chunk_body(n_past, carry, causal=True)

    w0 = pl.multiple_of(jnp.clip(q_first + nq - WIN_SPAN, 0, t_len - WIN_SPAN), LANES)
    acc_w, l_w = _window_branch(q, qpos, kwin_ref[0, 0, :, pl.ds(w0, WIN_SPAN)],
                                vwin_ref[0, 0, :, pl.ds(w0, WIN_SPAN)], w0)
    _gate_and_store(o_ref, gate_ref[0], o_c, acc_s, l_s, acc_w, l_w, nq)


def nsa_attend_prompt(q, gates, kc_t, vc_t, ovt, ksel_t, vsel_t, kwin_t, vwin_t, *, nc, ns):
    b, _, _, t, _ = q.shape
    nq = PROMPT_NQ
    kc_keys = min(512, t)
    ncp = kc_t.shape[3]
    nsp = ovt.shape[0]
    per_bg = lambda rows, cols: pl.BlockSpec((1, 1, rows, cols), lambda bi, g, ti: (bi, g, 0, 0))
    kernel = functools.partial(_nsa_prompt_kernel, nq=nq, nc=nc, ns=ns, kc_keys=kc_keys)
    return pl.pallas_call(
        kernel,
        grid=(b, NSA_GROUPS, t // nq),
        in_specs=[
            pl.BlockSpec((1, 1, HEADS_PER_GROUP, nq, HEAD_DIM), lambda bi, g, ti: (bi, g, 0, ti, 0)),
            pl.BlockSpec((1, nq, LANES), lambda bi, g, ti: (bi, ti, g)),
            per_bg(HEAD_DIM, ncp), per_bg(HEAD_DIM, ncp),
            pl.BlockSpec((nsp, ncp), lambda bi, g, ti: (0, 0)),
            per_bg(HEAD_DIM + MASK_ROWS, t), per_bg(HEAD_DIM, t), per_bg(HEAD_DIM, t), per_bg(HEAD_DIM, t),
        ],
        out_specs=pl.BlockSpec((1, nq, GROUP_LANES), lambda bi, g, ti: (bi, ti, g)),
        out_shape=jax.ShapeDtypeStruct((b, t, NSA_HEADS * HEAD_DIM), BF16),
        scratch_shapes=[pltpu.VMEM((nsp, nq), F32)],
        compiler_params=_params("parallel", "parallel", "arbitrary"),
        name="nsa_attend_prompt",
    )(q, gates, kc_t, vc_t, ovt, ksel_t, vsel_t, kwin_t, vwin_t)


def _nsa_decode_kernel(q_ref, gate_ref, kc_ref, vc_ref, ovt_ref, ind_ref, ksel_ref, vsel_ref, kwin_ref, vwin_ref,
                       o_ref, *, nq, nc, ns, q_pos0, win_pos0):
    hpg = HEADS_PER_GROUP
    r = hpg * nq
    nsp = ovt_ref.shape[0]
    l_keys = ksel_ref.shape[3]
    q = q_ref[0, 0].reshape(r, HEAD_DIM).astype(BF16)
    qpos = q_pos0 + lax.broadcasted_iota(I32, (r, 1), 0) % nq
    o_c, score_t, valid_s = _compressed_branch(q, qpos, q_pos0, kc_ref[0, 0], vc_ref[0, 0], ovt_ref[...], nq, nc, ns)

    score = jnp.transpose(score_t)
    valid = jnp.transpose(valid_s.astype(F32))
    s_other = lax.broadcasted_iota(I32, (nsp, 1), 0)
    s_self = lax.broadcasted_iota(I32, (1, nsp), 1)
    qrow = lax.broadcasted_iota(I32, (nq, 1), 0)
    rank = jnp.zeros((nq, nsp), F32)
    for qi in range(nq):
        other = score_t[:, qi:qi + 1]
        own = score[qi:qi + 1, :]
        ahead = (other > own) | ((other == own) & (s_other < s_self))
        rank = jnp.where(qrow == qi, jnp.sum(ahead.astype(F32), axis=0, keepdims=True), rank)
    sel = ((rank < float(min(N_SELECT, ns))) & (valid > 0.5)).astype(F32)
    sel_rows = jnp.concatenate([sel] * hpg, axis=0).astype(BF16)

    s = jnp.dot(q, ksel_ref[0, 0], preferred_element_type=F32)
    picked = jnp.dot(sel_rows, ind_ref[...], preferred_element_type=F32)
    kpos = lax.broadcasted_iota(I32, (1, l_keys), 1)
    e_s, l_s = _exp_rows(s, (picked > 0.5) & (kpos <= qpos))
    acc_s = lax.dot_general(e_s.astype(BF16), vsel_ref[0, 0], NT_DIMS, preferred_element_type=F32)

    acc_w, l_w = _window_branch(q, qpos, kwin_ref[0, 0], vwin_ref[0, 0], win_pos0)
    _gate_and_store(o_ref, gate_ref[0], o_c, acc_s, l_s, acc_w, l_w, nq)


def nsa_attend_decode(q, gates, kc_t, vc_t, ovt, ind, ksel_t, vsel_t, kwin_t, vwin_t, *, nc, ns, q_pos0, win_pos0):
    b, _, _, nq, _ = q.shape
    ncp = kc_t.shape[3]
    nsp = ovt.shape[0]
    l_keys = ksel_t.shape[3]
    per_bg = lambda cols: pl.BlockSpec((1, 1, HEAD_DIM, cols), lambda bi, g: (bi, g, 0, 0))
    kernel = functools.partial(_nsa_decode_kernel, nq=nq, nc=nc, ns=ns, q_pos0=q_pos0, win_pos0=win_pos0)
    return pl.pallas_call(
        kernel,
        grid=(b, NSA_GROUPS),
        in_specs=[
            pl.BlockSpec((1, 1, HEADS_PER_GROUP, nq, HEAD_DIM), lambda bi, g: (bi, g, 0, 0, 0)),
            pl.BlockSpec((1, nq, LANES), lambda bi, g: (bi, 0, g)),
            per_bg(ncp), per_bg(ncp),
            pl.BlockSpec((nsp, ncp), lambda bi, g: (0, 0)),
            pl.BlockSpec((nsp, l_keys), lambda bi, g: (0, 0)),
            per_bg(l_keys), per_bg(l_keys), per_bg(WIN_SPAN), per_bg(WIN_SPAN),
        ],
        out_specs=pl.BlockSpec((1, nq, GROUP_LANES), lambda bi, g: (bi, 0, g)),
        out_shape=jax.ShapeDtypeStruct((b, nq, NSA_HEADS * HEAD_DIM), BF16),
        compiler_params=_params("parallel", "parallel"),
        name="nsa_attend_decode",
    )(q, gates, kc_t, vc_t, ovt, ind, ksel_t, vsel_t, kwin_t, vwin_t)


def _gather_kernel(pt_ref, *refs, n_steps):
    page_refs = refs[:GATHER_PAGES]
    new_ref, cmp_ref, ksel_ref, vsel_ref, scr = refs[GATHER_PAGES:]
    p = pl.program_id(1)
    page = page_refs[0].shape[2]
    k_lo = 2 * GROUP_LANES
    v_lo = 3 * GROUP_LANES

    @pl.when(p < n_steps - 1)
    def _():
        for k, page_ref in enumerate(page_refs):
            x = page_ref[0]
            cols = slice(k * page, (k + 1) * page)
            xt = jnp.transpose(x[0:2 * GROUP_LANES, :])
            for c in range(scr.shape[0]):
                scr[c, cols, :] = xt[:, c * LANES:(c + 1) * LANES]
            for g in range(NSA_GROUPS):
                ksel_ref[0, g, :, cols] = x[k_lo + g * HEAD_DIM:k_lo + (g + 1) * HEAD_DIM, :].astype(BF16)
                vsel_ref[0, g, :, cols] = x[v_lo + g * HEAD_DIM:v_lo + (g + 1) * HEAD_DIM, :].astype(BF16)
        _store_chunk_major(cmp_ref, scr, GATHER_PAGES * page)

    @pl.when(p == n_steps - 1)
    def _():
        new = new_ref[0]
        padded = jnp.concatenate([new, jnp.zeros((GATHER_PAGES * page - new.shape[0], new.shape[1]), F32)], axis=0)
        kt = jnp.transpose(padded[:, k_lo:k_lo + GROUP_LANES])
        vt = jnp.transpose(padded[:, v_lo:v_lo + GROUP_LANES])
        for g in range(NSA_GROUPS):
            ksel_ref[0, g] = kt[g * HEAD_DIM:(g + 1) * HEAD_DIM, :].astype(BF16)
            vsel_ref[0, g] = vt[g * HEAD_DIM:(g + 1) * HEAD_DIM, :].astype(BF16)


def gather_past(page_table, cache_t, new_rows):
    db, n_pages = page_table.shape
    page = cache_t.shape[2]
    dq = new_rows.shape[1]
    past = n_pages * page
    n_full = n_pages // GATHER_PAGES
    n_steps = n_full + 1
    step_keys = GATHER_PAGES * page
    chunk_lanes = CMP_STRIDE * 2 * GROUP_LANES

    def page_spec(k):
        return pl.BlockSpec(
            (1, 4 * GROUP_LANES, page),
            lambda b, p, pt: (pt[b * n_pages + jnp.minimum(p, n_full - 1) * GATHER_PAGES + k], 0, 0))

    grid_spec = pltpu.PrefetchScalarGridSpec(
        num_scalar_prefetch=1,
        grid=(db, n_steps),
        in_specs=[page_spec(k) for k in range(GATHER_PAGES)] + [
            pl.BlockSpec((1, dq, 4 * GROUP_LANES), lambda b, p, pt: (b, 0, 0))],
        out_specs=[
            pl.BlockSpec((step_keys // CMP_STRIDE, chunk_lanes),
                         lambda b, p, pt: (b * n_full + jnp.minimum(p, n_full - 1), 0)),
            pl.BlockSpec((1, NSA_GROUPS, HEAD_DIM, step_keys), lambda b, p, pt: (b, 0, 0, p)),
            pl.BlockSpec((1, NSA_GROUPS, HEAD_DIM, step_keys), lambda b, p, pt: (b, 0, 0, p)),
        ],
        scratch_shapes=[pltpu.VMEM((2 * GROUP_LANES // LANES, step_keys, LANES), F32)],
    )
    return pl.pallas_call(
        functools.partial(_gather_kernel, n_steps=n_steps),
        grid_spec=grid_spec,
        out_shape=[
            jax.ShapeDtypeStruct((db * past // CMP_STRIDE, chunk_lanes), BF16),
            jax.ShapeDtypeStruct((db, NSA_GROUPS, HEAD_DIM, past + step_keys), BF16),
            jax.ShapeDtypeStruct((db, NSA_GROUPS, HEAD_DIM, past + step_keys), BF16),
        ],
        compiler_params=_params("parallel", "arbitrary"),
        name="gather_past",
    )(page_table.reshape(-1), *([cache_t] * GATHER_PAGES), new_rows)


def _win_assemble_kernel(cache_ref, new_ref, win_ref, kwin_ref, vwin_ref):
    old = cache_ref[0]
    new = new_ref[0]
    buf = old.shape[0]
    dq = new.shape[0]
    win_ref[0, 0:buf - dq, :] = old[dq:, :]
    win_ref[0, buf - dq:buf, :] = new
    old_t = jnp.transpose(old)
    tail = jnp.concatenate([new, jnp.zeros((WIN_SPAN - buf - dq, new.shape[1]), F32)], axis=0)
    tail_t = jnp.transpose(tail)
    for ref, off in ((kwin_ref, 0), (vwin_ref, GROUP_LANES)):
        for g in range(NSA_GROUPS):
            rows = slice(off + g * HEAD_DIM, off + (g + 1) * HEAD_DIM)
            ref[0, g] = jnp.concatenate([old_t[rows, :], tail_t[rows, :]], axis=1).astype(BF16)


def win_assemble(cache_win, new_win):
    db, buf, width = cache_win.shape
    dq = new_win.shape[1]
    return pl.pallas_call(
        _win_assemble_kernel,
        grid=(db,),
        in_specs=[
            pl.BlockSpec((1, buf, width), lambda b: (b, 0, 0)),
            pl.BlockSpec((1, dq, width), lambda b: (b, 0, 0)),
        ],
        out_specs=[
            pl.BlockSpec((1, buf, width), lambda b: (b, 0, 0)),
            pl.BlockSpec((1, NSA_GROUPS, HEAD_DIM, WIN_SPAN), lambda b: (b, 0, 0, 0)),
            pl.BlockSpec((1, NSA_GROUPS, HEAD_DIM, WIN_SPAN), lambda b: (b, 0, 0, 0)),
        ],
        out_shape=[
            jax.ShapeDtypeStruct((db, buf, width), F32),
            jax.ShapeDtypeStruct((db, NSA_GROUPS, HEAD_DIM, WIN_SPAN), BF16),
            jax.ShapeDtypeStruct((db, NSA_GROUPS, HEAD_DIM, WIN_SPAN), BF16),
        ],
        compiler_params=_params("parallel"),
        name="win_assemble",
    )(cache_win, new_win)


def _rope_angles(pos, half):
    inv = ROPE_THETA ** (-jnp.arange(half, dtype=F32) / half)
    ang = pos.astype(F32)[:, None] * inv[None, :]
    return jnp.cos(ang), jnp.sin(ang)


def _rope_tables_head64(pos):
    cos, sin = _rope_angles(pos, HEAD_DIM // 2)
    return jnp.concatenate([cos] * 4, axis=1), jnp.concatenate([-sin, sin] * 2, axis=1)


def _block_diag_groups(w):
    eye = jnp.eye(NSA_GROUPS, dtype=w.dtype)
    out = jnp.einsum("gh,...dn->...gdhn", eye, w)
    return out.reshape(*w.shape[:-2], GROUP_LANES, NSA_GROUPS * w.shape[-1])


def _compress_weights(cmp_w1, cmp_b1, cmp_w2, cmp_pe):
    r = CMP_BLOCK // CMP_STRIDE
    w1 = cmp_w1.reshape(2, r, CMP_STRIDE, HEAD_DIM, HEAD_DIM)
    bd = _block_diag_groups(w1)
    bd = jnp.concatenate([bd[:, 0], bd[:, 1]], axis=-1).astype(BF16)
    pe_rows = jnp.broadcast_to(cmp_pe.reshape(2, 1, CMP_BLOCK * HEAD_DIM), (2, 8, CMP_BLOCK * HEAD_DIM)).astype(BF16)
    w1_tiled = jnp.tile(cmp_w1, (1, 1, NSA_GROUPS)).astype(BF16)
    b1_tiled = jnp.tile(cmp_b1, (1, NSA_GROUPS)).reshape(2, 1, GROUP_LANES)
    w2_bd = _block_diag_groups(cmp_w2).astype(BF16)
    return bd[0], bd[1], pe_rows, w1_tiled, b1_tiled, w2_bd


def _overlap_table(nc, ncp, ns, nsp):
    ci = np.arange(ncp)[None, :]
    sj = np.arange(nsp)[:, None]
    overlap_t = ((ci * CMP_STRIDE < (sj + 1) * SEL_BLOCK) & (ci * CMP_STRIDE + CMP_BLOCK > sj * SEL_BLOCK)
                 & (ci < nc) & (sj < ns))
    return jnp.asarray(overlap_t, BF16)


def _block_membership(length, nsp):
    return jnp.asarray((np.arange(length)[None, :] // SEL_BLOCK) == np.arange(nsp)[:, None], BF16)


def _round_up(x, m):
    return -(-x // m) * m


def _trunk(x, pos, ret_s0, past, w):
    (ffn_norm, ffn_w_in, ffn_w_out, ret_norm, ret_w_in, ret_w_out, kv_norm, kv_w, k_norm_tiled,
     cmp_weights, nsa_norm, nsa_wq, nsa_wg, q_norm_tiled, nsa_w_out, ones_bd) = w
    b, t, _ = x.shape
    n = b * t
    prompt = past is None
    xf = x.reshape(n, D_MODEL)
    pos_rows = jnp.tile(pos, b) if not prompt else pos
    ret_cos, ret_sin = _rope_angles(pos_rows, RET_DK // 2)
    cos64, sin64 = _rope_tables_head64(pos)
    ret_states = []
    rows = win = attend = None
    for layer in range(DEPTH):
        if layer == N_A_LAYERS:
            xs = xf.reshape(b, t, D_MODEL)
            if prompt:
                rows, win, cmp_tok, ksel, vsel, kwin, vwin = kv_rows(
                    xs, kv_norm, kv_w, k_norm_tiled, cos64, sin64, ones_bd, aux=True)
                length = t
                new_win = win[:, t - min(WINDOW, t):]
            else:
                page_table, cache_t, cache_win = past
                rows, win = kv_rows(xs, kv_norm, kv_w, k_norm_tiled, cos64, sin64, ones_bd, aux=False)
                cmp_tok, ksel, vsel = gather_past(page_table, cache_t, rows)
                new_win, kwin, vwin = win_assemble(cache_win, win)
                past_len = page_table.shape[1] * cache_t.shape[2]
                length = past_len + t
            nc = (length - CMP_BLOCK) // CMP_STRIDE + 1
            n_chunk_rows = nc + CMP_BLOCK // CMP_STRIDE - 1
            ns = -(-length // SEL_BLOCK)
            wk_bd, wv_bd, pe_rows, w1_tiled, b1_tiled, w2_bd = cmp_weights
            assert cmp_tok.shape[0] == b * n_chunk_rows
            partial = cmp_partial(cmp_tok, wk_bd, wv_bd).reshape(b, n_chunk_rows, 4 * GROUP_LANES)
            c_end = jnp.arange(n_chunk_rows, dtype=I32) * CMP_STRIDE + (CMP_BLOCK - 1)
            cos_c, sin_c = _rope_tables_head64(c_end)
            kc_t, vc_t = cmp_combine(partial, pe_rows, w1_tiled, b1_tiled, w2_bd, k_norm_tiled, cos_c, sin_c, ones_bd)
            if prompt:
                assert ns <= MASK_ROWS and t % PROMPT_NQ == 0
                ovt = _overlap_table(nc, n_chunk_rows, ns, MASK_ROWS)
                attend = functools.partial(nsa_attend_prompt, kc_t=kc_t, vc_t=vc_t, ovt=ovt, ksel_t=ksel, vsel_t=vsel,
                                           kwin_t=kwin, vwin_t=vwin, nc=nc, ns=ns)
            else:
                nsp = _round_up(ns, 16)
                ovt = _overlap_table(nc, n_chunk_rows, ns, nsp)
                ind = _block_membership(ksel.shape[3], nsp)
                attend = functools.partial(nsa_attend_decode, kc_t=kc_t, vc_t=vc_t, ovt=ovt, ind=ind, ksel_t=ksel,
                                           vsel_t=vsel, kwin_t=kwin, vwin_t=vwin, nc=nc, ns=ns,
                                           q_pos0=int(past_len), win_pos0=int(past_len - cache_win.shape[1]))
        xf = ffn_half(xf, ffn_norm[layer, 0], ffn_w_in[layer, 0], ffn_w_out[layer, 0])
        if layer < N_A_LAYERS:
            qkvg = ret_inproj(xf, ret_norm[layer], ret_w_in[layer], ret_cos, ret_sin)
            s0 = None if ret_s0 is None else ret_s0[layer]
            gated, s_fin = ret_core(qkvg.reshape(b, t, 6 * D_MODEL), s0)
            ret_states.append(s_fin)
            xf = mm_residual(gated.reshape(n, 2 * D_MODEL), ret_w_out[layer], xf)
        else:
            j = layer - N_A_LAYERS
            q, gates = nsa_q(xf.reshape(b, t, D_MODEL), nsa_norm[j], nsa_wq[j], nsa_wg[j], q_norm_tiled[j],
                             cos64, sin64, ones_bd)
            o = attend(q, gates)
            xf = mm_residual(o.reshape(n, NSA_HEADS * HEAD_DIM), nsa_w_out[j], xf)
        xf = ffn_half(xf, ffn_norm[layer, 1], ffn_w_in[layer, 1], ffn_w_out[layer, 1])
    return xf.reshape(b, t, D_MODEL), jnp.stack(ret_states), rows, new_win


def kernel(x_prompt, x_sample, state_ret, cache_kv, cache_win, page_table, ffn_norm, ffn_w_in, ffn_w_out, ret_norm,
           ret_w_in, ret_w_out, kv_norm, kv_w, k_norm, cmp_w1, cmp_b1, cmp_w2, cmp_pe, nsa_norm, nsa_w_in, q_norm,
           nsa_w_out):
    b, t, _ = x_prompt.shape
    db, dq, _ = x_sample.shape
    n_phys, page = cache_kv.shape[:2]
    past_len = page_table.shape[1] * page
    n_q_cols = NSA_HEADS * HEAD_DIM

    gate_w = nsa_w_in[:, :, n_q_cols:].reshape(-1, D_MODEL, NSA_GROUPS, N_GATES)
    gate_w = jnp.pad(gate_w, ((0, 0), (0, 0), (0, 0), (0, LANES - N_GATES))).reshape(-1, D_MODEL, NSA_GROUPS * LANES)
    eye = np.arange(GROUP_LANES)
    ones_bd = jnp.asarray((eye[:, None] // HEAD_DIM) == (eye[None, :] // HEAD_DIM), BF16)
    w = (ffn_norm, ffn_w_in.astype(BF16), ffn_w_out.astype(BF16), ret_norm, ret_w_in.astype(BF16),
         ret_w_out.astype(BF16), kv_norm, kv_w.astype(BF16), jnp.tile(k_norm, (1, NSA_GROUPS)),
         _compress_weights(cmp_w1, cmp_b1, cmp_w2, cmp_pe), nsa_norm, nsa_w_in[:, :, :n_q_cols].astype(BF16),
         gate_w.astype(BF16), jnp.tile(q_norm, (1, NSA_GROUPS)).reshape(-1, 1, GROUP_LANES),
         nsa_w_out.astype(BF16), ones_bd)

    pos_p = jnp.arange(t, dtype=I32)
    pos_s = past_len + jnp.arange(dq, dtype=I32)
    y_p, ret_p, rows_p, win_p = _trunk(x_prompt, pos_p, None, None, w)
    cache_t = jnp.transpose(cache_kv, (0, 2, 3, 4, 1)).reshape(n_phys, 4 * GROUP_LANES, page)
    cwin = cache_win.reshape(db, cache_win.shape[1], 2 * GROUP_LANES)
    y_s, ret_s, rows_s, win_s = _trunk(x_sample, pos_s, state_ret, (page_table, cache_t, cwin), w)
    kv_shape = (4, NSA_GROUPS, HEAD_DIM)
    win_shape = (2, NSA_GROUPS, HEAD_DIM)
    return (y_p, y_s, ret_p.astype(state_ret.dtype), ret_s.astype(state_ret.dtype),
            rows_p.reshape(b, t, *kv_shape), rows_s.reshape(db, dq, *kv_shape),
            win_p.reshape(b, win_p.shape[1], *win_shape), win_s.reshape(db, win_s.shape[1], *win_shape))
```

```python
import functools

import jax
import jax.numpy as jnp
import numpy as np
from jax import lax
from jax.experimental import pallas as pl
from jax.experimental.pallas import tpu as pltpu

F32 = jnp.float32
BF16 = jnp.bfloat16
I32 = jnp.int32

D_MODEL = 1024
DEPTH = 4
N_A_LAYERS = DEPTH // 2
RET_HEADS = 4
RET_DK = D_MODEL // RET_HEADS
RET_DV = 2 * D_MODEL // RET_HEADS
RET_CHUNK = 128
NSA_HEADS = 16
NSA_GROUPS = 4
HEADS_PER_GROUP = NSA_HEADS // NSA_GROUPS
HEAD_DIM = D_MODEL // NSA_HEADS
CMP_BLOCK = 32
CMP_STRIDE = 16
SEL_BLOCK = 64
N_SELECT = 16
N_LOCAL = 2
WINDOW = 512
D_FF = 2816
ROPE_THETA = 10000.0
EPS = 1e-6
NEG = -1e30
BIG = 1e9
N_GATES = 3 * HEADS_PER_GROUP
GROUP_LANES = NSA_GROUPS * HEAD_DIM
PROMPT_NQ = 128
WIN_SPAN = WINDOW + PROMPT_NQ
MASK_ROWS = 64
MASK_BIAS = -(2.0 ** 100)
GATHER_PAGES = 4
LOG2E = 1.4426950408889634

VMEM_LIMIT_BYTES = 56 * 1024 * 1024
LANES = 128

NT_DIMS = (((1,), (1,)), ((), ()))
TN_DIMS = (((0,), (0,)), ((), ()))


def _params(*semantics):
    return pltpu.CompilerParams(dimension_semantics=semantics, vmem_limit_bytes=VMEM_LIMIT_BYTES)


def _rms(x, gain):
    ms = jnp.mean(x * x, axis=-1, keepdims=True)
    return x * lax.rsqrt(ms + EPS) * gain


def _seg_rms(y, ones_bd, gain):
    sq = y * y
    hi = sq.astype(BF16)
    lo = (sq - hi.astype(F32)).astype(BF16)
    ss = jnp.dot(hi, ones_bd, preferred_element_type=F32) + jnp.dot(lo, ones_bd, preferred_element_type=F32)
    return y * lax.rsqrt(ss * (1.0 / HEAD_DIM) + EPS) * gain


def _rope64(x, cos, sin_signed):
    lane = lax.broadcasted_iota(I32, x.shape, 1)
    first_half = (lane % HEAD_DIM) < (HEAD_DIM // 2)
    rot = jnp.where(first_half, pltpu.roll(x, LANES - HEAD_DIM // 2, 1), pltpu.roll(x, HEAD_DIM // 2, 1))
    return x * cos + rot * sin_signed


def _exp_rows(s, ok):
    s = jnp.where(ok, s, NEG)
    m = jnp.max(s, axis=-1, keepdims=True)
    e = jnp.exp2(s - m)
    return e, jnp.sum(e, axis=-1, keepdims=True)


def _ffn_kernel(x_ref, g_ref, wi_ref, wo_ref, o_ref):
    x = x_ref[...]
    xn = _rms(x, g_ref[...]).astype(BF16)
    a = jnp.dot(xn, wi_ref[:, 0:D_FF], preferred_element_type=F32)
    b = jnp.dot(xn, wi_ref[:, D_FF:2 * D_FF], preferred_element_type=F32)
    h = (a * jax.nn.sigmoid(a) * b).astype(BF16)
    o_ref[...] = x + 0.5 * jnp.dot(h, wo_ref[...], preferred_element_type=F32)


def ffn_half(x, gain, w_in, w_out, layer, half):
    n = x.shape[0]
    tm = min(n, 512)
    resident = pl.Buffered(1)
    return pl.pallas_call(
        _ffn_kernel,
        grid=(n // tm,),
        in_specs=[
            pl.BlockSpec((tm, D_MODEL), lambda i: (i, 0)),
            pl.BlockSpec((1, D_MODEL), lambda i: (0, 0)),
            pl.BlockSpec((None, None, D_MODEL, 2 * D_FF), lambda i: (layer, half, 0, 0), pipeline_mode=resident),
            pl.BlockSpec((None, None, D_FF, D_MODEL), lambda i: (layer, half, 0, 0), pipeline_mode=resident),
        ],
        out_specs=pl.BlockSpec((tm, D_MODEL), lambda i: (i, 0)),
        out_shape=jax.ShapeDtypeStruct((n, D_MODEL), F32),
        compiler_params=_params("parallel"),
        name="ffn_half",
    )(x, gain.reshape(1, D_MODEL), w_in, w_out)


def _mm_res_kernel(a_ref, w_ref, x_ref, o_ref):
    o_ref[...] = x_ref[...] + jnp.dot(a_ref[...], w_ref[...], preferred_element_type=F32)


def mm_residual(a, w, layer, x):
    n, k = a.shape
    tm = min(n, 512)
    return pl.pallas_call(
        _mm_res_kernel,
        grid=(n // tm,),
        in_specs=[
            pl.BlockSpec((tm, k), lambda i: (i, 0)),
            pl.BlockSpec((None, k, D_MODEL), lambda i: (layer, 0, 0)),
            pl.BlockSpec((tm, D_MODEL), lambda i: (i, 0)),
        ],
        out_specs=pl.BlockSpec((tm, D_MODEL), lambda i: (i, 0)),
        out_shape=jax.ShapeDtypeStruct((n, D_MODEL), F32),
        compiler_params=_params("parallel"),
        name="mm_residual",
    )(a, w, x)


def _ret_inproj_kernel(x_ref, g_ref, w_ref, cos_ref, sin_ref, o_ref):
    xn = _rms(x_ref[...], g_ref[...]).astype(BF16)
    c = cos_ref[...]
    s = sin_ref[...]
    half = RET_DK // 2
    for h in range(2 * RET_HEADS):
        lo = h * RET_DK
        y = jnp.dot(xn, w_ref[:, lo:lo + RET_DK], preferred_element_type=F32)
        scale = 1.0 if h < RET_HEADS else RET_DK ** -0.5
        x1 = y[:, :half]
        x2 = y[:, half:]
        o_ref[:, lo:lo + half] = ((x1 * c - x2 * s) * scale).astype(BF16)
        o_ref[:, lo + half:lo + RET_DK] = ((x1 * s + x2 * c) * scale).astype(BF16)
    for h in range(2 * RET_HEADS):
        lo = 2 * D_MODEL + h * RET_DV
        o_ref[:, lo:lo + RET_DV] = jnp.dot(xn, w_ref[:, lo:lo + RET_DV], preferred_element_type=F32).astype(BF16)


def ret_inproj(x, gain, w, layer, cos, sin):
    n = x.shape[0]
    p = cos.shape[0]
    tm = min(n, 512, p)
    n_out = w.shape[2]
    tab_blocks = p // tm
    return pl.pallas_call(
        _ret_inproj_kernel,
        grid=(n // tm,),
        in_specs=[
            pl.BlockSpec((tm, D_MODEL), lambda i: (i, 0)),
            pl.BlockSpec((1, D_MODEL), lambda i: (0, 0)),
            pl.BlockSpec((None, D_MODEL, n_out), lambda i: (layer, 0, 0), pipeline_mode=pl.Buffered(1)),
            pl.BlockSpec((tm, RET_DK // 2), lambda i: (i % tab_blocks, 0)),
            pl.BlockSpec((tm, RET_DK // 2), lambda i: (i % tab_blocks, 0)),
        ],
        out_specs=pl.BlockSpec((tm, n_out), lambda i: (i, 0)),
        out_shape=jax.ShapeDtypeStruct((n, n_out), BF16),
        compiler_params=_params("parallel"),
        name="ret_inproj",
    )(x, gain.reshape(1, D_MODEL), w, cos, sin)


def _ret_core_kernel(*refs, chunk, n_inner, hps, has_s0):
    if has_s0:
        (q_ref, k_ref, v_ref, g_ref, dm_ref, qd_ref, kd_ref, sd_ref, s0_ref, o_ref, so_ref, s_scr) = refs
    else:
        (q_ref, k_ref, v_ref, g_ref, dm_ref, qd_ref, kd_ref, sd_ref, o_ref, so_ref, s_scr) = refs
    t = pl.program_id(2)

    @pl.when(t == 0)
    def _():
        if has_s0:
            s_scr[...] = s0_ref[0]
        else:
            s_scr[...] = jnp.zeros_like(s_scr)

    for c in range(n_inner):
        rows = slice(c * chunk, (c + 1) * chunk)
        for h in range(hps):
            qk_cols = slice(h * RET_DK, (h + 1) * RET_DK)
            v_cols = slice(h * RET_DV, (h + 1) * RET_DV)
            q = q_ref[0, rows, qk_cols]
            k = k_ref[0, rows, qk_cols]
            v = v_ref[0, rows, v_cols]
            g = g_ref[0, rows, v_cols].astype(F32)
            s = s_scr[h]
            scores = lax.dot_general(q, k, NT_DIMS, preferred_element_type=F32) * dm_ref[h]
            intra = jnp.dot(scores.astype(BF16), v, preferred_element_type=F32)
            cross = jnp.dot((q.astype(F32) * qd_ref[h]).astype(BF16), s.astype(BF16), preferred_element_type=F32)
            o = intra + cross
            kv = lax.dot_general((k.astype(F32) * kd_ref[h]).astype(BF16), v, TN_DIMS, preferred_element_type=F32)
            s_scr[h] = sd_ref[h, 0:1, 0:1] * s + kv
            mu = jnp.mean(o, axis=-1, keepdims=True)
            d = o - mu
            var = jnp.mean(d * d, axis=-1, keepdims=True)
            on = d * lax.rsqrt(var + EPS)
            o_ref[0, rows, v_cols] = (g * jax.nn.sigmoid(g) * on).astype(BF16)

    @pl.when(t == pl.num_programs(2) - 1)
    def _():
        so_ref[0] = s_scr[...]


def _decay_tables(chunk):
    lg = jnp.log(1.0 - 2.0 ** (-5.0 - jnp.arange(RET_HEADS, dtype=F32)))
    idx = jnp.arange(chunk, dtype=F32)
    rel = idx[:, None] - idx[None, :]
    dmat = jnp.where(rel >= 0, jnp.exp(jnp.maximum(rel, 0.0)[None] * lg[:, None, None]), 0.0)
    qdec = jnp.exp((idx + 1.0)[None, :] * lg[:, None])
    kdec = jnp.exp((chunk - 1.0 - idx)[None, :] * lg[:, None])
    sdec = jnp.exp(chunk * lg)
    qdec = jnp.broadcast_to(qdec[:, :, None], (RET_HEADS, chunk, RET_DK))
    kdec = jnp.broadcast_to(kdec[:, :, None], (RET_HEADS, chunk, RET_DK))
    sdec = jnp.broadcast_to(sdec[:, None, None], (RET_HEADS, 8, LANES))
    return dmat, qdec, kdec, sdec


def ret_core(qkvg, s0_all, layer):
    b, t, _ = qkvg.shape
    chunk = RET_CHUNK if t % RET_CHUNK == 0 else t
    tb = min(t, 4 * chunk)
    n_inner = tb // chunk
    hps = RET_HEADS if t < RET_CHUNK else 1
    nh = RET_HEADS // hps
    dmat, qdec, kdec, sdec = _decay_tables(chunk)
    in_specs = [
        pl.BlockSpec((1, tb, hps * RET_DK), lambda bi, h, ti: (bi, ti, h)),
        pl.BlockSpec((1, tb, hps * RET_DK), lambda bi, h, ti: (bi, ti, nh + h)),
        pl.BlockSpec((1, tb, hps * RET_DV), lambda bi, h, ti: (bi, ti, nh + h)),
        pl.BlockSpec((1, tb, hps * RET_DV), lambda bi, h, ti: (bi, ti, 2 * nh + h)),
        pl.BlockSpec((hps, chunk, chunk), lambda bi, h, ti: (h, 0, 0)),
        pl.BlockSpec((hps, chunk, RET_DK), lambda bi, h, ti: (h, 0, 0)),
        pl.BlockSpec((hps, chunk, RET_DK), lambda bi, h, ti: (h, 0, 0)),
        pl.BlockSpec((hps, 8, LANES), lambda bi, h, ti: (h, 0, 0)),
    ]
    args = [qkvg, qkvg, qkvg, qkvg, dmat, qdec, kdec, sdec]
    if s0_all is not None:
        in_specs.append(pl.BlockSpec((None, 1, hps, RET_DK, RET_DV), lambda bi, h, ti: (layer, bi, h, 0, 0)))
        args.append(s0_all)
    return pl.pallas_call(
        functools.partial(_ret_core_kernel, chunk=chunk, n_inner=n_inner, hps=hps, has_s0=s0_all is not None),
        grid=(b, nh, t // tb),
        in_specs=in_specs,
        out_specs=[
            pl.BlockSpec((1, tb, hps * RET_DV), lambda bi, h, ti: (bi, ti, h)),
            pl.BlockSpec((1, hps, RET_DK, RET_DV), lambda bi, h, ti: (bi, h, 0, 0)),
        ],
        out_shape=[
            jax.ShapeDtypeStruct((b, t, 2 * D_MODEL), BF16),
            jax.ShapeDtypeStruct((b, RET_HEADS, RET_DK, RET_DV), F32),
        ],
        scratch_shapes=[pltpu.VMEM((hps, RET_DK, RET_DV), F32)],
        compiler_params=_params("parallel", "parallel", "arbitrary"),
        name="ret_core",
    )(*args)


def _store_chunk_major(cmp_ref, scr, n_rows):
    n_planes = scr.shape[0]
    for p in range(CMP_STRIDE):
        for c in range(n_planes):
            lo = p * n_planes * LANES + c * LANES
            cmp_ref[:, lo:lo + LANES] = scr[c, pl.ds(p, n_rows // CMP_STRIDE, stride=CMP_STRIDE), :].astype(BF16)


def _kv_rows_kernel(x_ref, g_ref, w_ref, kn_ref, cos_ref, sin_ref, ones_ref, rows_ref, win_ref, *aux_refs):
    xn = _rms(x_ref[0], g_ref[...]).astype(BF16)
    y = jnp.dot(xn, w_ref[...], preferred_element_type=F32)
    tm = y.shape[0]
    cos = cos_ref[...]
    sin = sin_ref[...]
    ones_bd = ones_ref[...]
    slot = lambda s: y[:, s * GROUP_LANES:(s + 1) * GROUP_LANES]

    def norm_rope(v, gain):
        vn = _seg_rms(v, ones_bd, gain)
        return jnp.concatenate([_rope64(vn[:, :LANES], cos, sin), _rope64(vn[:, LANES:], cos, sin)], axis=1)

    k_slc = norm_rope(slot(2), kn_ref[1:2, :])
    k_win = norm_rope(slot(4), kn_ref[2:3, :])
    rows_ref[0, :, 0:2 * GROUP_LANES] = y[:, 0:2 * GROUP_LANES]
    rows_ref[0, :, 2 * GROUP_LANES:3 * GROUP_LANES] = k_slc
    rows_ref[0, :, 3 * GROUP_LANES:4 * GROUP_LANES] = slot(3)
    win_ref[0, :, 0:GROUP_LANES] = k_win
    win_ref[0, :, GROUP_LANES:2 * GROUP_LANES] = slot(5)
    if aux_refs:
        cmp_ref, ksel_ref, vsel_ref, kwin_ref, vwin_ref, scr = aux_refs
        for c in range(scr.shape[0]):
            scr[c] = y[:, c * LANES:(c + 1) * LANES]
        _store_chunk_major(cmp_ref, scr, tm)
        for ref, val in ((ksel_ref, k_slc), (vsel_ref, slot(3)), (kwin_ref, k_win), (vwin_ref, slot(5))):
            vt = jnp.transpose(val)
            for g in range(NSA_GROUPS):
                ref[0, g, 0:HEAD_DIM, :] = vt[g * HEAD_DIM:(g + 1) * HEAD_DIM, :].astype(BF16)
        key_blk = (pl.program_id(1) * tm + lax.broadcasted_iota(I32, (MASK_ROWS, tm), 1)) // SEL_BLOCK
        mask_rows = jnp.where(key_blk == lax.broadcasted_iota(I32, (MASK_ROWS, tm), 0), MASK_BIAS, 0.0).astype(BF16)
        for g in range(NSA_GROUPS):
            ksel_ref[0, g, HEAD_DIM:HEAD_DIM + MASK_ROWS, :] = mask_rows


def kv_rows(x, gain, w, k_norm_tiled, cos, sin, ones_bd, aux):
    b, t, _ = x.shape
    tm = min(t, 512)
    n_kv = w.shape[1]
    nt = t // tm
    out_specs = [
        pl.BlockSpec((1, tm, 4 * GROUP_LANES), lambda bi, ti: (bi, ti, 0)),
        pl.BlockSpec((1, tm, 2 * GROUP_LANES), lambda bi, ti: (bi, ti, 0)),
    ]
    out_shape = [
        jax.ShapeDtypeStruct((b, t, 4 * GROUP_LANES), F32),
        jax.ShapeDtypeStruct((b, t, 2 * GROUP_LANES), F32),
    ]
    scratch = []
    if aux:
        chunk_lanes = CMP_STRIDE * 2 * GROUP_LANES
        out_specs.append(pl.BlockSpec((tm // CMP_STRIDE, chunk_lanes), lambda bi, ti: (bi * nt + ti, 0)))
        out_shape.append(jax.ShapeDtypeStruct((b * t // CMP_STRIDE, chunk_lanes), BF16))
        for rows in (HEAD_DIM + MASK_ROWS, HEAD_DIM, HEAD_DIM, HEAD_DIM):
            out_specs.append(pl.BlockSpec((1, NSA_GROUPS, rows, tm), lambda bi, ti: (bi, 0, 0, ti)))
            out_shape.append(jax.ShapeDtypeStruct((b, NSA_GROUPS, rows, t), BF16))
        scratch.append(pltpu.VMEM((2 * GROUP_LANES // LANES, tm, LANES), F32))
    return pl.pallas_call(
        _kv_rows_kernel,
        grid=(b, nt),
        in_specs=[
            pl.BlockSpec((1, tm, D_MODEL), lambda bi, ti: (bi, ti, 0)),
            pl.BlockSpec((1, D_MODEL), lambda bi, ti: (0, 0)),
            pl.BlockSpec((D_MODEL, n_kv), lambda bi, ti: (0, 0)),
            pl.BlockSpec((3, GROUP_LANES), lambda bi, ti: (0, 0)),
            pl.BlockSpec((tm, LANES), lambda bi, ti: (ti, 0)),
            pl.BlockSpec((tm, LANES), lambda bi, ti: (ti, 0)),
            pl.BlockSpec((GROUP_LANES, GROUP_LANES), lambda bi, ti: (0, 0)),
        ],
        out_specs=out_specs,
        out_shape=out_shape,
        scratch_shapes=scratch,
        compiler_params=_params("parallel", "parallel"),
        name="kv_rows",
    )(x, gain.reshape(1, D_MODEL), w, k_norm_tiled, cos, sin, ones_bd)


def _cmp_partial_kernel(x_ref, wk_ref, wv_ref, o_ref):
    acc_k = jnp.zeros((x_ref.shape[0], 2 * GROUP_LANES), F32)
    acc_v = jnp.zeros((x_ref.shape[0], 2 * GROUP_LANES), F32)
    for p in range(CMP_STRIDE):
        lo = p * 2 * GROUP_LANES
        acc_k += jnp.dot(x_ref[:, lo:lo + GROUP_LANES], wk_ref[p], preferred_element_type=F32)
        acc_v += jnp.dot(x_ref[:, lo + GROUP_LANES:lo + 2 * GROUP_LANES], wv_ref[p], preferred_element_type=F32)
    o_ref[:, 0:2 * GROUP_LANES] = acc_k
    o_ref[:, 2 * GROUP_LANES:4 * GROUP_LANES] = acc_v


def cmp_partial(tok_chunks, wk_bd, wv_bd):
    n, width = tok_chunks.shape
    tm = min(n, 512)
    return pl.pallas_call(
        _cmp_partial_kernel,
        grid=(n // tm,),
        in_specs=[
            pl.BlockSpec((tm, width), lambda i: (i, 0)),
            pl.BlockSpec(wk_bd.shape, lambda i: (0, 0, 0)),
            pl.BlockSpec(wv_bd.shape, lambda i: (0, 0, 0)),
        ],
        out_specs=pl.BlockSpec((tm, 4 * GROUP_LANES), lambda i: (i, 0)),
        out_shape=jax.ShapeDtypeStruct((n, 4 * GROUP_LANES), F32),
        compiler_params=_params("parallel"),
        name="cmp_partial",
    )(tok_chunks, wk_bd, wv_bd)


def _cmp_combine_kernel(a_ref, pe_ref, w1_ref, b1_ref, w2_ref, kn_ref, cos_ref, sin_ref, ones_ref, kc_ref, vc_ref):
    a = a_ref[0]
    n = a.shape[0]
    for t, out_ref in enumerate((kc_ref, vc_ref)):
        first = a[:, 2 * t * GROUP_LANES:(2 * t + 1) * GROUP_LANES]
        second = a[:, (2 * t + 1) * GROUP_LANES:(2 * t + 2) * GROUP_LANES]
        second = pltpu.roll(second, n - 1, 0)
        pe_term = jnp.dot(pe_ref[t], w1_ref[t], preferred_element_type=F32)[0:1]
        h = b1_ref[t] + pe_term + first + second
        y = jnp.dot(jax.nn.gelu(h).astype(BF16), w2_ref[t], preferred_element_type=F32)
        if t == 0:
            y = _seg_rms(y, ones_ref[...], kn_ref[0:1, :])
            y = jnp.concatenate(
                [_rope64(y[:, :LANES], cos_ref[...], sin_ref[...]), _rope64(y[:, LANES:], cos_ref[...], sin_ref[...])],
                axis=1)
        yt = jnp.transpose(y)
        for g in range(NSA_GROUPS):
            out_ref[0, g] = yt[g * HEAD_DIM:(g + 1) * HEAD_DIM, :].astype(BF16)


def cmp_combine(partial, pe_rows, w1_tiled, b1_tiled, w2_bd, k_norm_tiled, cos, sin, ones_bd):
    b, ncp, _ = partial.shape
    full = lambda *shape: pl.BlockSpec(shape, lambda bi: (0,) * len(shape))
    return pl.pallas_call(
        _cmp_combine_kernel,
        grid=(b,),
        in_specs=[
            pl.BlockSpec((1, ncp, 4 * GROUP_LANES), lambda bi: (bi, 0, 0)),
            full(*pe_rows.shape), full(*w1_tiled.shape), full(*b1_tiled.shape), full(*w2_bd.shape),
            full(3, GROUP_LANES), full(ncp, LANES), full(ncp, LANES), full(GROUP_LANES, GROUP_LANES),
        ],
        out_specs=[pl.BlockSpec((1, NSA_GROUPS, HEAD_DIM, ncp), lambda bi: (bi, 0, 0, 0))] * 2,
        out_shape=[jax.ShapeDtypeStruct((b, NSA_GROUPS, HEAD_DIM, ncp), BF16)] * 2,
        compiler_params=_params("parallel"),
        name="cmp_combine",
    )(partial, pe_rows, w1_tiled, b1_tiled, w2_bd, k_norm_tiled, cos, sin, ones_bd)


def _nsa_q_kernel(x_ref, g_ref, wq_ref, wg_ref, qn_ref, cos_ref, sin_ref, ones_ref, q_ref, gate_ref):
    xn = _rms(x_ref[0], g_ref[...]).astype(BF16)
    y = jnp.dot(xn, wq_ref[...], preferred_element_type=F32)
    cos = cos_ref[...]
    sin = sin_ref[...]
    scale = HEAD_DIM ** -0.5 * LOG2E
    for g in range(NSA_GROUPS):
        yg = _seg_rms(y[:, g * GROUP_LANES:(g + 1) * GROUP_LANES], ones_ref[...], qn_ref[...])
        low = lax.broadcasted_iota(I32, (y.shape[0], LANES), 1) < HEAD_DIM
        for half in range(2):
            r = _rope64(yg[:, half * LANES:(half + 1) * LANES], cos, sin) * scale
            q_ref[0, g, 2 * half] = jnp.where(low, r, 0.0)
            q_ref[0, g, 2 * half + 1] = jnp.where(low, pltpu.roll(r, HEAD_DIM, 1), 0.0)
    gates = jnp.dot(xn, wg_ref[...], preferred_element_type=F32)
    gate_ref[0] = jax.nn.sigmoid(gates)


def nsa_q(x, gain, wq, wg, layer, q_norm_tiled, cos, sin, ones_bd):
    b, t, _ = x.shape
    tm = min(t, 512)
    return pl.pallas_call(
        _nsa_q_kernel,
        grid=(b, t // tm),
        in_specs=[
            pl.BlockSpec((1, tm, D_MODEL), lambda bi, ti: (bi, ti, 0)),
            pl.BlockSpec((1, D_MODEL), lambda bi, ti: (0, 0)),
            pl.BlockSpec((None,) + wq.shape[1:], lambda bi, ti: (layer, 0, 0)),
            pl.BlockSpec((None,) + wg.shape[1:], lambda bi, ti: (layer, 0, 0)),
            pl.BlockSpec((1, GROUP_LANES), lambda bi, ti: (0, 0)),
            pl.BlockSpec((tm, LANES), lambda bi, ti: (ti, 0)),
            pl.BlockSpec((tm, LANES), lambda bi, ti: (ti, 0)),
            pl.BlockSpec((GROUP_LANES, GROUP_LANES), lambda bi, ti: (0, 0)),
        ],
        out_specs=[
            pl.BlockSpec((1, NSA_GROUPS, HEADS_PER_GROUP, tm, LANES), lambda bi, ti: (bi, 0, 0, ti, 0)),
            pl.BlockSpec((1, tm, NSA_GROUPS * LANES), lambda bi, ti: (bi, ti, 0)),
        ],
        out_shape=[
            jax.ShapeDtypeStruct((b, NSA_GROUPS, HEADS_PER_GROUP, t, LANES), F32),
            jax.ShapeDtypeStruct((b, t, NSA_GROUPS * LANES), F32),
        ],
        compiler_params=_params("parallel", "parallel"),
        name="nsa_q",
    )(x, gain.reshape(1, D_MODEL), wq, wg, q_norm_tiled, cos, sin, ones_bd)


def _compressed_branch(q, qpos, q_first, kc_t, vc_t, ovt, nq, nc, ns):
    hpg = HEADS_PER_GROUP
    ncp = kc_t.shape[1]
    nsp = ovt.shape[0]
    s_c = jnp.dot(q, kc_t, preferred_element_type=F32)
    cidx = lax.broadcasted_iota(I32, (1, ncp), 1)
    valid_c = (cidx * CMP_STRIDE + (CMP_BLOCK - 1) <= qpos) & (cidx < nc)
    e_c, l_c = _exp_rows(s_c, valid_c)
    p_c = jnp.where(valid_c, e_c * (1.0 / l_c), 0.0)
    o_c = lax.dot_general(p_c.astype(BF16), vc_t, NT_DIMS, preferred_element_type=F32)
    p_sum = p_c[0:nq]
    for hh in range(1, hpg):
        p_sum = p_sum + p_c[hh * nq:(hh + 1) * nq]
    p_hi = p_sum.astype(BF16)
    p_lo = (p_sum - p_hi.astype(F32)).astype(BF16)
    imp_t = (lax.dot_general(ovt, p_hi, NT_DIMS, preferred_element_type=F32)
             + lax.dot_general(ovt, p_lo, NT_DIMS, preferred_element_type=F32))
    sidx = lax.broadcasted_iota(I32, (nsp, 1), 0)
    q_blk = (q_first + lax.broadcasted_iota(I32, (1, nq), 1)) // SEL_BLOCK
    valid_s = (sidx <= q_blk) & (sidx < ns)
    forced = (sidx == 0) | (valid_s & (q_blk - sidx < N_LOCAL))
    score_t = jnp.where(forced, BIG, jnp.where(valid_s, imp_t, NEG))
    return o_c, score_t, valid_s


def _window_branch(q, qpos, kw_t, vw_t, kpos0):
    s_w = jnp.dot(q, kw_t, preferred_element_type=F32)
    kpos_w = kpos0 + lax.broadcasted_iota(I32, (1, kw_t.shape[1]), 1)
    kpos_w = jnp.where(kpos_w >= 0, kpos_w, -(2 ** 30))
    behind = lax.bitcast_convert_type(qpos - kpos_w, jnp.uint32)
    e_w, l_w = _exp_rows(s_w, behind < jnp.uint32(WINDOW))
    return lax.dot_general(e_w.astype(BF16), vw_t, NT_DIMS, preferred_element_type=F32), l_w


def _gate_and_store(o_ref, gates, o_c, acc_s, l_s, acc_w, l_w, nq):
    inv_s = 1.0 / l_s
    inv_w = 1.0 / l_w
    for hh in range(HEADS_PER_GROUP):
        rows = slice(hh * nq, (hh + 1) * nq)
        o_h = (gates[:, 3 * hh:3 * hh + 1] * o_c[rows] + (gates[:, 3 * hh + 1:3 * hh + 2] * inv_s[rows]) * acc_s[rows]
               + (gates[:, 3 * hh + 2:3 * hh + 3] * inv_w[rows]) * acc_w[rows])
        o_ref[0, :, hh * HEAD_DIM:(hh + 1) * HEAD_DIM] = o_h.astype(BF16)


def _nsa_prompt_kernel(q_ref, gate_ref, kc_ref, vc_ref, ovt_ref, ksel_ref, vsel_ref, kwin_ref, vwin_ref,
                       o_ref, score_scr, s_a, s_b, *, nq, nc, ns, kc_keys):
    i = pl.program_id(2)
    hpg = HEADS_PER_GROUP
    r = hpg * nq
    nsp = ovt_ref.shape[0]
    t_len = kwin_ref.shape[3]
    q_first = i * nq
    q_wide = q_ref[0, 0].reshape(r, LANES)
    q = q_wide[:, 0:HEAD_DIM].astype(BF16)
    qpos = q_first + lax.broadcasted_iota(I32, (r, 1), 0) % nq
    o_c, score_t, valid_s = _compressed_branch(q, qpos, q_first, kc_ref[0, 0], vc_ref[0, 0], ovt_ref[...], nq, nc, ns)

    score_scr[...] = score_t
    sidx = lax.broadcasted_iota(I32, (nsp, 1), 0)
    n_live = jnp.minimum((q_first + nq - 1) // SEL_BLOCK + 1, ns)

    def rank_body(sp, rank):
        row = score_scr[pl.ds(sp, 1), :]
        ahead = (row > score_t) | ((row == score_t) & (sidx > sp))
        return rank + ahead.astype(F32)

    rank = lax.fori_loop(0, n_live, rank_body, jnp.zeros((nsp, nq), F32))
    dropped_t = 1.0 - ((rank < float(min(N_SELECT, ns))) & valid_s).astype(F32)
    dropped = jnp.transpose(jnp.concatenate([jnp.zeros((HEAD_DIM, nq), F32), dropped_t], axis=0))
    qa = (q_wide + jnp.concatenate([dropped] * hpg, axis=0)).astype(BF16)

    def scores_into(dst, c):
        start = pl.multiple_of(c * kc_keys, kc_keys)
        dst[...] = jnp.dot(qa, ksel_ref[0, 0, :, pl.ds(start, kc_keys)], preferred_element_type=F32)

    def update(src, c, carry, causal):
        m, l, acc = carry
        start = pl.multiple_of(c * kc_keys, kc_keys)
        s = src[...]
        if causal:
            kpos = c * kc_keys + lax.broadcasted_iota(I32, (1, kc_keys), 1)
            s = jnp.where(kpos <= qpos, s, NEG)
        m_new = jnp.maximum(m, jnp.max(s, axis=-1, keepdims=True))
        alpha = jnp.exp2(m - m_new)
        p = jnp.exp2(s - m_new)
        l = alpha * l + jnp.sum(p, axis=-1, keepdims=True)
        pv = lax.dot_general(p.astype(BF16), vsel_ref[0, 0, :, pl.ds(start, kc_keys)], NT_DIMS,
                             preferred_element_type=F32)
        return m_new, l, alpha * acc + pv

    n_past = q_first // kc_keys
    scores_into(s_a, 0)

    def pair_body(pi, carry):
        c = 2 * pi
        scores_into(s_b, c + 1)
        carry = update(s_a, c, carry, causal=False)
        scores_into(s_a, c + 2)
        return update(s_b, c + 1, carry, causal=False)

    init = (jnp.full((r, 1), NEG, F32), jnp.zeros((r, 1), F32), jnp.zeros((r, HEAD_DIM), F32))
    carry = lax.fori_loop(0, n_past // 2, pair_body, init)

    def odd_tail(carry):
        scores_into(s_b, n_past)
        carry = update(s_a, n_past - 1, carry, causal=False)
        return update(s_b, n_past, carry, causal=True)

    def even_tail(carry):
        return update(s_a, n_past, carry, causal=True)

    _, l_s, acc_s = lax.cond(n_past % 2 == 1, odd_tail, even_tail, carry)

    w0 = pl.multiple_of(jnp.clip(q_first + nq - WIN_SPAN, 0, t_len - WIN_SPAN), LANES)
    acc_w, l_w = _window_branch(q, qpos, kwin_ref[0, 0, :, pl.ds(w0, WIN_SPAN)],
                                vwin_ref[0, 0, :, pl.ds(w0, WIN_SPAN)], w0)
    _gate_and_store(o_ref, gate_ref[0], o_c, acc_s, l_s, acc_w, l_w, nq)


def nsa_attend_prompt(q, gates, kc_t, vc_t, ovt, ksel_t, vsel_t, kwin_t, vwin_t, *, nc, ns):
    b, _, _, t, _ = q.shape
    assert ovt.shape[0] == MASK_ROWS == LANES - HEAD_DIM
    nq = PROMPT_NQ
    kc_keys = min(512, t)
    ncp = kc_t.shape[3]
    nsp = ovt.shape[0]
    per_bg = lambda rows, cols: pl.BlockSpec((1, 1, rows, cols), lambda bi, g, ti: (bi, g, 0, 0))
    kernel = functools.partial(_nsa_prompt_kernel, nq=nq, nc=nc, ns=ns, kc_keys=kc_keys)
    return pl.pallas_call(
        kernel,
        grid=(b, NSA_GROUPS, t // nq),
        in_specs=[
            pl.BlockSpec((1, 1, HEADS_PER_GROUP, nq, LANES), lambda bi, g, ti: (bi, g, 0, ti, 0)),
            pl.BlockSpec((1, nq, LANES), lambda bi, g, ti: (bi, ti, g)),
            per_bg(HEAD_DIM, ncp), per_bg(HEAD_DIM, ncp),
            pl.BlockSpec((nsp, ncp), lambda bi, g, ti: (0, 0)),
            per_bg(HEAD_DIM + MASK_ROWS, t), per_bg(HEAD_DIM, t), per_bg(HEAD_DIM, t), per_bg(HEAD_DIM, t),
        ],
        out_specs=pl.BlockSpec((1, nq, GROUP_LANES), lambda bi, g, ti: (bi, ti, g)),
        out_shape=jax.ShapeDtypeStruct((b, t, NSA_HEADS * HEAD_DIM), BF16),
        scratch_shapes=[pltpu.VMEM((nsp, nq), F32), pltpu.VMEM((HEADS_PER_GROUP * nq, kc_keys), F32),
                        pltpu.VMEM((HEADS_PER_GROUP * nq, kc_keys), F32)],
        compiler_params=_params("parallel", "parallel", "arbitrary"),
        name="nsa_attend_prompt",
    )(q, gates, kc_t, vc_t, ovt, ksel_t, vsel_t, kwin_t, vwin_t)


def _nsa_decode_kernel(q_ref, gate_ref, kc_ref, vc_ref, ovt_ref, ind_ref, ksel_ref, vsel_ref, kwin_ref, vwin_ref,
                       o_ref, *, nq, nc, ns, q_pos0, win_pos0):
    hpg = HEADS_PER_GROUP
    r = hpg * nq
    nsp = ovt_ref.shape[0]
    l_keys = ksel_ref.shape[3]
    q = q_ref[0, 0].reshape(r, LANES)[:, 0:HEAD_DIM].astype(BF16)
    qpos = q_pos0 + lax.broadcasted_iota(I32, (r, 1), 0) % nq
    o_c, score_t, valid_s = _compressed_branch(q, qpos, q_pos0, kc_ref[0, 0], vc_ref[0, 0], ovt_ref[...], nq, nc, ns)

    score = jnp.transpose(score_t)
    valid = jnp.transpose(valid_s.astype(F32))
    s_other = lax.broadcasted_iota(I32, (nsp, 1), 0)
    s_self = lax.broadcasted_iota(I32, (1, nsp), 1)
    qrow = lax.broadcasted_iota(I32, (nq, 1), 0)
    rank = jnp.zeros((nq, nsp), F32)
    for qi in range(nq):
        other = score_t[:, qi:qi + 1]
        own = score[qi:qi + 1, :]
        ahead = (other > own) | ((other == own) & (s_other < s_self))
        rank = jnp.where(qrow == qi, jnp.sum(ahead.astype(F32), axis=0, keepdims=True), rank)
    sel = ((rank < float(min(N_SELECT, ns))) & (valid > 0.5)).astype(F32)
    sel_rows = jnp.concatenate([sel] * hpg, axis=0).astype(BF16)

    s = jnp.dot(q, ksel_ref[0, 0], preferred_element_type=F32)
    picked = jnp.dot(sel_rows, ind_ref[...], preferred_element_type=F32)
    kpos = lax.broadcasted_iota(I32, (1, l_keys), 1)
    e_s, l_s = _exp_rows(s, (picked > 0.5) & (kpos <= qpos))
    acc_s = lax.dot_general(e_s.astype(BF16), vsel_ref[0, 0], NT_DIMS, preferred_element_type=F32)

    acc_w, l_w = _window_branch(q, qpos, kwin_ref[0, 0], vwin_ref[0, 0], win_pos0)
    _gate_and_store(o_ref, gate_ref[0], o_c, acc_s, l_s, acc_w, l_w, nq)


def nsa_attend_decode(q, gates, kc_t, vc_t, ovt, ind, ksel_t, vsel_t, kwin_t, vwin_t, *, nc, ns, q_pos0, win_pos0):
    b, _, _, nq, _ = q.shape
    ncp = kc_t.shape[3]
    nsp = ovt.shape[0]
    l_keys = ksel_t.shape[3]
    per_bg = lambda cols: pl.BlockSpec((1, 1, HEAD_DIM, cols), lambda bi, g: (bi, g, 0, 0))
    kernel = functools.partial(_nsa_decode_kernel, nq=nq, nc=nc, ns=ns, q_pos0=q_pos0, win_pos0=win_pos0)
    return pl.pallas_call(
        kernel,
        grid=(b, NSA_GROUPS),
        in_specs=[
            pl.BlockSpec((1, 1, HEADS_PER_GROUP, nq, LANES), lambda bi, g: (bi, g, 0, 0, 0)),
            pl.BlockSpec((1, nq, LANES), lambda bi, g: (bi, 0, g)),
            per_bg(ncp), per_bg(ncp),
            pl.BlockSpec((nsp, ncp), lambda bi, g: (0, 0)),
            pl.BlockSpec((nsp, l_keys), lambda bi, g: (0, 0)),
            per_bg(l_keys), per_bg(l_keys), per_bg(WIN_SPAN), per_bg(WIN_SPAN),
        ],
        out_specs=pl.BlockSpec((1, nq, GROUP_LANES), lambda bi, g: (bi, 0, g)),
        out_shape=jax.ShapeDtypeStruct((b, nq, NSA_HEADS * HEAD_DIM), BF16),
        compiler_params=_params("parallel", "parallel"),
        name="nsa_attend_decode",
    )(q, gates, kc_t, vc_t, ovt, ind, ksel_t, vsel_t, kwin_t, vwin_t)


def _gather_kernel(pt_ref, *refs, n_steps):
    page_refs = refs[:GATHER_PAGES]
    new_ref, perm_ref, cmp_ref, ksel_ref, vsel_ref = refs[GATHER_PAGES:]
    p = pl.program_id(1)
    page = page_refs[0].shape[2]
    k_lo = 2 * GROUP_LANES
    v_lo = 3 * GROUP_LANES

    @pl.when(p < n_steps - 1)
    def _():
        cmp_feats = []
        for k, page_ref in enumerate(page_refs):
            x = page_ref[0]
            cols = slice(k * page, (k + 1) * page)
            cmp_feats.append(x[0:2 * GROUP_LANES, :].astype(BF16))
            for g in range(NSA_GROUPS):
                ksel_ref[0, g, :, cols] = x[k_lo + g * HEAD_DIM:k_lo + (g + 1) * HEAD_DIM, :].astype(BF16)
                vsel_ref[0, g, :, cols] = x[v_lo + g * HEAD_DIM:v_lo + (g + 1) * HEAD_DIM, :].astype(BF16)
        feats = jnp.concatenate(cmp_feats, axis=1)
        toks = lax.dot_general(perm_ref[...], feats, NT_DIMS, preferred_element_type=F32).astype(BF16)
        n_chunks = toks.shape[0] // CMP_STRIDE
        for pos in range(CMP_STRIDE):
            cmp_ref[:, pos * 2 * GROUP_LANES:(pos + 1) * 2 * GROUP_LANES] = toks[pos * n_chunks:(pos + 1) * n_chunks, :]

    @pl.when(p == n_steps - 1)
    def _():
        new = new_ref[0]
        padded = jnp.concatenate([new, jnp.zeros((GATHER_PAGES * page - new.shape[0], new.shape[1]), F32)], axis=0)
        kt = jnp.transpose(padded[:, k_lo:k_lo + GROUP_LANES])
        vt = jnp.transpose(padded[:, v_lo:v_lo + GROUP_LANES])
        for g in range(NSA_GROUPS):
            ksel_ref[0, g] = kt[g * HEAD_DIM:(g + 1) * HEAD_DIM, :].astype(BF16)
            vsel_ref[0, g] = vt[g * HEAD_DIM:(g + 1) * HEAD_DIM, :].astype(BF16)


def gather_past(page_table, cache_t, new_rows):
    db, n_pages = page_table.shape
    page = cache_t.shape[2]
    dq = new_rows.shape[1]
    past = n_pages * page
    n_full = n_pages // GATHER_PAGES
    n_steps = n_full + 1
    step_keys = GATHER_PAGES * page
    chunk_lanes = CMP_STRIDE * 2 * GROUP_LANES
    n_chunks = step_keys // CMP_STRIDE
    out_row = np.arange(step_keys)
    perm = jnp.asarray((out_row % n_chunks * CMP_STRIDE + out_row // n_chunks)[:, None] == np.arange(step_keys)[None, :],
                       BF16)

    def page_spec(k):
        return pl.BlockSpec(
            (1, 4 * GROUP_LANES, page),
            lambda b, p, pt: (pt[b * n_pages + jnp.minimum(p, n_full - 1) * GATHER_PAGES + k], 0, 0))

    grid_spec = pltpu.PrefetchScalarGridSpec(
        num_scalar_prefetch=1,
        grid=(db, n_steps),
        in_specs=[page_spec(k) for k in range(GATHER_PAGES)] + [
            pl.BlockSpec((1, dq, 4 * GROUP_LANES), lambda b, p, pt: (b, 0, 0)),
            pl.BlockSpec((step_keys, step_keys), lambda b, p, pt: (0, 0))],
        out_specs=[
            pl.BlockSpec((step_keys // CMP_STRIDE, chunk_lanes),
                         lambda b, p, pt: (b * n_full + jnp.minimum(p, n_full - 1), 0)),
            pl.BlockSpec((1, NSA_GROUPS, HEAD_DIM, step_keys), lambda b, p, pt: (b, 0, 0, p)),
            pl.BlockSpec((1, NSA_GROUPS, HEAD_DIM, step_keys), lambda b, p, pt: (b, 0, 0, p)),
        ],
    )
    return pl.pallas_call(
        functools.partial(_gather_kernel, n_steps=n_steps),
        grid_spec=grid_spec,
        out_shape=[
            jax.ShapeDtypeStruct((db * past // CMP_STRIDE, chunk_lanes), BF16),
            jax.ShapeDtypeStruct((db, NSA_GROUPS, HEAD_DIM, past + step_keys), BF16),
            jax.ShapeDtypeStruct((db, NSA_GROUPS, HEAD_DIM, past + step_keys), BF16),
        ],
        compiler_params=_params("parallel", "arbitrary"),
        name="gather_past",
    )(page_table.reshape(-1), *([cache_t] * GATHER_PAGES), new_rows, perm)


def _win_assemble_kernel(cache_ref, new_ref, win_ref, kwin_ref, vwin_ref):
    old = cache_ref[0]
    new = new_ref[0]
    buf = old.shape[0]
    dq = new.shape[0]
    win_ref[0, 0:buf - dq, :] = old[dq:, :]
    win_ref[0, buf - dq:buf, :] = new
    old_t = jnp.transpose(old)
    tail = jnp.concatenate([new, jnp.zeros((WIN_SPAN - buf - dq, new.shape[1]), F32)], axis=0)
    tail_t = jnp.transpose(tail)
    for ref, off in ((kwin_ref, 0), (vwin_ref, GROUP_LANES)):
        for g in range(NSA_GROUPS):
            rows = slice(off + g * HEAD_DIM, off + (g + 1) * HEAD_DIM)
            ref[0, g] = jnp.concatenate([old_t[rows, :], tail_t[rows, :]], axis=1).astype(BF16)


def win_assemble(cache_win, new_win):
    db, buf, width = cache_win.shape
    dq = new_win.shape[1]
    return pl.pallas_call(
        _win_assemble_kernel,
        grid=(db,),
        in_specs=[
            pl.BlockSpec((1, buf, width), lambda b: (b, 0, 0)),
            pl.BlockSpec((1, dq, width), lambda b: (b, 0, 0)),
        ],
        out_specs=[
            pl.BlockSpec((1, buf, width), lambda b: (b, 0, 0)),
            pl.BlockSpec((1, NSA_GROUPS, HEAD_DIM, WIN_SPAN), lambda b: (b, 0, 0, 0)),
            pl.BlockSpec((1, NSA_GROUPS, HEAD_DIM, WIN_SPAN), lambda b: (b, 0, 0, 0)),
        ],
        out_shape=[
            jax.ShapeDtypeStruct((db, buf, width), F32),
            jax.ShapeDtypeStruct((db, NSA_GROUPS, HEAD_DIM, WIN_SPAN), BF16),
            jax.ShapeDtypeStruct((db, NSA_GROUPS, HEAD_DIM, WIN_SPAN), BF16),
        ],
        compiler_params=_params("parallel"),
        name="win_assemble",
    )(cache_win, new_win)


def _rope_angles(pos, half):
    inv = ROPE_THETA ** (-jnp.arange(half, dtype=F32) / half)
    ang = pos.astype(F32)[:, None] * inv[None, :]
    return jnp.cos(ang), jnp.sin(ang)


def _rope_tables_head64(pos):
    cos, sin = _rope_angles(pos, HEAD_DIM // 2)
    return jnp.concatenate([cos] * 4, axis=1), jnp.concatenate([-sin, sin] * 2, axis=1)


def _block_diag_groups(w):
    eye = jnp.eye(NSA_GROUPS, dtype=w.dtype)
    out = jnp.einsum("gh,...dn->...gdhn", eye, w)
    return out.reshape(*w.shape[:-2], GROUP_LANES, NSA_GROUPS * w.shape[-1])


def _compress_weights(cmp_w1, cmp_b1, cmp_w2, cmp_pe):
    r = CMP_BLOCK // CMP_STRIDE
    w1 = cmp_w1.reshape(2, r, CMP_STRIDE, HEAD_DIM, HEAD_DIM)
    bd = _block_diag_groups(w1)
    bd = jnp.concatenate([bd[:, 0], bd[:, 1]], axis=-1).astype(BF16)
    pe_rows = jnp.broadcast_to(cmp_pe.reshape(2, 1, CMP_BLOCK * HEAD_DIM), (2, 8, CMP_BLOCK * HEAD_DIM)).astype(BF16)
    w1_tiled = jnp.tile(cmp_w1, (1, 1, NSA_GROUPS)).astype(BF16)
    b1_tiled = jnp.tile(cmp_b1, (1, NSA_GROUPS)).reshape(2, 1, GROUP_LANES)
    w2_bd = _block_diag_groups(cmp_w2).astype(BF16)
    return bd[0], bd[1], pe_rows, w1_tiled, b1_tiled, w2_bd


def _overlap_table(nc, ncp, ns, nsp):
    ci = np.arange(ncp)[None, :]
    sj = np.arange(nsp)[:, None]
    overlap_t = ((ci * CMP_STRIDE < (sj + 1) * SEL_BLOCK) & (ci * CMP_STRIDE + CMP_BLOCK > sj * SEL_BLOCK)
                 & (ci < nc) & (sj < ns))
    return jnp.asarray(overlap_t, BF16)


def _block_membership(length, nsp):
    return jnp.asarray((np.arange(length)[None, :] // SEL_BLOCK) == np.arange(nsp)[:, None], BF16)


def _round_up(x, m):
    return -(-x // m) * m


def _trunk(x, pos, ret_s0, past, w):
    (ffn_norm, ffn_w_in, ffn_w_out, ret_norm, ret_w_in, ret_w_out, kv_norm, kv_w, k_norm_tiled,
     cmp_weights, nsa_norm, nsa_wq, nsa_wg, q_norm_tiled, nsa_w_out, ones_bd) = w
    b, t, _ = x.shape
    n = b * t
    prompt = past is None
    xf = x.reshape(n, D_MODEL)
    pos_rows = jnp.tile(pos, b) if not prompt else pos
    ret_cos, ret_sin = _rope_angles(pos_rows, RET_DK // 2)
    cos64, sin64 = _rope_tables_head64(pos)
    ret_states = []
    rows = win = attend = None
    for layer in range(DEPTH):
        if layer == N_A_LAYERS:
            xs = xf.reshape(b, t, D_MODEL)
            if prompt:
                rows, win, cmp_tok, ksel, vsel, kwin, vwin = kv_rows(
                    xs, kv_norm, kv_w, k_norm_tiled, cos64, sin64, ones_bd, aux=True)
                length = t
                new_win = win[:, t - min(WINDOW, t):]
            else:
                page_table, cache_t, cache_win = past
                rows, win = kv_rows(xs, kv_norm, kv_w, k_norm_tiled, cos64, sin64, ones_bd, aux=False)
                cmp_tok, ksel, vsel = gather_past(page_table, cache_t, rows)
                new_win, kwin, vwin = win_assemble(cache_win, win)
                past_len = page_table.shape[1] * cache_t.shape[2]
                length = past_len + t
            nc = (length - CMP_BLOCK) // CMP_STRIDE + 1
            n_chunk_rows = nc + CMP_BLOCK // CMP_STRIDE - 1
            ns = -(-length // SEL_BLOCK)
            wk_bd, wv_bd, pe_rows, w1_tiled, b1_tiled, w2_bd = cmp_weights
            assert cmp_tok.shape[0] == b * n_chunk_rows
            partial = cmp_partial(cmp_tok, wk_bd, wv_bd).reshape(b, n_chunk_rows, 4 * GROUP_LANES)
            c_end = jnp.arange(n_chunk_rows, dtype=I32) * CMP_STRIDE + (CMP_BLOCK - 1)
            cos_c, sin_c = _rope_tables_head64(c_end)
            kc_t, vc_t = cmp_combine(partial, pe_rows, w1_tiled, b1_tiled, w2_bd, k_norm_tiled, cos_c, sin_c, ones_bd)
            if prompt:
                assert ns <= MASK_ROWS and t % PROMPT_NQ == 0
                ovt = _overlap_table(nc, n_chunk_rows, ns, MASK_ROWS)
                attend = functools.partial(nsa_attend_prompt, kc_t=kc_t, vc_t=vc_t, ovt=ovt, ksel_t=ksel, vsel_t=vsel,
                                           kwin_t=kwin, vwin_t=vwin, nc=nc, ns=ns)
            else:
                nsp = _round_up(ns, 16)
                ovt = _overlap_table(nc, n_chunk_rows, ns, nsp)
                ind = _block_membership(ksel.shape[3], nsp)
                attend = functools.partial(nsa_attend_decode, kc_t=kc_t, vc_t=vc_t, ovt=ovt, ind=ind, ksel_t=ksel,
                                           vsel_t=vsel, kwin_t=kwin, vwin_t=vwin, nc=nc, ns=ns,
                                           q_pos0=int(past_len), win_pos0=int(past_len - cache_win.shape[1]))
        xf = ffn_half(xf, ffn_norm[layer, 0], ffn_w_in, ffn_w_out, layer, 0)
        if layer < N_A_LAYERS:
            qkvg = ret_inproj(xf, ret_norm[layer], ret_w_in, layer, ret_cos, ret_sin)
            gated, s_fin = ret_core(qkvg.reshape(b, t, 6 * D_MODEL), ret_s0, layer)
            ret_states.append(s_fin)
            xf = mm_residual(gated.reshape(n, 2 * D_MODEL), ret_w_out, layer, xf)
        else:
            j = layer - N_A_LAYERS
            q, gates = nsa_q(xf.reshape(b, t, D_MODEL), nsa_norm[j], nsa_wq, nsa_wg, j, q_norm_tiled[j],
                             cos64, sin64, ones_bd)
            o = attend(q, gates)
            xf = mm_residual(o.reshape(n, NSA_HEADS * HEAD_DIM), nsa_w_out, j, xf)
        xf = ffn_half(xf, ffn_norm[layer, 1], ffn_w_in, ffn_w_out, layer, 1)
    return xf.reshape(b, t, D_MODEL), jnp.stack(ret_states), rows, new_win


def kernel(x_prompt, x_sample, state_ret, cache_kv, cache_win, page_table, ffn_norm, ffn_w_in, ffn_w_out, ret_norm,
           ret_w_in, ret_w_out, kv_norm, kv_w, k_norm, cmp_w1, cmp_b1, cmp_w2, cmp_pe, nsa_norm, nsa_w_in, q_norm,
           nsa_w_out):
    b, t, _ = x_prompt.shape
    db, dq, _ = x_sample.shape
    n_phys, page = cache_kv.shape[:2]
    past_len = page_table.shape[1] * page
    n_q_cols = NSA_HEADS * HEAD_DIM

    gate_w = nsa_w_in[:, :, n_q_cols:].reshape(-1, D_MODEL, NSA_GROUPS, N_GATES)
    gate_w = jnp.pad(gate_w, ((0, 0), (0, 0), (0, 0), (0, LANES - N_GATES))).reshape(-1, D_MODEL, NSA_GROUPS * LANES)
    eye = np.arange(GROUP_LANES)
    ones_bd = jnp.asarray((eye[:, None] // HEAD_DIM) == (eye[None, :] // HEAD_DIM), BF16)
    w = (ffn_norm, ffn_w_in.astype(BF16), ffn_w_out.astype(BF16), ret_norm, ret_w_in.astype(BF16),
         ret_w_out.astype(BF16), kv_norm, kv_w.astype(BF16), jnp.tile(k_norm, (1, NSA_GROUPS)),
         _compress_weights(cmp_w1, cmp_b1, cmp_w2, cmp_pe), nsa_norm, nsa_w_in[:, :, :n_q_cols].astype(BF16),
         gate_w.astype(BF16), jnp.tile(q_norm, (1, NSA_GROUPS)).reshape(-1, 1, GROUP_LANES),
         nsa_w_out.astype(BF16), ones_bd)

    pos_p = jnp.arange(t, dtype=I32)
    pos_s = past_len + jnp.arange(dq, dtype=I32)
    y_p, ret_p, rows_p, win_p = _trunk(x_prompt, pos_p, None, None, w)
    cache_t = jnp.transpose(cache_kv, (0, 2, 3, 4, 1)).reshape(n_phys, 4 * GROUP_LANES, page)
    cwin = cache_win.reshape(db, cache_win.shape[1], 2 * GROUP_LANES)
    y_s, ret_s, rows_s, win_s = _trunk(x_sample, pos_s, state_ret, (page_table, cache_t, cwin), w)
    kv_shape = (4, NSA_GROUPS, HEAD_DIM)
    win_shape = (2, NSA_GROUPS, HEAD_DIM)
    return (y_p, y_s, ret_p.astype(state_ret.dtype), ret_s.astype(state_ret.dtype),
            rows_p.reshape(b, t, *kv_shape), rows_s.reshape(db, dq, *kv_shape),
            win_p.reshape(b, win_p.shape[1], *win_shape), win_s.reshape(db, win_s.shape[1], *win_shape))
```

```python
import functools

import jax
import jax.numpy as jnp
import numpy as np
from jax import lax
from jax.experimental import pallas as pl
from jax.experimental.pallas import tpu as pltpu

F32 = jnp.float32
BF16 = jnp.bfloat16
I32 = jnp.int32

D_MODEL = 1024
DEPTH = 4
N_A_LAYERS = DEPTH // 2
RET_HEADS = 4
RET_DK = D_MODEL // RET_HEADS
RET_DV = 2 * D_MODEL // RET_HEADS
RET_CHUNK = 128
NSA_HEADS = 16
NSA_GROUPS = 4
HEADS_PER_GROUP = NSA_HEADS // NSA_GROUPS
HEAD_DIM = D_MODEL // NSA_HEADS
CMP_BLOCK = 32
CMP_STRIDE = 16
SEL_BLOCK = 64
N_SELECT = 16
N_LOCAL = 2
WINDOW = 512
D_FF = 2816
ROPE_THETA = 10000.0
EPS = 1e-6
NEG = -1e30
BIG = 1e9
N_GATES = 3 * HEADS_PER_GROUP
GROUP_LANES = NSA_GROUPS * HEAD_DIM
PROMPT_NQ = 128
WIN_SPAN = WINDOW + PROMPT_NQ
MASK_ROWS = 64
MASK_BIAS = -(2.0 ** 100)
GATHER_PAGES = 4
LOG2E = 1.4426950408889634

VMEM_LIMIT_BYTES = 56 * 1024 * 1024
LANES = 128

NT_DIMS = (((1,), (1,)), ((), ()))
TN_DIMS = (((0,), (0,)), ((), ()))


def _params(*semantics):
    return pltpu.CompilerParams(dimension_semantics=semantics, vmem_limit_bytes=VMEM_LIMIT_BYTES)


def _rms(x, gain):
    ms = jnp.mean(x * x, axis=-1, keepdims=True)
    return x * lax.rsqrt(ms + EPS) * gain


def _seg_rms(y, ones_bd, gain):
    sq = y * y
    hi = sq.astype(BF16)
    lo = (sq - hi.astype(F32)).astype(BF16)
    ss = jnp.dot(hi, ones_bd, preferred_element_type=F32) + jnp.dot(lo, ones_bd, preferred_element_type=F32)
    return y * lax.rsqrt(ss * (1.0 / HEAD_DIM) + EPS) * gain


def _rope64(x, cos, sin_signed):
    lane = lax.broadcasted_iota(I32, x.shape, 1)
    first_half = (lane % HEAD_DIM) < (HEAD_DIM // 2)
    rot = jnp.where(first_half, pltpu.roll(x, LANES - HEAD_DIM // 2, 1), pltpu.roll(x, HEAD_DIM // 2, 1))
    return x * cos + rot * sin_signed


def _exp_rows(s, ok):
    s = jnp.where(ok, s, NEG)
    m = jnp.max(s, axis=-1, keepdims=True)
    e = jnp.exp2(s - m)
    return e, jnp.sum(e, axis=-1, keepdims=True)


def _ffn_kernel(x_ref, g_ref, wi_ref, wo_ref, o_ref):
    x = x_ref[...]
    xn = _rms(x, g_ref[...]).astype(BF16)
    a = jnp.dot(xn, wi_ref[:, 0:D_FF], preferred_element_type=F32)
    b = jnp.dot(xn, wi_ref[:, D_FF:2 * D_FF], preferred_element_type=F32)
    h = (a * jax.nn.sigmoid(a) * b).astype(BF16)
    o_ref[...] = x + 0.5 * jnp.dot(h, wo_ref[...], preferred_element_type=F32)


def ffn_half(x, gain, w_in, w_out, layer, half):
    n = x.shape[0]
    tm = min(n, 512)
    resident = pl.Buffered(1)
    return pl.pallas_call(
        _ffn_kernel,
        grid=(n // tm,),
        in_specs=[
            pl.BlockSpec((tm, D_MODEL), lambda i: (i, 0)),
            pl.BlockSpec((1, D_MODEL), lambda i: (0, 0)),
            pl.BlockSpec((None, None, D_MODEL, 2 * D_FF), lambda i: (layer, half, 0, 0), pipeline_mode=resident),
            pl.BlockSpec((None, None, D_FF, D_MODEL), lambda i: (layer, half, 0, 0), pipeline_mode=resident),
        ],
        out_specs=pl.BlockSpec((tm, D_MODEL), lambda i: (i, 0)),
        out_shape=jax.ShapeDtypeStruct((n, D_MODEL), F32),
        compiler_params=_params("parallel"),
        name="ffn_half",
    )(x, gain.reshape(1, D_MODEL), w_in, w_out)


def _proj_ffn_kernel(a_ref, wp_ref, x_ref, g_ref, wi_ref, wo_ref, o_ref):
    x = x_ref[...] + jnp.dot(a_ref[...], wp_ref[...], preferred_element_type=F32)
    xn = _rms(x, g_ref[...]).astype(BF16)
    a = jnp.dot(xn, wi_ref[:, 0:D_FF], preferred_element_type=F32)
    b = jnp.dot(xn, wi_ref[:, D_FF:2 * D_FF], preferred_element_type=F32)
    h = (a * jax.nn.sigmoid(a) * b).astype(BF16)
    o_ref[...] = x + 0.5 * jnp.dot(h, wo_ref[...], preferred_element_type=F32)


def proj_ffn_half(a, w_proj, proj_layer, x, gain, w_in, w_out, layer, half):
    n, k = a.shape
    tm = min(n, 512)
    resident = pl.Buffered(1)
    return pl.pallas_call(
        _proj_ffn_kernel,
        grid=(n // tm,),
        in_specs=[
            pl.BlockSpec((tm, k), lambda i: (i, 0)),
            pl.BlockSpec((None, k, D_MODEL), lambda i: (proj_layer, 0, 0), pipeline_mode=resident),
            pl.BlockSpec((tm, D_MODEL), lambda i: (i, 0)),
            pl.BlockSpec((1, D_MODEL), lambda i: (0, 0)),
            pl.BlockSpec((None, None, D_MODEL, 2 * D_FF), lambda i: (layer, half, 0, 0), pipeline_mode=resident),
            pl.BlockSpec((None, None, D_FF, D_MODEL), lambda i: (layer, half, 0, 0), pipeline_mode=resident),
        ],
        out_specs=pl.BlockSpec((tm, D_MODEL), lambda i: (i, 0)),
        out_shape=jax.ShapeDtypeStruct((n, D_MODEL), F32),
        compiler_params=_params("parallel"),
        name="proj_ffn_half",
    )(a, w_proj, x, gain.reshape(1, D_MODEL), w_in, w_out)


def _ret_inproj_kernel(x_ref, g_ref, w_ref, cos_ref, sin_ref, o_ref):
    xn = _rms(x_ref[...], g_ref[...]).astype(BF16)
    c = cos_ref[...]
    s = sin_ref[...]
    half = RET_DK // 2
    for h in range(2 * RET_HEADS):
        lo = h * RET_DK
        y = jnp.dot(xn, w_ref[:, lo:lo + RET_DK], preferred_element_type=F32)
        scale = 1.0 if h < RET_HEADS else RET_DK ** -0.5
        x1 = y[:, :half]
        x2 = y[:, half:]
        o_ref[:, lo:lo + half] = ((x1 * c - x2 * s) * scale).astype(BF16)
        o_ref[:, lo + half:lo + RET_DK] = ((x1 * s + x2 * c) * scale).astype(BF16)
    for h in range(2 * RET_HEADS):
        lo = 2 * D_MODEL + h * RET_DV
        o_ref[:, lo:lo + RET_DV] = jnp.dot(xn, w_ref[:, lo:lo + RET_DV], preferred_element_type=F32).astype(BF16)


def ret_inproj(x, gain, w, layer, cos, sin):
    n = x.shape[0]
    p = cos.shape[0]
    tm = min(n, 512, p)
    n_out = w.shape[2]
    tab_blocks = p // tm
    return pl.pallas_call(
        _ret_inproj_kernel,
        grid=(n // tm,),
        in_specs=[
            pl.BlockSpec((tm, D_MODEL), lambda i: (i, 0)),
            pl.BlockSpec((1, D_MODEL), lambda i: (0, 0)),
            pl.BlockSpec((None, D_MODEL, n_out), lambda i: (layer, 0, 0), pipeline_mode=pl.Buffered(1)),
            pl.BlockSpec((tm, RET_DK // 2), lambda i: (i % tab_blocks, 0)),
            pl.BlockSpec((tm, RET_DK // 2), lambda i: (i % tab_blocks, 0)),
        ],
        out_specs=pl.BlockSpec((tm, n_out), lambda i: (i, 0)),
        out_shape=jax.ShapeDtypeStruct((n, n_out), BF16),
        compiler_params=_params("parallel"),
        name="ret_inproj",
    )(x, gain.reshape(1, D_MODEL), w, cos, sin)


def _ret_core_kernel(*refs, chunk, n_inner, hps, has_s0):
    if has_s0:
        (q_ref, k_ref, v_ref, g_ref, dm_ref, qd_ref, kd_ref, sd_ref, s0_ref, o_ref, so_ref, s_scr) = refs
    else:
        (q_ref, k_ref, v_ref, g_ref, dm_ref, qd_ref, kd_ref, sd_ref, o_ref, so_ref, s_scr) = refs
    t = pl.program_id(2)

    @pl.when(t == 0)
    def _():
        if has_s0:
            s_scr[...] = s0_ref[0]
        else:
            s_scr[...] = jnp.zeros_like(s_scr)

    for c in range(n_inner):
        rows = slice(c * chunk, (c + 1) * chunk)
        for h in range(hps):
            qk_cols = slice(h * RET_DK, (h + 1) * RET_DK)
            v_cols = slice(h * RET_DV, (h + 1) * RET_DV)
            q = q_ref[0, rows, qk_cols]
            k = k_ref[0, rows, qk_cols]
            v = v_ref[0, rows, v_cols]
            g = g_ref[0, rows, v_cols].astype(F32)
            s = s_scr[h]
            scores = lax.dot_general(q, k, NT_DIMS, preferred_element_type=F32) * dm_ref[h]
            intra = jnp.dot(scores.astype(BF16), v, preferred_element_type=F32)
            cross = jnp.dot((q.astype(F32) * qd_ref[h]).astype(BF16), s.astype(BF16), preferred_element_type=F32)
            o = intra + cross
            kv = lax.dot_general((k.astype(F32) * kd_ref[h]).astype(BF16), v, TN_DIMS, preferred_element_type=F32)
            s_scr[h] = sd_ref[h, 0:1, 0:1] * s + kv
            mu = jnp.mean(o, axis=-1, keepdims=True)
            d = o - mu
            var = jnp.mean(d * d, axis=-1, keepdims=True)
            on = d * lax.rsqrt(var + EPS)
            o_ref[0, rows, v_cols] = (g * jax.nn.sigmoid(g) * on).astype(BF16)

    @pl.when(t == pl.num_programs(2) - 1)
    def _():
        so_ref[0] = s_scr[...]


def _decay_tables(chunk):
    lg = jnp.log(1.0 - 2.0 ** (-5.0 - jnp.arange(RET_HEADS, dtype=F32)))
    idx = jnp.arange(chunk, dtype=F32)
    rel = idx[:, None] - idx[None, :]
    dmat = jnp.where(rel >= 0, jnp.exp(jnp.maximum(rel, 0.0)[None] * lg[:, None, None]), 0.0)
    qdec = jnp.exp((idx + 1.0)[None, :] * lg[:, None])
    kdec = jnp.exp((chunk - 1.0 - idx)[None, :] * lg[:, None])
    sdec = jnp.exp(chunk * lg)
    qdec = jnp.broadcast_to(qdec[:, :, None], (RET_HEADS, chunk, RET_DK))
    kdec = jnp.broadcast_to(kdec[:, :, None], (RET_HEADS, chunk, RET_DK))
    sdec = jnp.broadcast_to(sdec[:, None, None], (RET_HEADS, 8, LANES))
    return dmat, qdec, kdec, sdec


def ret_core(qkvg, s0_all, layer):
    b, t, _ = qkvg.shape
    chunk = RET_CHUNK if t % RET_CHUNK == 0 else t
    tb = min(t, 4 * chunk)
    n_inner = tb // chunk
    hps = RET_HEADS if t < RET_CHUNK else 1
    nh = RET_HEADS // hps
    dmat, qdec, kdec, sdec = _decay_tables(chunk)
    in_specs = [
        pl.BlockSpec((1, tb, hps * RET_DK), lambda bi, h, ti: (bi, ti, h)),
        pl.BlockSpec((1, tb, hps * RET_DK), lambda bi, h, ti: (bi, ti, nh + h)),
        pl.BlockSpec((1, tb, hps * RET_DV), lambda bi, h, ti: (bi, ti, nh + h)),
        pl.BlockSpec((1, tb, hps * RET_DV), lambda bi, h, ti: (bi, ti, 2 * nh + h)),
        pl.BlockSpec((hps, chunk, chunk), lambda bi, h, ti: (h, 0, 0)),
        pl.BlockSpec((hps, chunk, RET_DK), lambda bi, h, ti: (h, 0, 0)),
        pl.BlockSpec((hps, chunk, RET_DK), lambda bi, h, ti: (h, 0, 0)),
        pl.BlockSpec((hps, 8, LANES), lambda bi, h, ti: (h, 0, 0)),
    ]
    args = [qkvg, qkvg, qkvg, qkvg, dmat, qdec, kdec, sdec]
    if s0_all is not None:
        in_specs.append(pl.BlockSpec((None, 1, hps, RET_DK, RET_DV), lambda bi, h, ti: (layer, bi, h, 0, 0)))
        args.append(s0_all)
    return pl.pallas_call(
        functools.partial(_ret_core_kernel, chunk=chunk, n_inner=n_inner, hps=hps, has_s0=s0_all is not None),
        grid=(b, nh, t // tb),
        in_specs=in_specs,
        out_specs=[
            pl.BlockSpec((1, tb, hps * RET_DV), lambda bi, h, ti: (bi, ti, h)),
            pl.BlockSpec((1, hps, RET_DK, RET_DV), lambda bi, h, ti: (bi, h, 0, 0)),
        ],
        out_shape=[
            jax.ShapeDtypeStruct((b, t, 2 * D_MODEL), BF16),
            jax.ShapeDtypeStruct((b, RET_HEADS, RET_DK, RET_DV), F32),
        ],
        scratch_shapes=[pltpu.VMEM((hps, RET_DK, RET_DV), F32)],
        compiler_params=_params("parallel", "parallel", "arbitrary"),
        name="ret_core",
    )(*args)


def _store_chunk_major(cmp_ref, scr, n_rows):
    n_planes = scr.shape[0]
    for p in range(CMP_STRIDE):
        for c in range(n_planes):
            lo = p * n_planes * LANES + c * LANES
            cmp_ref[:, lo:lo + LANES] = scr[c, pl.ds(p, n_rows // CMP_STRIDE, stride=CMP_STRIDE), :].astype(BF16)


def _kv_rows_kernel(x_ref, g_ref, w_ref, kn_ref, cos_ref, sin_ref, ones_ref, rows_ref, win_ref, *aux_refs):
    xn = _rms(x_ref[0], g_ref[...]).astype(BF16)
    y = jnp.dot(xn, w_ref[...], preferred_element_type=F32)
    tm = y.shape[0]
    cos = cos_ref[...]
    sin = sin_ref[...]
    ones_bd = ones_ref[...]
    slot = lambda s: y[:, s * GROUP_LANES:(s + 1) * GROUP_LANES]

    def norm_rope(v, gain):
        vn = _seg_rms(v, ones_bd, gain)
        return jnp.concatenate([_rope64(vn[:, :LANES], cos, sin), _rope64(vn[:, LANES:], cos, sin)], axis=1)

    k_slc = norm_rope(slot(2), kn_ref[1:2, :])
    k_win = norm_rope(slot(4), kn_ref[2:3, :])
    rows_ref[0, :, 0:2 * GROUP_LANES] = y[:, 0:2 * GROUP_LANES]
    rows_ref[0, :, 2 * GROUP_LANES:3 * GROUP_LANES] = k_slc
    rows_ref[0, :, 3 * GROUP_LANES:4 * GROUP_LANES] = slot(3)
    win_ref[0, :, 0:GROUP_LANES] = k_win
    win_ref[0, :, GROUP_LANES:2 * GROUP_LANES] = slot(5)
    if aux_refs:
        cmp_ref, ksel_ref, vsel_ref, kwin_ref, vwin_ref, scr = aux_refs
        for c in range(scr.shape[0]):
            scr[c] = y[:, c * LANES:(c + 1) * LANES]
        _store_chunk_major(cmp_ref, scr, tm)
        for ref, val in ((ksel_ref, k_slc), (vsel_ref, slot(3)), (kwin_ref, k_win), (vwin_ref, slot(5))):
            vt = jnp.transpose(val)
            for g in range(NSA_GROUPS):
                ref[0, g, 0:HEAD_DIM, :] = vt[g * HEAD_DIM:(g + 1) * HEAD_DIM, :].astype(BF16)
        key_blk = (pl.program_id(1) * tm + lax.broadcasted_iota(I32, (MASK_ROWS, tm), 1)) // SEL_BLOCK
        mask_rows = jnp.where(key_blk == lax.broadcasted_iota(I32, (MASK_ROWS, tm), 0), MASK_BIAS, 0.0).astype(BF16)
        for g in range(NSA_GROUPS):
            ksel_ref[0, g, HEAD_DIM:HEAD_DIM + MASK_ROWS, :] = mask_rows


def kv_rows(x, gain, w, k_norm_tiled, cos, sin, ones_bd, aux):
    b, t, _ = x.shape
    tm = min(t, 512)
    n_kv = w.shape[1]
    nt = t // tm
    out_specs = [
        pl.BlockSpec((1, tm, 4 * GROUP_LANES), lambda bi, ti: (bi, ti, 0)),
        pl.BlockSpec((1, tm, 2 * GROUP_LANES), lambda bi, ti: (bi, ti, 0)),
    ]
    out_shape = [
        jax.ShapeDtypeStruct((b, t, 4 * GROUP_LANES), F32),
        jax.ShapeDtypeStruct((b, t, 2 * GROUP_LANES), F32),
    ]
    scratch = []
    if aux:
        chunk_lanes = CMP_STRIDE * 2 * GROUP_LANES
        out_specs.append(pl.BlockSpec((tm // CMP_STRIDE, chunk_lanes), lambda bi, ti: (bi * nt + ti, 0)))
        out_shape.append(jax.ShapeDtypeStruct((b * t // CMP_STRIDE, chunk_lanes), BF16))
        for rows in (HEAD_DIM + MASK_ROWS, HEAD_DIM, HEAD_DIM, HEAD_DIM):
            out_specs.append(pl.BlockSpec((1, NSA_GROUPS, rows, tm), lambda bi, ti: (bi, 0, 0, ti)))
            out_shape.append(jax.ShapeDtypeStruct((b, NSA_GROUPS, rows, t), BF16))
        scratch.append(pltpu.VMEM((2 * GROUP_LANES // LANES, tm, LANES), F32))
    return pl.pallas_call(
        _kv_rows_kernel,
        grid=(b, nt),
        in_specs=[
            pl.BlockSpec((1, tm, D_MODEL), lambda bi, ti: (bi, ti, 0)),
            pl.BlockSpec((1, D_MODEL), lambda bi, ti: (0, 0)),
            pl.BlockSpec((D_MODEL, n_kv), lambda bi, ti: (0, 0)),
            pl.BlockSpec((3, GROUP_LANES), lambda bi, ti: (0, 0)),
            pl.BlockSpec((tm, LANES), lambda bi, ti: (ti, 0)),
            pl.BlockSpec((tm, LANES), lambda bi, ti: (ti, 0)),
            pl.BlockSpec((GROUP_LANES, GROUP_LANES), lambda bi, ti: (0, 0)),
        ],
        out_specs=out_specs,
        out_shape=out_shape,
        scratch_shapes=scratch,
        compiler_params=_params("parallel", "parallel"),
        name="kv_rows",
    )(x, gain.reshape(1, D_MODEL), w, k_norm_tiled, cos, sin, ones_bd)


def _cmp_partial_kernel(x_ref, wk_ref, wv_ref, o_ref):
    acc_k = jnp.zeros((x_ref.shape[0], 2 * GROUP_LANES), F32)
    acc_v = jnp.zeros((x_ref.shape[0], 2 * GROUP_LANES), F32)
    for p in range(CMP_STRIDE):
        lo = p * 2 * GROUP_LANES
        acc_k += jnp.dot(x_ref[:, lo:lo + GROUP_LANES], wk_ref[p], preferred_element_type=F32)
        acc_v += jnp.dot(x_ref[:, lo + GROUP_LANES:lo + 2 * GROUP_LANES], wv_ref[p], preferred_element_type=F32)
    o_ref[:, 0:2 * GROUP_LANES] = acc_k
    o_ref[:, 2 * GROUP_LANES:4 * GROUP_LANES] = acc_v


def cmp_partial(tok_chunks, wk_bd, wv_bd):
    n, width = tok_chunks.shape
    tm = min(n, 512)
    return pl.pallas_call(
        _cmp_partial_kernel,
        grid=(n // tm,),
        in_specs=[
            pl.BlockSpec((tm, width), lambda i: (i, 0)),
            pl.BlockSpec(wk_bd.shape, lambda i: (0, 0, 0)),
            pl.BlockSpec(wv_bd.shape, lambda i: (0, 0, 0)),
        ],
        out_specs=pl.BlockSpec((tm, 4 * GROUP_LANES), lambda i: (i, 0)),
        out_shape=jax.ShapeDtypeStruct((n, 4 * GROUP_LANES), F32),
        compiler_params=_params("parallel"),
        name="cmp_partial",
    )(tok_chunks, wk_bd, wv_bd)


def _cmp_combine_kernel(a_ref, pe_ref, w1_ref, b1_ref, w2_ref, kn_ref, cos_ref, sin_ref, ones_ref, kc_ref, vc_ref):
    a = a_ref[0]
    n = a.shape[0]
    for t, out_ref in enumerate((kc_ref, vc_ref)):
        first = a[:, 2 * t * GROUP_LANES:(2 * t + 1) * GROUP_LANES]
        second = a[:, (2 * t + 1) * GROUP_LANES:(2 * t + 2) * GROUP_LANES]
        second = pltpu.roll(second, n - 1, 0)
        pe_term = jnp.dot(pe_ref[t], w1_ref[t], preferred_element_type=F32)[0:1]
        h = b1_ref[t] + pe_term + first + second
        y = jnp.dot(jax.nn.gelu(h).astype(BF16), w2_ref[t], preferred_element_type=F32)
        if t == 0:
            y = _seg_rms(y, ones_ref[...], kn_ref[0:1, :])
            y = jnp.concatenate(
                [_rope64(y[:, :LANES], cos_ref[...], sin_ref[...]), _rope64(y[:, LANES:], cos_ref[...], sin_ref[...])],
                axis=1)
        yt = jnp.transpose(y)
        for g in range(NSA_GROUPS):
            out_ref[0, g] = yt[g * HEAD_DIM:(g + 1) * HEAD_DIM, :].astype(BF16)


def cmp_combine(partial, pe_rows, w1_tiled, b1_tiled, w2_bd, k_norm_tiled, cos, sin, ones_bd):
    b, ncp, _ = partial.shape
    full = lambda *shape: pl.BlockSpec(shape, lambda bi: (0,) * len(shape))
    return pl.pallas_call(
        _cmp_combine_kernel,
        grid=(b,),
        in_specs=[
            pl.BlockSpec((1, ncp, 4 * GROUP_LANES), lambda bi: (bi, 0, 0)),
            full(*pe_rows.shape), full(*w1_tiled.shape), full(*b1_tiled.shape), full(*w2_bd.shape),
            full(3, GROUP_LANES), full(ncp, LANES), full(ncp, LANES), full(GROUP_LANES, GROUP_LANES),
        ],
        out_specs=[pl.BlockSpec((1, NSA_GROUPS, HEAD_DIM, ncp), lambda bi: (bi, 0, 0, 0))] * 2,
        out_shape=[jax.ShapeDtypeStruct((b, NSA_GROUPS, HEAD_DIM, ncp), BF16)] * 2,
        compiler_params=_params("parallel"),
        name="cmp_combine",
    )(partial, pe_rows, w1_tiled, b1_tiled, w2_bd, k_norm_tiled, cos, sin, ones_bd)


def _nsa_q_kernel(x_ref, g_ref, wq_ref, wg_ref, qn_ref, cos_ref, sin_ref, ones_ref, q_ref, gate_ref):
    xn = _rms(x_ref[0], g_ref[...]).astype(BF16)
    y = jnp.dot(xn, wq_ref[...], preferred_element_type=F32)
    cos = cos_ref[...]
    sin = sin_ref[...]
    scale = HEAD_DIM ** -0.5 * LOG2E
    for g in range(NSA_GROUPS):
        yg = _seg_rms(y[:, g * GROUP_LANES:(g + 1) * GROUP_LANES], ones_ref[...], qn_ref[...])
        low = lax.broadcasted_iota(I32, (y.shape[0], LANES), 1) < HEAD_DIM
        for half in range(2):
            r = _rope64(yg[:, half * LANES:(half + 1) * LANES], cos, sin) * scale
            q_ref[0, g, 2 * half] = jnp.where(low, r, 0.0)
            q_ref[0, g, 2 * half + 1] = jnp.where(low, pltpu.roll(r, HEAD_DIM, 1), 0.0)
    gates = jnp.dot(xn, wg_ref[...], preferred_element_type=F32)
    gate_ref[0] = jax.nn.sigmoid(gates)


def nsa_q(x, gain, wq, wg, layer, q_norm_tiled, cos, sin, ones_bd):
    b, t, _ = x.shape
    tm = min(t, 512)
    return pl.pallas_call(
        _nsa_q_kernel,
        grid=(b, t // tm),
        in_specs=[
            pl.BlockSpec((1, tm, D_MODEL), lambda bi, ti: (bi, ti, 0)),
            pl.BlockSpec((1, D_MODEL), lambda bi, ti: (0, 0)),
            pl.BlockSpec((None,) + wq.shape[1:], lambda bi, ti: (layer, 0, 0)),
            pl.BlockSpec((None,) + wg.shape[1:], lambda bi, ti: (layer, 0, 0)),
            pl.BlockSpec((1, GROUP_LANES), lambda bi, ti: (0, 0)),
            pl.BlockSpec((tm, LANES), lambda bi, ti: (ti, 0)),
            pl.BlockSpec((tm, LANES), lambda bi, ti: (ti, 0)),
            pl.BlockSpec((GROUP_LANES, GROUP_LANES), lambda bi, ti: (0, 0)),
        ],
        out_specs=[
            pl.BlockSpec((1, NSA_GROUPS, HEADS_PER_GROUP, tm, LANES), lambda bi, ti: (bi, 0, 0, ti, 0)),
            pl.BlockSpec((1, tm, NSA_GROUPS * LANES), lambda bi, ti: (bi, ti, 0)),
        ],
        out_shape=[
            jax.ShapeDtypeStruct((b, NSA_GROUPS, HEADS_PER_GROUP, t, LANES), F32),
            jax.ShapeDtypeStruct((b, t, NSA_GROUPS * LANES), F32),
        ],
        compiler_params=_params("parallel", "parallel"),
        name="nsa_q",
    )(x, gain.reshape(1, D_MODEL), wq, wg, q_norm_tiled, cos, sin, ones_bd)


def _compressed_branch(q, qpos, q_first, kc_t, vc_t, ovt, nq, nc, ns):
    hpg = HEADS_PER_GROUP
    ncp = kc_t.shape[1]
    nsp = ovt.shape[0]
    s_c = jnp.dot(q, kc_t, preferred_element_type=F32)
    cidx = lax.broadcasted_iota(I32, (1, ncp), 1)
    valid_c = (cidx * CMP_STRIDE + (CMP_BLOCK - 1) <= qpos) & (cidx < nc)
    e_c, l_c = _exp_rows(s_c, valid_c)
    p_c = jnp.where(valid_c, e_c * (1.0 / l_c), 0.0)
    o_c = lax.dot_general(p_c.astype(BF16), vc_t, NT_DIMS, preferred_element_type=F32)
    p_sum = p_c[0:nq]
    for hh in range(1, hpg):
        p_sum = p_sum + p_c[hh * nq:(hh + 1) * nq]
    p_hi = p_sum.astype(BF16)
    p_lo = (p_sum - p_hi.astype(F32)).astype(BF16)
    imp_t = (lax.dot_general(ovt, p_hi, NT_DIMS, preferred_element_type=F32)
             + lax.dot_general(ovt, p_lo, NT_DIMS, preferred_element_type=F32))
    sidx = lax.broadcasted_iota(I32, (nsp, 1), 0)
    q_blk = (q_first + lax.broadcasted_iota(I32, (1, nq), 1)) // SEL_BLOCK
    valid_s = (sidx <= q_blk) & (sidx < ns)
    forced = (sidx == 0) | (valid_s & (q_blk - sidx < N_LOCAL))
    score_t = jnp.where(forced, BIG, jnp.where(valid_s, imp_t, NEG))
    return o_c, score_t, valid_s


def _window_branch(q, qpos, kw_t, vw_t, kpos0):
    s_w = jnp.dot(q, kw_t, preferred_element_type=F32)
    kpos_w = kpos0 + lax.broadcasted_iota(I32, (1, kw_t.shape[1]), 1)
    kpos_w = jnp.where(kpos_w >= 0, kpos_w, -(2 ** 30))
    behind = lax.bitcast_convert_type(qpos - kpos_w, jnp.uint32)
    e_w, l_w = _exp_rows(s_w, behind < jnp.uint32(WINDOW))
    return lax.dot_general(e_w.astype(BF16), vw_t, NT_DIMS, preferred_element_type=F32), l_w


def _gate_and_store(o_ref, lane0, gates, o_c, acc_s, l_s, acc_w, l_w, nq):
    inv_s = 1.0 / l_s
    inv_w = 1.0 / l_w
    for hh in range(HEADS_PER_GROUP):
        rows = slice(hh * nq, (hh + 1) * nq)
        o_h = (gates[:, 3 * hh:3 * hh + 1] * o_c[rows] + (gates[:, 3 * hh + 1:3 * hh + 2] * inv_s[rows]) * acc_s[rows]
               + (gates[:, 3 * hh + 2:3 * hh + 3] * inv_w[rows]) * acc_w[rows])
        o_ref[0, :, lane0 + hh * HEAD_DIM:lane0 + (hh + 1) * HEAD_DIM] = o_h.astype(BF16)


def _nsa_prompt_kernel(q_ref, gate_ref, kc_ref, vc_ref, ovt_ref, ksel_ref, vsel_ref, kwin_ref, vwin_ref,
                       o_ref, score_scr, s_a, s_b, *, nq, nc, ns, kc_keys):
    i = pl.program_id(2)
    hpg = HEADS_PER_GROUP
    r = hpg * nq
    nsp = ovt_ref.shape[0]
    t_len = kwin_ref.shape[3]
    q_first = i * nq
    q_wide = q_ref[0, 0].reshape(r, LANES)
    q = q_wide[:, 0:HEAD_DIM].astype(BF16)
    qpos = q_first + lax.broadcasted_iota(I32, (r, 1), 0) % nq
    o_c, score_t, valid_s = _compressed_branch(q, qpos, q_first, kc_ref[0, 0], vc_ref[0, 0], ovt_ref[...], nq, nc, ns)

    score_scr[...] = score_t
    sidx = lax.broadcasted_iota(I32, (nsp, 1), 0)
    n_live = jnp.minimum((q_first + nq - 1) // SEL_BLOCK + 1, ns)

    def rank_body(sp, rank):
        row = score_scr[pl.ds(sp, 1), :]
        ahead = (row > score_t) | ((row == score_t) & (sidx > sp))
        return rank + ahead.astype(F32)

    rank = lax.fori_loop(0, n_live, rank_body, jnp.zeros((nsp, nq), F32))
    dropped_t = 1.0 - ((rank < float(min(N_SELECT, ns))) & valid_s).astype(F32)
    dropped = jnp.transpose(jnp.concatenate([jnp.zeros((HEAD_DIM, nq), F32), dropped_t], axis=0))
    qa = (q_wide + jnp.concatenate([dropped] * hpg, axis=0)).astype(BF16)

    def scores_into(dst, c):
        start = pl.multiple_of(c * kc_keys, kc_keys)
        dst[...] = jnp.dot(qa, ksel_ref[0, 0, :, pl.ds(start, kc_keys)], preferred_element_type=F32)

    def update(src, c, carry, causal):
        m, l, acc = carry
        start = pl.multiple_of(c * kc_keys, kc_keys)
        s = src[...]
        if causal:
            kpos = c * kc_keys + lax.broadcasted_iota(I32, (1, kc_keys), 1)
            s = jnp.where(kpos <= qpos, s, NEG)
        m_new = jnp.maximum(m, jnp.max(s, axis=-1, keepdims=True))
        alpha = jnp.exp2(m - m_new)
        p = jnp.exp2(s - m_new)
        l = alpha * l + jnp.sum(p, axis=-1, keepdims=True)
        pv = lax.dot_general(p.astype(BF16), vsel_ref[0, 0, :, pl.ds(start, kc_keys)], NT_DIMS,
                             preferred_element_type=F32)
        return m_new, l, alpha * acc + pv

    n_past = q_first // kc_keys
    scores_into(s_a, 0)

    def pair_body(pi, carry):
        c = 2 * pi
        scores_into(s_b, c + 1)
        carry = update(s_a, c, carry, causal=False)
        scores_into(s_a, c + 2)
        return update(s_b, c + 1, carry, causal=False)

    init = (jnp.full((r, 1), NEG, F32), jnp.zeros((r, 1), F32), jnp.zeros((r, HEAD_DIM), F32))
    carry = lax.fori_loop(0, n_past // 2, pair_body, init)

    def odd_tail(carry):
        scores_into(s_b, n_past)
        carry = update(s_a, n_past - 1, carry, causal=False)
        return update(s_b, n_past, carry, causal=True)

    def even_tail(carry):
        return update(s_a, n_past, carry, causal=True)

    _, l_s, acc_s = lax.cond(n_past % 2 == 1, odd_tail, even_tail, carry)

    w0 = pl.multiple_of(jnp.clip(q_first + nq - WIN_SPAN, 0, t_len - WIN_SPAN), LANES)
    acc_w, l_w = _window_branch(q, qpos, kwin_ref[0, 0, :, pl.ds(w0, WIN_SPAN)],
                                vwin_ref[0, 0, :, pl.ds(w0, WIN_SPAN)], w0)
    _gate_and_store(o_ref, 0, gate_ref[0], o_c, acc_s, l_s, acc_w, l_w, nq)


def nsa_attend_prompt(q, gates, kc_t, vc_t, ovt, ksel_t, vsel_t, kwin_t, vwin_t, *, nc, ns):
    b, _, _, t, _ = q.shape
    assert ovt.shape[0] == MASK_ROWS == LANES - HEAD_DIM
    nq = PROMPT_NQ
    kc_keys = min(512, t)
    ncp = kc_t.shape[3]
    nsp = ovt.shape[0]
    per_bg = lambda rows, cols: pl.BlockSpec((1, 1, rows, cols), lambda bi, g, ti: (bi, g, 0, 0))
    kernel = functools.partial(_nsa_prompt_kernel, nq=nq, nc=nc, ns=ns, kc_keys=kc_keys)
    return pl.pallas_call(
        kernel,
        grid=(b, NSA_GROUPS, t // nq),
        in_specs=[
            pl.BlockSpec((1, 1, HEADS_PER_GROUP, nq, LANES), lambda bi, g, ti: (bi, g, 0, ti, 0)),
            pl.BlockSpec((1, nq, LANES), lambda bi, g, ti: (bi, ti, g)),
            per_bg(HEAD_DIM, ncp), per_bg(HEAD_DIM, ncp),
            pl.BlockSpec((nsp, ncp), lambda bi, g, ti: (0, 0)),
            per_bg(HEAD_DIM + MASK_ROWS, t), per_bg(HEAD_DIM, t), per_bg(HEAD_DIM, t), per_bg(HEAD_DIM, t),
        ],
        out_specs=pl.BlockSpec((1, nq, GROUP_LANES), lambda bi, g, ti: (bi, ti, g)),
        out_shape=jax.ShapeDtypeStruct((b, t, NSA_HEADS * HEAD_DIM), BF16),
        scratch_shapes=[pltpu.VMEM((nsp, nq), F32), pltpu.VMEM((HEADS_PER_GROUP * nq, kc_keys), F32),
                        pltpu.VMEM((HEADS_PER_GROUP * nq, kc_keys), F32)],
        compiler_params=_params("parallel", "parallel", "arbitrary"),
        name="nsa_attend_prompt",
    )(q, gates, kc_t, vc_t, ovt, ksel_t, vsel_t, kwin_t, vwin_t)


def _nsa_decode_kernel(q_ref, gate_ref, kc_ref, vc_ref, ovt_ref, ind_ref, ksel_ref, vsel_ref, kwin_ref, vwin_ref,
                       o_ref, *, nq, nc, ns, q_pos0, win_pos0):
    hpg = HEADS_PER_GROUP
    r = hpg * nq
    nsp = ovt_ref.shape[0]
    l_keys = ksel_ref.shape[3]
    qpos = q_pos0 + lax.broadcasted_iota(I32, (r, 1), 0) % nq
    s_other = lax.broadcasted_iota(I32, (nsp, 1), 0)
    s_self = lax.broadcasted_iota(I32, (1, nsp), 1)
    qrow = lax.broadcasted_iota(I32, (nq, 1), 0)
    kpos = lax.broadcasted_iota(I32, (1, l_keys), 1)
    for g in range(NSA_GROUPS):
        q = q_ref[0, g].reshape(r, LANES)[:, 0:HEAD_DIM].astype(BF16)
        o_c, score_t, valid_s = _compressed_branch(q, qpos, q_pos0, kc_ref[0, g], vc_ref[0, g], ovt_ref[...],
                                                   nq, nc, ns)
        score = jnp.transpose(score_t)
        valid = jnp.transpose(valid_s.astype(F32))
        rank = jnp.zeros((nq, nsp), F32)
        for qi in range(nq):
            other = score_t[:, qi:qi + 1]
            own = score[qi:qi + 1, :]
            ahead = (other > own) | ((other == own) & (s_other < s_self))
            rank = jnp.where(qrow == qi, jnp.sum(ahead.astype(F32), axis=0, keepdims=True), rank)
        sel = ((rank < float(min(N_SELECT, ns))) & (valid > 0.5)).astype(F32)
        sel_rows = jnp.concatenate([sel] * hpg, axis=0).astype(BF16)

        s = jnp.dot(q, ksel_ref[0, g], preferred_element_type=F32)
        picked = jnp.dot(sel_rows, ind_ref[...], preferred_element_type=F32)
        e_s, l_s = _exp_rows(s, (picked > 0.5) & (kpos <= qpos))
        acc_s = lax.dot_general(e_s.astype(BF16), vsel_ref[0, g], NT_DIMS, preferred_element_type=F32)

        acc_w, l_w = _window_branch(q, qpos, kwin_ref[0, g], vwin_ref[0, g], win_pos0)
        _gate_and_store(o_ref, g * GROUP_LANES, gate_ref[0, :, g * LANES:(g + 1) * LANES], o_c, acc_s, l_s, acc_w, l_w, nq)


def nsa_attend_decode(q, gates, kc_t, vc_t, ovt, ind, ksel_t, vsel_t, kwin_t, vwin_t, *, nc, ns, q_pos0, win_pos0):
    b, _, _, nq, _ = q.shape
    ncp = kc_t.shape[3]
    nsp = ovt.shape[0]
    l_keys = ksel_t.shape[3]
    per_b = lambda cols: pl.BlockSpec((1, NSA_GROUPS, HEAD_DIM, cols), lambda bi: (bi, 0, 0, 0))
    kernel = functools.partial(_nsa_decode_kernel, nq=nq, nc=nc, ns=ns, q_pos0=q_pos0, win_pos0=win_pos0)
    return pl.pallas_call(
        kernel,
        grid=(b,),
        in_specs=[
            pl.BlockSpec((1, NSA_GROUPS, HEADS_PER_GROUP, nq, LANES), lambda bi: (bi, 0, 0, 0, 0)),
            pl.BlockSpec((1, nq, NSA_GROUPS * LANES), lambda bi: (bi, 0, 0)),
            per_b(ncp), per_b(ncp),
            pl.BlockSpec((nsp, ncp), lambda bi: (0, 0)),
            pl.BlockSpec((nsp, l_keys), lambda bi: (0, 0)),
            per_b(l_keys), per_b(l_keys), per_b(WIN_SPAN), per_b(WIN_SPAN),
        ],
        out_specs=pl.BlockSpec((1, nq, NSA_HEADS * HEAD_DIM), lambda bi: (bi, 0, 0)),
        out_shape=jax.ShapeDtypeStruct((b, nq, NSA_HEADS * HEAD_DIM), BF16),
        compiler_params=_params("parallel"),
        name="nsa_attend_decode",
    )(q, gates, kc_t, vc_t, ovt, ind, ksel_t, vsel_t, kwin_t, vwin_t)


def _gather_kernel(pt_ref, *refs, n_steps):
    page_refs = refs[:GATHER_PAGES]
    new_ref, perm_ref, cmp_ref, ksel_ref, vsel_ref = refs[GATHER_PAGES:]
    p = pl.program_id(1)
    page = page_refs[0].shape[2]
    k_lo = 2 * GROUP_LANES
    v_lo = 3 * GROUP_LANES

    @pl.when(p < n_steps - 1)
    def _():
        cmp_feats = []
        for k, page_ref in enumerate(page_refs):
            x = page_ref[0]
            cols = slice(k * page, (k + 1) * page)
            cmp_feats.append(x[0:2 * GROUP_LANES, :].astype(BF16))
            for g in range(NSA_GROUPS):
                ksel_ref[0, g, :, cols] = x[k_lo + g * HEAD_DIM:k_lo + (g + 1) * HEAD_DIM, :].astype(BF16)
                vsel_ref[0, g, :, cols] = x[v_lo + g * HEAD_DIM:v_lo + (g + 1) * HEAD_DIM, :].astype(BF16)
        feats = jnp.concatenate(cmp_feats, axis=1)
        toks = lax.dot_general(perm_ref[...], feats, NT_DIMS, preferred_element_type=F32).astype(BF16)
        n_chunks = toks.shape[0] // CMP_STRIDE
        for pos in range(CMP_STRIDE):
            cmp_ref[:, pos * 2 * GROUP_LANES:(pos + 1) * 2 * GROUP_LANES] = toks[pos * n_chunks:(pos + 1) * n_chunks, :]

    @pl.when(p == n_steps - 1)
    def _():
        new = new_ref[0]
        padded = jnp.concatenate([new, jnp.zeros((GATHER_PAGES * page - new.shape[0], new.shape[1]), F32)], axis=0)
        kt = jnp.transpose(padded[:, k_lo:k_lo + GROUP_LANES])
        vt = jnp.transpose(padded[:, v_lo:v_lo + GROUP_LANES])
        for g in range(NSA_GROUPS):
            ksel_ref[0, g] = kt[g * HEAD_DIM:(g + 1) * HEAD_DIM, :].astype(BF16)
            vsel_ref[0, g] = vt[g * HEAD_DIM:(g + 1) * HEAD_DIM, :].astype(BF16)


def gather_past(page_table, cache_t, new_rows):
    db, n_pages = page_table.shape
    page = cache_t.shape[2]
    dq = new_rows.shape[1]
    past = n_pages * page
    n_full = n_pages // GATHER_PAGES
    n_steps = n_full + 1
    step_keys = GATHER_PAGES * page
    chunk_lanes = CMP_STRIDE * 2 * GROUP_LANES
    n_chunks = step_keys // CMP_STRIDE
    out_row = np.arange(step_keys)
    perm = jnp.asarray((out_row % n_chunks * CMP_STRIDE + out_row // n_chunks)[:, None] == np.arange(step_keys)[None, :],
                       BF16)

    def page_spec(k):
        return pl.BlockSpec(
            (1, 4 * GROUP_LANES, page),
            lambda b, p, pt: (pt[b * n_pages + jnp.minimum(p, n_full - 1) * GATHER_PAGES + k], 0, 0))

    grid_spec = pltpu.PrefetchScalarGridSpec(
        num_scalar_prefetch=1,
        grid=(db, n_steps),
        in_specs=[page_spec(k) for k in range(GATHER_PAGES)] + [
            pl.BlockSpec((1, dq, 4 * GROUP_LANES), lambda b, p, pt: (b, 0, 0)),
            pl.BlockSpec((step_keys, step_keys), lambda b, p, pt: (0, 0))],
        out_specs=[
            pl.BlockSpec((step_keys // CMP_STRIDE, chunk_lanes),
                         lambda b, p, pt: (b * n_full + jnp.minimum(p, n_full - 1), 0)),
            pl.BlockSpec((1, NSA_GROUPS, HEAD_DIM, step_keys), lambda b, p, pt: (b, 0, 0, p)),
            pl.BlockSpec((1, NSA_GROUPS, HEAD_DIM, step_keys), lambda b, p, pt: (b, 0, 0, p)),
        ],
    )
    return pl.pallas_call(
        functools.partial(_gather_kernel, n_steps=n_steps),
        grid_spec=grid_spec,
        out_shape=[
            jax.ShapeDtypeStruct((db * past // CMP_STRIDE, chunk_lanes), BF16),
            jax.ShapeDtypeStruct((db, NSA_GROUPS, HEAD_DIM, past + step_keys), BF16),
            jax.ShapeDtypeStruct((db, NSA_GROUPS, HEAD_DIM, past + step_keys), BF16),
        ],
        compiler_params=_params("parallel", "arbitrary"),
        name="gather_past",
    )(page_table.reshape(-1), *([cache_t] * GATHER_PAGES), new_rows, perm)


def _win_assemble_kernel(cache_ref, new_ref, win_ref, kwin_ref, vwin_ref):
    old = cache_ref[0]
    new = new_ref[0]
    buf = old.shape[0]
    dq = new.shape[0]
    win_ref[0, 0:buf - dq, :] = old[dq:, :]
    win_ref[0, buf - dq:buf, :] = new
    old_t = jnp.transpose(old)
    tail = jnp.concatenate([new, jnp.zeros((WIN_SPAN - buf - dq, new.shape[1]), F32)], axis=0)
    tail_t = jnp.transpose(tail)
    for ref, off in ((kwin_ref, 0), (vwin_ref, GROUP_LANES)):
        for g in range(NSA_GROUPS):
            rows = slice(off + g * HEAD_DIM, off + (g + 1) * HEAD_DIM)
            ref[0, g] = jnp.concatenate([old_t[rows, :], tail_t[rows, :]], axis=1).astype(BF16)


def win_assemble(cache_win, new_win):
    db, buf, width = cache_win.shape
    dq = new_win.shape[1]
    return pl.pallas_call(
        _win_assemble_kernel,
        grid=(db,),
        in_specs=[
            pl.BlockSpec((1, buf, width), lambda b: (b, 0, 0)),
            pl.BlockSpec((1, dq, width), lambda b: (b, 0, 0)),
        ],
        out_specs=[
            pl.BlockSpec((1, buf, width), lambda b: (b, 0, 0)),
            pl.BlockSpec((1, NSA_GROUPS, HEAD_DIM, WIN_SPAN), lambda b: (b, 0, 0, 0)),
            pl.BlockSpec((1, NSA_GROUPS, HEAD_DIM, WIN_SPAN), lambda b: (b, 0, 0, 0)),
        ],
        out_shape=[
            jax.ShapeDtypeStruct((db, buf, width), F32),
            jax.ShapeDtypeStruct((db, NSA_GROUPS, HEAD_DIM, WIN_SPAN), BF16),
            jax.ShapeDtypeStruct((db, NSA_GROUPS, HEAD_DIM, WIN_SPAN), BF16),
        ],
        compiler_params=_params("parallel"),
        name="win_assemble",
    )(cache_win, new_win)


def _rope_angles(pos, half):
    inv = ROPE_THETA ** (-jnp.arange(half, dtype=F32) / half)
    ang = pos.astype(F32)[:, None] * inv[None, :]
    return jnp.cos(ang), jnp.sin(ang)


def _rope_tables_head64(pos):
    cos, sin = _rope_angles(pos, HEAD_DIM // 2)
    return jnp.concatenate([cos] * 4, axis=1), jnp.concatenate([-sin, sin] * 2, axis=1)


def _block_diag_groups(w):
    eye = jnp.eye(NSA_GROUPS, dtype=w.dtype)
    out = jnp.einsum("gh,...dn->...gdhn", eye, w)
    return out.reshape(*w.shape[:-2], GROUP_LANES, NSA_GROUPS * w.shape[-1])


def _compress_weights(cmp_w1, cmp_b1, cmp_w2, cmp_pe):
    r = CMP_BLOCK // CMP_STRIDE
    w1 = cmp_w1.reshape(2, r, CMP_STRIDE, HEAD_DIM, HEAD_DIM)
    bd = _block_diag_groups(w1)
    bd = jnp.concatenate([bd[:, 0], bd[:, 1]], axis=-1).astype(BF16)
    pe_rows = jnp.broadcast_to(cmp_pe.reshape(2, 1, CMP_BLOCK * HEAD_DIM), (2, 8, CMP_BLOCK * HEAD_DIM)).astype(BF16)
    w1_tiled = jnp.tile(cmp_w1, (1, 1, NSA_GROUPS)).astype(BF16)
    b1_tiled = jnp.tile(cmp_b1, (1, NSA_GROUPS)).reshape(2, 1, GROUP_LANES)
    w2_bd = _block_diag_groups(cmp_w2).astype(BF16)
    return bd[0], bd[1], pe_rows, w1_tiled, b1_tiled, w2_bd


def _overlap_table(nc, ncp, ns, nsp):
    ci = np.arange(ncp)[None, :]
    sj = np.arange(nsp)[:, None]
    overlap_t = ((ci * CMP_STRIDE < (sj + 1) * SEL_BLOCK) & (ci * CMP_STRIDE + CMP_BLOCK > sj * SEL_BLOCK)
                 & (ci < nc) & (sj < ns))
    return jnp.asarray(overlap_t, BF16)


def _block_membership(length, nsp):
    return jnp.asarray((np.arange(length)[None, :] // SEL_BLOCK) == np.arange(nsp)[:, None], BF16)


def _round_up(x, m):
    return -(-x // m) * m


def _trunk(x, pos, ret_s0, past, w):
    (ffn_norm, ffn_w_in, ffn_w_out, ret_norm, ret_w_in, ret_w_out, kv_norm, kv_w, k_norm_tiled,
     cmp_weights, nsa_norm, nsa_wq, nsa_wg, q_norm_tiled, nsa_w_out, ones_bd) = w
    b, t, _ = x.shape
    n = b * t
    prompt = past is None
    xf = x.reshape(n, D_MODEL)
    pos_rows = jnp.tile(pos, b) if not prompt else pos
    ret_cos, ret_sin = _rope_angles(pos_rows, RET_DK // 2)
    cos64, sin64 = _rope_tables_head64(pos)
    ret_states = []
    rows = win = attend = None
    for layer in range(DEPTH):
        if layer == N_A_LAYERS:
            xs = xf.reshape(b, t, D_MODEL)
            if prompt:
                rows, win, cmp_tok, ksel, vsel, kwin, vwin = kv_rows(
                    xs, kv_norm, kv_w, k_norm_tiled, cos64, sin64, ones_bd, aux=True)
                length = t
                new_win = win[:, t - min(WINDOW, t):]
            else:
                page_table, cache_t, cache_win = past
                rows, win = kv_rows(xs, kv_norm, kv_w, k_norm_tiled, cos64, sin64, ones_bd, aux=False)
                cmp_tok, ksel, vsel = gather_past(page_table, cache_t, rows)
                new_win, kwin, vwin = win_assemble(cache_win, win)
                past_len = page_table.shape[1] * cache_t.shape[2]
                length = past_len + t
            nc = (length - CMP_BLOCK) // CMP_STRIDE + 1
            n_chunk_rows = nc + CMP_BLOCK // CMP_STRIDE - 1
            ns = -(-length // SEL_BLOCK)
            wk_bd, wv_bd, pe_rows, w1_tiled, b1_tiled, w2_bd = cmp_weights
            assert cmp_tok.shape[0] == b * n_chunk_rows
            partial = cmp_partial(cmp_tok, wk_bd, wv_bd).reshape(b, n_chunk_rows, 4 * GROUP_LANES)
            c_end = jnp.arange(n_chunk_rows, dtype=I32) * CMP_STRIDE + (CMP_BLOCK - 1)
            cos_c, sin_c = _rope_tables_head64(c_end)
            kc_t, vc_t = cmp_combine(partial, pe_rows, w1_tiled, b1_tiled, w2_bd, k_norm_tiled, cos_c, sin_c, ones_bd)
            if prompt:
                assert ns <= MASK_ROWS and t % PROMPT_NQ == 0
                ovt = _overlap_table(nc, n_chunk_rows, ns, MASK_ROWS)
                attend = functools.partial(nsa_attend_prompt, kc_t=kc_t, vc_t=vc_t, ovt=ovt, ksel_t=ksel, vsel_t=vsel,
                                           kwin_t=kwin, vwin_t=vwin, nc=nc, ns=ns)
            else:
                nsp = _round_up(ns, 16)
                ovt = _overlap_table(nc, n_chunk_rows, ns, nsp)
                ind = _block_membership(ksel.shape[3], nsp)
                attend = functools.partial(nsa_attend_decode, kc_t=kc_t, vc_t=vc_t, ovt=ovt, ind=ind, ksel_t=ksel,
                                           vsel_t=vsel, kwin_t=kwin, vwin_t=vwin, nc=nc, ns=ns,
                                           q_pos0=int(past_len), win_pos0=int(past_len - cache_win.shape[1]))
        xf = ffn_half(xf, ffn_norm[layer, 0], ffn_w_in, ffn_w_out, layer, 0)
        if layer < N_A_LAYERS:
            qkvg = ret_inproj(xf, ret_norm[layer], ret_w_in, layer, ret_cos, ret_sin)
            gated, s_fin = ret_core(qkvg.reshape(b, t, 6 * D_MODEL), ret_s0, layer)
            ret_states.append(s_fin)
            mix, w_mix, mix_layer = gated.reshape(n, 2 * D_MODEL), ret_w_out, layer
        else:
            j = layer - N_A_LAYERS
            q, gates = nsa_q(xf.reshape(b, t, D_MODEL), nsa_norm[j], nsa_wq, nsa_wg, j, q_norm_tiled[j],
                             cos64, sin64, ones_bd)
            o = attend(q, gates)
            mix, w_mix, mix_layer = o.reshape(n, NSA_HEADS * HEAD_DIM), nsa_w_out, j
        xf = proj_ffn_half(mix, w_mix, mix_layer, xf, ffn_norm[layer, 1], ffn_w_in, ffn_w_out, layer, 1)
    return xf.reshape(b, t, D_MODEL), jnp.stack(ret_states), rows, new_win


def kernel(x_prompt, x_sample, state_ret, cache_kv, cache_win, page_table, ffn_norm, ffn_w_in, ffn_w_out, ret_norm,
           ret_w_in, ret_w_out, kv_norm, kv_w, k_norm, cmp_w1, cmp_b1, cmp_w2, cmp_pe, nsa_norm, nsa_w_in, q_norm,
           nsa_w_out):
    b, t, _ = x_prompt.shape
    db, dq, _ = x_sample.shape
    n_phys, page = cache_kv.shape[:2]
    past_len = page_table.shape[1] * page
    n_q_cols = NSA_HEADS * HEAD_DIM

    gate_w = nsa_w_in[:, :, n_q_cols:].reshape(-1, D_MODEL, NSA_GROUPS, N_GATES)
    gate_w = jnp.pad(gate_w, ((0, 0), (0, 0), (0, 0), (0, LANES - N_GATES))).reshape(-1, D_MODEL, NSA_GROUPS * LANES)
    eye = np.arange(GROUP_LANES)
    ones_bd = jnp.asarray((eye[:, None] // HEAD_DIM) == (eye[None, :] // HEAD_DIM), BF16)
    w = (ffn_norm, ffn_w_in.astype(BF16), ffn_w_out.astype(BF16), ret_norm, ret_w_in.astype(BF16),
         ret_w_out.astype(BF16), kv_norm, kv_w.astype(BF16), jnp.tile(k_norm, (1, NSA_GROUPS)),
         _compress_weights(cmp_w1, cmp_b1, cmp_w2, cmp_pe), nsa_norm, nsa_w_in[:, :, :n_q_cols].astype(BF16),
         gate_w.astype(BF16), jnp.tile(q_norm, (1, NSA_GROUPS)).reshape(-1, 1, GROUP_LANES),
         nsa_w_out.astype(BF16), ones_bd)

    pos_p = jnp.arange(t, dtype=I32)
    pos_s = past_len + jnp.arange(dq, dtype=I32)
    y_p, ret_p, rows_p, win_p = _trunk(x_prompt, pos_p, None, None, w)
    cache_t = jnp.transpose(cache_kv, (0, 2, 3, 4, 1)).reshape(n_phys, 4 * GROUP_LANES, page)
    cwin = cache_win.reshape(db, cache_win.shape[1], 2 * GROUP_LANES)
    y_s, ret_s, rows_s, win_s = _trunk(x_sample, pos_s, state_ret, (page_table, cache_t, cwin), w)
    kv_shape = (4, NSA_GROUPS, HEAD_DIM)
    win_shape = (2, NSA_GROUPS, HEAD_DIM)
    return (y_p, y_s, ret_p.astype(state_ret.dtype), ret_s.astype(state_ret.dtype),
            rows_p.reshape(b, t, *kv_shape), rows_s.reshape(db, dq, *kv_shape),
            win_p.reshape(b, win_p.shape[1], *win_shape), win_s.reshape(db, win_s.shape[1], *win_shape))
```

```python
import functools

import jax
import jax.numpy as jnp
import numpy as np
from jax import lax
from jax.experimental import pallas as pl
from jax.experimental.pallas import tpu as pltpu

F32 = jnp.float32
BF16 = jnp.bfloat16
I32 = jnp.int32

D_MODEL = 1024
DEPTH = 4
N_A_LAYERS = DEPTH // 2
RET_HEADS = 4
RET_DK = D_MODEL // RET_HEADS
RET_DV = 2 * D_MODEL // RET_HEADS
RET_CHUNK = 128
NSA_HEADS = 16
NSA_GROUPS = 4
HEADS_PER_GROUP = NSA_HEADS // NSA_GROUPS
HEAD_DIM = D_MODEL // NSA_HEADS
CMP_BLOCK = 32
CMP_STRIDE = 16
SEL_BLOCK = 64
N_SELECT = 16
N_LOCAL = 2
WINDOW = 512
D_FF = 2816
ROPE_THETA = 10000.0
EPS = 1e-6
NEG = -1e30
BIG = 1e9
N_GATES = 3 * HEADS_PER_GROUP
GROUP_LANES = NSA_GROUPS * HEAD_DIM
PROMPT_NQ = 128
WIN_SPAN = WINDOW + PROMPT_NQ
MASK_ROWS = 64
MASK_BIAS = -(2.0 ** 100)
GATHER_PAGES = 4
LOG2E = 1.4426950408889634

VMEM_LIMIT_BYTES = 56 * 1024 * 1024
LANES = 128

NT_DIMS = (((1,), (1,)), ((), ()))
TN_DIMS = (((0,), (0,)), ((), ()))


def _params(*semantics):
    return pltpu.CompilerParams(dimension_semantics=semantics, vmem_limit_bytes=VMEM_LIMIT_BYTES)


def _rms(x, gain):
    ms = jnp.mean(x * x, axis=-1, keepdims=True)
    return x * lax.rsqrt(ms + EPS) * gain


def _seg_rms(y, ones_bd, gain):
    sq = y * y
    hi = sq.astype(BF16)
    lo = (sq - hi.astype(F32)).astype(BF16)
    ss = jnp.dot(hi, ones_bd, preferred_element_type=F32) + jnp.dot(lo, ones_bd, preferred_element_type=F32)
    return y * lax.rsqrt(ss * (1.0 / HEAD_DIM) + EPS) * gain


def _rope64(x, cos, sin_signed):
    lane = lax.broadcasted_iota(I32, x.shape, 1)
    first_half = (lane % HEAD_DIM) < (HEAD_DIM // 2)
    rot = jnp.where(first_half, pltpu.roll(x, LANES - HEAD_DIM // 2, 1), pltpu.roll(x, HEAD_DIM // 2, 1))
    return x * cos + rot * sin_signed


def _exp_rows(s, ok):
    s = jnp.where(ok, s, NEG)
    m = jnp.max(s, axis=-1, keepdims=True)
    e = jnp.exp2(s - m)
    return e, jnp.sum(e, axis=-1, keepdims=True)


def _ffn_kernel(x_ref, g_ref, wi_ref, wo_ref, o_ref):
    x = x_ref[...]
    xn = _rms(x, g_ref[...]).astype(BF16)
    a = jnp.dot(xn, wi_ref[:, 0:D_FF], preferred_element_type=F32)
    b = jnp.dot(xn, wi_ref[:, D_FF:2 * D_FF], preferred_element_type=F32)
    h = (a * jax.nn.sigmoid(a) * b).astype(BF16)
    o_ref[...] = x + 0.5 * jnp.dot(h, wo_ref[...], preferred_element_type=F32)


def ffn_half(x, gain, w_in, w_out, layer, half):
    n = x.shape[0]
    tm = min(n, 512)
    resident = pl.Buffered(1)
    return pl.pallas_call(
        _ffn_kernel,
        grid=(n // tm,),
        in_specs=[
            pl.BlockSpec((tm, D_MODEL), lambda i: (i, 0)),
            pl.BlockSpec((1, D_MODEL), lambda i: (0, 0)),
            pl.BlockSpec((None, None, D_MODEL, 2 * D_FF), lambda i: (layer, half, 0, 0), pipeline_mode=resident),
            pl.BlockSpec((None, None, D_FF, D_MODEL), lambda i: (layer, half, 0, 0), pipeline_mode=resident),
        ],
        out_specs=pl.BlockSpec((tm, D_MODEL), lambda i: (i, 0)),
        out_shape=jax.ShapeDtypeStruct((n, D_MODEL), F32),
        compiler_params=_params("parallel"),
        name="ffn_half",
    )(x, gain.reshape(1, D_MODEL), w_in, w_out)


def _proj_ffn_kernel(a_ref, wp_ref, x_ref, g_ref, wi_ref, wo_ref, o_ref):
    x = x_ref[...] + jnp.dot(a_ref[...], wp_ref[...], preferred_element_type=F32)
    xn = _rms(x, g_ref[...]).astype(BF16)
    a = jnp.dot(xn, wi_ref[:, 0:D_FF], preferred_element_type=F32)
    b = jnp.dot(xn, wi_ref[:, D_FF:2 * D_FF], preferred_element_type=F32)
    h = (a * jax.nn.sigmoid(a) * b).astype(BF16)
    o_ref[...] = x + 0.5 * jnp.dot(h, wo_ref[...], preferred_element_type=F32)


def proj_ffn_half(a, w_proj, proj_layer, x, gain, w_in, w_out, layer, half):
    n, k = a.shape
    tm = min(n, 512)
    resident = pl.Buffered(1)
    return pl.pallas_call(
        _proj_ffn_kernel,
        grid=(n // tm,),
        in_specs=[
            pl.BlockSpec((tm, k), lambda i: (i, 0)),
            pl.BlockSpec((None, k, D_MODEL), lambda i: (proj_layer, 0, 0), pipeline_mode=resident),
            pl.BlockSpec((tm, D_MODEL), lambda i: (i, 0)),
            pl.BlockSpec((1, D_MODEL), lambda i: (0, 0)),
            pl.BlockSpec((None, None, D_MODEL, 2 * D_FF), lambda i: (layer, half, 0, 0), pipeline_mode=resident),
            pl.BlockSpec((None, None, D_FF, D_MODEL), lambda i: (layer, half, 0, 0), pipeline_mode=resident),
        ],
        out_specs=pl.BlockSpec((tm, D_MODEL), lambda i: (i, 0)),
        out_shape=jax.ShapeDtypeStruct((n, D_MODEL), F32),
        compiler_params=_params("parallel"),
        name="proj_ffn_half",
    )(a, w_proj, x, gain.reshape(1, D_MODEL), w_in, w_out)


def _ret_inproj_kernel(x_ref, g_ref, w_ref, cos_ref, sin_ref, o_ref):
    xn = _rms(x_ref[...], g_ref[...]).astype(BF16)
    c = cos_ref[...]
    s = sin_ref[...]
    half = RET_DK // 2
    for h in range(2 * RET_HEADS):
        lo = h * RET_DK
        y = jnp.dot(xn, w_ref[:, lo:lo + RET_DK], preferred_element_type=F32)
        scale = 1.0 if h < RET_HEADS else RET_DK ** -0.5
        x1 = y[:, :half]
        x2 = y[:, half:]
        o_ref[:, lo:lo + half] = ((x1 * c - x2 * s) * scale).astype(BF16)
        o_ref[:, lo + half:lo + RET_DK] = ((x1 * s + x2 * c) * scale).astype(BF16)
    for h in range(2 * RET_HEADS):
        lo = 2 * D_MODEL + h * RET_DV
        o_ref[:, lo:lo + RET_DV] = jnp.dot(xn, w_ref[:, lo:lo + RET_DV], preferred_element_type=F32).astype(BF16)


def ret_inproj(x, gain, w, layer, cos, sin):
    n = x.shape[0]
    p = cos.shape[0]
    tm = min(n, 512, p)
    n_out = w.shape[2]
    tab_blocks = p // tm
    return pl.pallas_call(
        _ret_inproj_kernel,
        grid=(n // tm,),
        in_specs=[
            pl.BlockSpec((tm, D_MODEL), lambda i: (i, 0)),
            pl.BlockSpec((1, D_MODEL), lambda i: (0, 0)),
            pl.BlockSpec((None, D_MODEL, n_out), lambda i: (layer, 0, 0), pipeline_mode=pl.Buffered(1)),
            pl.BlockSpec((tm, RET_DK // 2), lambda i: (i % tab_blocks, 0)),
            pl.BlockSpec((tm, RET_DK // 2), lambda i: (i % tab_blocks, 0)),
        ],
        out_specs=pl.BlockSpec((tm, n_out), lambda i: (i, 0)),
        out_shape=jax.ShapeDtypeStruct((n, n_out), BF16),
        compiler_params=_params("parallel"),
        name="ret_inproj",
    )(x, gain.reshape(1, D_MODEL), w, cos, sin)


def _ret_core_kernel(*refs, chunk, n_inner, hps, has_s0):
    if has_s0:
        (q_ref, k_ref, v_ref, g_ref, dm_ref, qd_ref, kd_ref, sd_ref, s0_ref, o_ref, so_ref, s_scr) = refs
    else:
        (q_ref, k_ref, v_ref, g_ref, dm_ref, qd_ref, kd_ref, sd_ref, o_ref, so_ref, s_scr) = refs
    t = pl.program_id(2)

    @pl.when(t == 0)
    def _():
        if has_s0:
            s_scr[...] = s0_ref[0]
        else:
            s_scr[...] = jnp.zeros_like(s_scr)

    for c in range(n_inner):
        rows = slice(c * chunk, (c + 1) * chunk)
        for h in range(hps):
            qk_cols = slice(h * RET_DK, (h + 1) * RET_DK)
            v_cols = slice(h * RET_DV, (h + 1) * RET_DV)
            q = q_ref[0, rows, qk_cols]
            k = k_ref[0, rows, qk_cols]
            v = v_ref[0, rows, v_cols]
            g = g_ref[0, rows, v_cols].astype(F32)
            s = s_scr[h]
            scores = lax.dot_general(q, k, NT_DIMS, preferred_element_type=F32) * dm_ref[h]
            intra = jnp.dot(scores.astype(BF16), v, preferred_element_type=F32)
            cross = jnp.dot((q.astype(F32) * qd_ref[h]).astype(BF16), s.astype(BF16), preferred_element_type=F32)
            o = intra + cross
            kv = lax.dot_general((k.astype(F32) * kd_ref[h]).astype(BF16), v, TN_DIMS, preferred_element_type=F32)
            s_scr[h] = sd_ref[h, 0:1, 0:1] * s + kv
            mu = jnp.mean(o, axis=-1, keepdims=True)
            d = o - mu
            var = jnp.mean(d * d, axis=-1, keepdims=True)
            on = d * lax.rsqrt(var + EPS)
            o_ref[0, rows, v_cols] = (g * jax.nn.sigmoid(g) * on).astype(BF16)

    @pl.when(t == pl.num_programs(2) - 1)
    def _():
        so_ref[0] = s_scr[...]


def _decay_tables(chunk):
    lg = jnp.log(1.0 - 2.0 ** (-5.0 - jnp.arange(RET_HEADS, dtype=F32)))
    idx = jnp.arange(chunk, dtype=F32)
    rel = idx[:, None] - idx[None, :]
    dmat = jnp.where(rel >= 0, jnp.exp(jnp.maximum(rel, 0.0)[None] * lg[:, None, None]), 0.0)
    qdec = jnp.exp((idx + 1.0)[None, :] * lg[:, None])
    kdec = jnp.exp((chunk - 1.0 - idx)[None, :] * lg[:, None])
    sdec = jnp.exp(chunk * lg)
    qdec = jnp.broadcast_to(qdec[:, :, None], (RET_HEADS, chunk, RET_DK))
    kdec = jnp.broadcast_to(kdec[:, :, None], (RET_HEADS, chunk, RET_DK))
    sdec = jnp.broadcast_to(sdec[:, None, None], (RET_HEADS, 8, LANES))
    return dmat, qdec, kdec, sdec


def ret_core(qkvg, s0_all, layer):
    b, t, _ = qkvg.shape
    chunk = RET_CHUNK if t % RET_CHUNK == 0 else t
    tb = min(t, 4 * chunk)
    n_inner = tb // chunk
    hps = RET_HEADS if t < RET_CHUNK else 1
    nh = RET_HEADS // hps
    dmat, qdec, kdec, sdec = _decay_tables(chunk)
    in_specs = [
        pl.BlockSpec((1, tb, hps * RET_DK), lambda bi, h, ti: (bi, ti, h)),
        pl.BlockSpec((1, tb, hps * RET_DK), lambda bi, h, ti: (bi, ti, nh + h)),
        pl.BlockSpec((1, tb, hps * RET_DV), lambda bi, h, ti: (bi, ti, nh + h)),
        pl.BlockSpec((1, tb, hps * RET_DV), lambda bi, h, ti: (bi, ti, 2 * nh + h)),
        pl.BlockSpec((hps, chunk, chunk), lambda bi, h, ti: (h, 0, 0)),
        pl.BlockSpec((hps, chunk, RET_DK), lambda bi, h, ti: (h, 0, 0)),
        pl.BlockSpec((hps, chunk, RET_DK), lambda bi, h, ti: (h, 0, 0)),
        pl.BlockSpec((hps, 8, LANES), lambda bi, h, ti: (h, 0, 0)),
    ]
    args = [qkvg, qkvg, qkvg, qkvg, dmat, qdec, kdec, sdec]
    if s0_all is not None:
        in_specs.append(pl.BlockSpec((None, 1, hps, RET_DK, RET_DV), lambda bi, h, ti: (layer, bi, h, 0, 0)))
        args.append(s0_all)
    return pl.pallas_call(
        functools.partial(_ret_core_kernel, chunk=chunk, n_inner=n_inner, hps=hps, has_s0=s0_all is not None),
        grid=(b, nh, t // tb),
        in_specs=in_specs,
        out_specs=[
            pl.BlockSpec((1, tb, hps * RET_DV), lambda bi, h, ti: (bi, ti, h)),
            pl.BlockSpec((1, hps, RET_DK, RET_DV), lambda bi, h, ti: (bi, h, 0, 0)),
        ],
        out_shape=[
            jax.ShapeDtypeStruct((b, t, 2 * D_MODEL), BF16),
            jax.ShapeDtypeStruct((b, RET_HEADS, RET_DK, RET_DV), F32),
        ],
        scratch_shapes=[pltpu.VMEM((hps, RET_DK, RET_DV), F32)],
        compiler_params=_params("parallel", "parallel", "arbitrary"),
        name="ret_core",
    )(*args)


def _store_chunk_major(cmp_ref, scr, n_rows):
    n_planes = scr.shape[0]
    for p in range(CMP_STRIDE):
        for c in range(n_planes):
            lo = p * n_planes * LANES + c * LANES
            cmp_ref[:, lo:lo + LANES] = scr[c, pl.ds(p, n_rows // CMP_STRIDE, stride=CMP_STRIDE), :].astype(BF16)


def _kv_rows_kernel(x_ref, g_ref, w_ref, kn_ref, cos_ref, sin_ref, ones_ref, rows_ref, win_ref, *aux_refs):
    xn = _rms(x_ref[0], g_ref[...]).astype(BF16)
    y = jnp.dot(xn, w_ref[...], preferred_element_type=F32)
    tm = y.shape[0]
    cos = cos_ref[...]
    sin = sin_ref[...]
    ones_bd = ones_ref[...]
    slot = lambda s: y[:, s * GROUP_LANES:(s + 1) * GROUP_LANES]

    def norm_rope(v, gain):
        vn = _seg_rms(v, ones_bd, gain)
        return jnp.concatenate([_rope64(vn[:, :LANES], cos, sin), _rope64(vn[:, LANES:], cos, sin)], axis=1)

    k_slc = norm_rope(slot(2), kn_ref[1:2, :])
    k_win = norm_rope(slot(4), kn_ref[2:3, :])
    rows_ref[0, :, 0:2 * GROUP_LANES] = y[:, 0:2 * GROUP_LANES]
    rows_ref[0, :, 2 * GROUP_LANES:3 * GROUP_LANES] = k_slc
    rows_ref[0, :, 3 * GROUP_LANES:4 * GROUP_LANES] = slot(3)
    win_ref[0, :, 0:GROUP_LANES] = k_win
    win_ref[0, :, GROUP_LANES:2 * GROUP_LANES] = slot(5)
    if aux_refs:
        cmp_ref, ksel_ref, vsel_ref, kwin_ref, vwin_ref, scr = aux_refs
        for c in range(scr.shape[0]):
            scr[c] = y[:, c * LANES:(c + 1) * LANES]
        _store_chunk_major(cmp_ref, scr, tm)
        for ref, val in ((ksel_ref, k_slc), (vsel_ref, slot(3)), (kwin_ref, k_win), (vwin_ref, slot(5))):
            vt = jnp.transpose(val)
            for g in range(NSA_GROUPS):
                ref[0, g, 0:HEAD_DIM, :] = vt[g * HEAD_DIM:(g + 1) * HEAD_DIM, :].astype(BF16)
        key_blk = (pl.program_id(1) * tm + lax.broadcasted_iota(I32, (MASK_ROWS, tm), 1)) // SEL_BLOCK
        mask_rows = jnp.where(key_blk == lax.broadcasted_iota(I32, (MASK_ROWS, tm), 0), MASK_BIAS, 0.0).astype(BF16)
        for g in range(NSA_GROUPS):
            ksel_ref[0, g, HEAD_DIM:HEAD_DIM + MASK_ROWS, :] = mask_rows


def kv_rows(x, gain, w, k_norm_tiled, cos, sin, ones_bd, aux):
    b, t, _ = x.shape
    tm = min(t, 512)
    n_kv = w.shape[1]
    nt = t // tm
    out_specs = [
        pl.BlockSpec((1, tm, 4 * GROUP_LANES), lambda bi, ti: (bi, ti, 0)),
        pl.BlockSpec((1, tm, 2 * GROUP_LANES), lambda bi, ti: (bi, ti, 0)),
    ]
    out_shape = [
        jax.ShapeDtypeStruct((b, t, 4 * GROUP_LANES), F32),
        jax.ShapeDtypeStruct((b, t, 2 * GROUP_LANES), F32),
    ]
    scratch = []
    if aux:
        chunk_lanes = CMP_STRIDE * 2 * GROUP_LANES
        out_specs.append(pl.BlockSpec((tm // CMP_STRIDE, chunk_lanes), lambda bi, ti: (bi * nt + ti, 0)))
        out_shape.append(jax.ShapeDtypeStruct((b * t // CMP_STRIDE, chunk_lanes), BF16))
        for rows in (HEAD_DIM + MASK_ROWS, HEAD_DIM, HEAD_DIM, HEAD_DIM):
            out_specs.append(pl.BlockSpec((1, NSA_GROUPS, rows, tm), lambda bi, ti: (bi, 0, 0, ti)))
            out_shape.append(jax.ShapeDtypeStruct((b, NSA_GROUPS, rows, t), BF16))
        scratch.append(pltpu.VMEM((2 * GROUP_LANES // LANES, tm, LANES), F32))
    return pl.pallas_call(
        _kv_rows_kernel,
        grid=(b, nt),
        in_specs=[
            pl.BlockSpec((1, tm, D_MODEL), lambda bi, ti: (bi, ti, 0)),
            pl.BlockSpec((1, D_MODEL), lambda bi, ti: (0, 0)),
            pl.BlockSpec((D_MODEL, n_kv), lambda bi, ti: (0, 0)),
            pl.BlockSpec((3, GROUP_LANES), lambda bi, ti: (0, 0)),
            pl.BlockSpec((tm, LANES), lambda bi, ti: (ti, 0)),
            pl.BlockSpec((tm, LANES), lambda bi, ti: (ti, 0)),
            pl.BlockSpec((GROUP_LANES, GROUP_LANES), lambda bi, ti: (0, 0)),
        ],
        out_specs=out_specs,
        out_shape=out_shape,
        scratch_shapes=scratch,
        compiler_params=_params("parallel", "parallel"),
        name="kv_rows",
    )(x, gain.reshape(1, D_MODEL), w, k_norm_tiled, cos, sin, ones_bd)


def _cmp_partial_kernel(x_ref, wk_ref, wv_ref, o_ref):
    acc_k = jnp.zeros((x_ref.shape[0], 2 * GROUP_LANES), F32)
    acc_v = jnp.zeros((x_ref.shape[0], 2 * GROUP_LANES), F32)
    for p in range(CMP_STRIDE):
        lo = p * 2 * GROUP_LANES
        acc_k += jnp.dot(x_ref[:, lo:lo + GROUP_LANES], wk_ref[p], preferred_element_type=F32)
        acc_v += jnp.dot(x_ref[:, lo + GROUP_LANES:lo + 2 * GROUP_LANES], wv_ref[p], preferred_element_type=F32)
    o_ref[:, 0:2 * GROUP_LANES] = acc_k
    o_ref[:, 2 * GROUP_LANES:4 * GROUP_LANES] = acc_v


def cmp_partial(tok_chunks, wk_bd, wv_bd):
    n, width = tok_chunks.shape
    tm = min(n, 512)
    return pl.pallas_call(
        _cmp_partial_kernel,
        grid=(n // tm,),
        in_specs=[
            pl.BlockSpec((tm, width), lambda i: (i, 0)),
            pl.BlockSpec(wk_bd.shape, lambda i: (0, 0, 0)),
            pl.BlockSpec(wv_bd.shape, lambda i: (0, 0, 0)),
        ],
        out_specs=pl.BlockSpec((tm, 4 * GROUP_LANES), lambda i: (i, 0)),
        out_shape=jax.ShapeDtypeStruct((n, 4 * GROUP_LANES), F32),
        compiler_params=_params("parallel"),
        name="cmp_partial",
    )(tok_chunks, wk_bd, wv_bd)


def _cmp_combine_kernel(a_ref, pe_ref, w1_ref, b1_ref, w2_ref, kn_ref, cos_ref, sin_ref, ones_ref, kc_ref, vc_ref):
    a = a_ref[0]
    n = a.shape[0]
    for t, out_ref in enumerate((kc_ref, vc_ref)):
        first = a[:, 2 * t * GROUP_LANES:(2 * t + 1) * GROUP_LANES]
        second = a[:, (2 * t + 1) * GROUP_LANES:(2 * t + 2) * GROUP_LANES]
        second = pltpu.roll(second, n - 1, 0)
        pe_term = jnp.dot(pe_ref[t], w1_ref[t], preferred_element_type=F32)[0:1]
        h = b1_ref[t] + pe_term + first + second
        y = jnp.dot(jax.nn.gelu(h).astype(BF16), w2_ref[t], preferred_element_type=F32)
        if t == 0:
            y = _seg_rms(y, ones_ref[...], kn_ref[0:1, :])
            y = jnp.concatenate(
                [_rope64(y[:, :LANES], cos_ref[...], sin_ref[...]), _rope64(y[:, LANES:], cos_ref[...], sin_ref[...])],
                axis=1)
        yt = jnp.transpose(y)
        for g in range(NSA_GROUPS):
            out_ref[0, g] = yt[g * HEAD_DIM:(g + 1) * HEAD_DIM, :].astype(BF16)


def cmp_combine(partial, pe_rows, w1_tiled, b1_tiled, w2_bd, k_norm_tiled, cos, sin, ones_bd):
    b, ncp, _ = partial.shape
    full = lambda *shape: pl.BlockSpec(shape, lambda bi: (0,) * len(shape))
    return pl.pallas_call(
        _cmp_combine_kernel,
        grid=(b,),
        in_specs=[
            pl.BlockSpec((1, ncp, 4 * GROUP_LANES), lambda bi: (bi, 0, 0)),
            full(*pe_rows.shape), full(*w1_tiled.shape), full(*b1_tiled.shape), full(*w2_bd.shape),
            full(3, GROUP_LANES), full(ncp, LANES), full(ncp, LANES), full(GROUP_LANES, GROUP_LANES),
        ],
        out_specs=[pl.BlockSpec((1, NSA_GROUPS, HEAD_DIM, ncp), lambda bi: (bi, 0, 0, 0))] * 2,
        out_shape=[jax.ShapeDtypeStruct((b, NSA_GROUPS, HEAD_DIM, ncp), BF16)] * 2,
        compiler_params=_params("parallel"),
        name="cmp_combine",
    )(partial, pe_rows, w1_tiled, b1_tiled, w2_bd, k_norm_tiled, cos, sin, ones_bd)


def _nsa_q_kernel(x_ref, g_ref, wq_ref, wg_ref, qn_ref, cos_ref, sin_ref, ones_ref, q_ref, gate_ref):
    xn = _rms(x_ref[0], g_ref[...]).astype(BF16)
    y = jnp.dot(xn, wq_ref[...], preferred_element_type=F32)
    cos = cos_ref[...]
    sin = sin_ref[...]
    scale = HEAD_DIM ** -0.5 * LOG2E
    for g in range(NSA_GROUPS):
        yg = _seg_rms(y[:, g * GROUP_LANES:(g + 1) * GROUP_LANES], ones_ref[...], qn_ref[...])
        low = lax.broadcasted_iota(I32, (y.shape[0], LANES), 1) < HEAD_DIM
        for half in range(2):
            r = _rope64(yg[:, half * LANES:(half + 1) * LANES], cos, sin) * scale
            q_ref[0, g, 2 * half] = jnp.where(low, r, 0.0)
            q_ref[0, g, 2 * half + 1] = jnp.where(low, pltpu.roll(r, HEAD_DIM, 1), 0.0)
    gates = jnp.dot(xn, wg_ref[...], preferred_element_type=F32)
    gate_ref[0] = jax.nn.sigmoid(gates)


def nsa_q(x, gain, wq, wg, layer, q_norm_tiled, cos, sin, ones_bd):
    b, t, _ = x.shape
    tm = min(t, 512)
    return pl.pallas_call(
        _nsa_q_kernel,
        grid=(b, t // tm),
        in_specs=[
            pl.BlockSpec((1, tm, D_MODEL), lambda bi, ti: (bi, ti, 0)),
            pl.BlockSpec((1, D_MODEL), lambda bi, ti: (0, 0)),
            pl.BlockSpec((None,) + wq.shape[1:], lambda bi, ti: (layer, 0, 0)),
            pl.BlockSpec((None,) + wg.shape[1:], lambda bi, ti: (layer, 0, 0)),
            pl.BlockSpec((1, GROUP_LANES), lambda bi, ti: (0, 0)),
            pl.BlockSpec((tm, LANES), lambda bi, ti: (ti, 0)),
            pl.BlockSpec((tm, LANES), lambda bi, ti: (ti, 0)),
            pl.BlockSpec((GROUP_LANES, GROUP_LANES), lambda bi, ti: (0, 0)),
        ],
        out_specs=[
            pl.BlockSpec((1, NSA_GROUPS, HEADS_PER_GROUP, tm, LANES), lambda bi, ti: (bi, 0, 0, ti, 0)),
            pl.BlockSpec((1, tm, NSA_GROUPS * LANES), lambda bi, ti: (bi, ti, 0)),
        ],
        out_shape=[
            jax.ShapeDtypeStruct((b, NSA_GROUPS, HEADS_PER_GROUP, t, LANES), F32),
            jax.ShapeDtypeStruct((b, t, NSA_GROUPS * LANES), F32),
        ],
        compiler_params=_params("parallel", "parallel"),
        name="nsa_q",
    )(x, gain.reshape(1, D_MODEL), wq, wg, q_norm_tiled, cos, sin, ones_bd)


def _compressed_branch(q, qpos, q_first, kc_t, vc_t, ovt, nq, nc, ns):
    hpg = HEADS_PER_GROUP
    ncp = kc_t.shape[1]
    nsp = ovt.shape[0]
    s_c = jnp.dot(q, kc_t, preferred_element_type=F32)
    cidx = lax.broadcasted_iota(I32, (1, ncp), 1)
    c_end = jnp.where(cidx < nc, cidx * CMP_STRIDE + (CMP_BLOCK - 1), 2 ** 30)
    valid_c = c_end <= qpos
    e_c, l_c = _exp_rows(s_c, valid_c)
    p_c = jnp.where(valid_c, e_c * (1.0 / l_c), 0.0)
    o_c = lax.dot_general(p_c.astype(BF16), vc_t, NT_DIMS, preferred_element_type=F32)
    p_sum = p_c[0:nq]
    for hh in range(1, hpg):
        p_sum = p_sum + p_c[hh * nq:(hh + 1) * nq]
    p_hi = p_sum.astype(BF16)
    p_lo = (p_sum - p_hi.astype(F32)).astype(BF16)
    imp_t = (lax.dot_general(ovt, p_hi, NT_DIMS, preferred_element_type=F32)
             + lax.dot_general(ovt, p_lo, NT_DIMS, preferred_element_type=F32))
    sidx = lax.broadcasted_iota(I32, (nsp, 1), 0)
    q_blk = (q_first + lax.broadcasted_iota(I32, (1, nq), 1)) // SEL_BLOCK
    valid_s = (sidx <= q_blk) & (sidx < ns)
    forced = (sidx == 0) | (valid_s & (q_blk - sidx < N_LOCAL))
    score_t = jnp.where(forced, BIG, jnp.where(valid_s, imp_t, NEG))
    return o_c, score_t, valid_s


def _window_branch(q, qpos, kw_t, vw_t, kpos0):
    s_w = jnp.dot(q, kw_t, preferred_element_type=F32)
    kpos_w = kpos0 + lax.broadcasted_iota(I32, (1, kw_t.shape[1]), 1)
    kpos_w = jnp.where(kpos_w >= 0, kpos_w, -(2 ** 30))
    behind = lax.bitcast_convert_type(qpos - kpos_w, jnp.uint32)
    e_w, l_w = _exp_rows(s_w, behind < jnp.uint32(WINDOW))
    return lax.dot_general(e_w.astype(BF16), vw_t, NT_DIMS, preferred_element_type=F32), l_w


def _gate_and_store(o_ref, lane0, gates, o_c, acc_s, l_s, acc_w, l_w, nq):
    inv_s = 1.0 / l_s
    inv_w = 1.0 / l_w
    for hh in range(HEADS_PER_GROUP):
        rows = slice(hh * nq, (hh + 1) * nq)
        o_h = (gates[:, 3 * hh:3 * hh + 1] * o_c[rows] + (gates[:, 3 * hh + 1:3 * hh + 2] * inv_s[rows]) * acc_s[rows]
               + (gates[:, 3 * hh + 2:3 * hh + 3] * inv_w[rows]) * acc_w[rows])
        o_ref[0, :, lane0 + hh * HEAD_DIM:lane0 + (hh + 1) * HEAD_DIM] = o_h.astype(BF16)


def _nsa_prompt_kernel(q_ref, gate_ref, kc_ref, vc_ref, ovt_ref, ksel_ref, vsel_ref, kwin_ref, vwin_ref,
                       o_ref, score_scr, s_a, s_b, *, nq, nc, ns, kc_keys):
    i = pl.program_id(2)
    hpg = HEADS_PER_GROUP
    r = hpg * nq
    nsp = ovt_ref.shape[0]
    t_len = kwin_ref.shape[3]
    q_first = i * nq
    q_wide = q_ref[0, 0].reshape(r, LANES)
    q = q_wide[:, 0:HEAD_DIM].astype(BF16)
    qpos = q_first + lax.broadcasted_iota(I32, (r, 1), 0) % nq
    o_c, score_t, valid_s = _compressed_branch(q, qpos, q_first, kc_ref[0, 0], vc_ref[0, 0], ovt_ref[...], nq, nc, ns)

    score_scr[...] = score_t
    blk = lax.broadcasted_iota(I32, (nsp, nq), 0)
    n_live = jnp.minimum((q_first + nq - 1) // SEL_BLOCK + 1, ns)

    def rank_body(sp, rank):
        row = score_scr[pl.ds(sp, 1), :]
        tie = jnp.where(row == score_t, jnp.where(blk > sp, 1.0, 0.0), 0.0)
        return rank + jnp.where(row > score_t, 1.0, tie)

    rank = lax.fori_loop(0, n_live, rank_body, jnp.zeros((nsp, nq), F32))
    dropped_t = 1.0 - ((rank < float(min(N_SELECT, ns))) & valid_s).astype(F32)
    dropped = jnp.transpose(jnp.concatenate([jnp.zeros((HEAD_DIM, nq), F32), dropped_t], axis=0))
    qa = (q_wide + jnp.concatenate([dropped] * hpg, axis=0)).astype(BF16)

    def scores_into(dst, c):
        start = pl.multiple_of(c * kc_keys, kc_keys)
        dst[...] = jnp.dot(qa, ksel_ref[0, 0, :, pl.ds(start, kc_keys)], preferred_element_type=F32)

    def update(src, c, carry, causal):
        m, l, acc = carry
        start = pl.multiple_of(c * kc_keys, kc_keys)
        s = src[...]
        if causal:
            kpos = c * kc_keys + lax.broadcasted_iota(I32, (1, kc_keys), 1)
            s = jnp.where(kpos <= qpos, s, NEG)
        m_new = jnp.maximum(m, jnp.max(s, axis=-1, keepdims=True))
        alpha = jnp.exp2(m - m_new)
        p = jnp.exp2(s - m_new)
        l = alpha * l + jnp.sum(p, axis=-1, keepdims=True)
        pv = lax.dot_general(p.astype(BF16), vsel_ref[0, 0, :, pl.ds(start, kc_keys)], NT_DIMS,
                             preferred_element_type=F32)
        return m_new, l, alpha * acc + pv

    n_past = q_first // kc_keys
    scores_into(s_a, 0)

    def pair_body(pi, carry):
        c = 2 * pi
        scores_into(s_b, c + 1)
        carry = update(s_a, c, carry, causal=False)
        scores_into(s_a, c + 2)
        return update(s_b, c + 1, carry, causal=False)

    init = (jnp.full((r, 1), NEG, F32), jnp.zeros((r, 1), F32), jnp.zeros((r, HEAD_DIM), F32))
    carry = lax.fori_loop(0, n_past // 2, pair_body, init)

    def odd_tail(carry):
        scores_into(s_b, n_past)
        carry = update(s_a, n_past - 1, carry, causal=False)
        return update(s_b, n_past, carry, causal=True)

    def even_tail(carry):
        return update(s_a, n_past, carry, causal=True)

    _, l_s, acc_s = lax.cond(n_past % 2 == 1, odd_tail, even_tail, carry)

    w0 = pl.multiple_of(jnp.clip(q_first + nq - WIN_SPAN, 0, t_len - WIN_SPAN), LANES)
    acc_w, l_w = _window_branch(q, qpos, kwin_ref[0, 0, :, pl.ds(w0, WIN_SPAN)],
                                vwin_ref[0, 0, :, pl.ds(w0, WIN_SPAN)], w0)
    _gate_and_store(o_ref, 0, gate_ref[0], o_c, acc_s, l_s, acc_w, l_w, nq)


def nsa_attend_prompt(q, gates, kc_t, vc_t, ovt, ksel_t, vsel_t, kwin_t, vwin_t, *, nc, ns):
    b, _, _, t, _ = q.shape
    assert ovt.shape[0] == MASK_ROWS == LANES - HEAD_DIM
    nq = PROMPT_NQ
    kc_keys = min(512, t)
    ncp = kc_t.shape[3]
    nsp = ovt.shape[0]
    per_bg = lambda rows, cols: pl.BlockSpec((1, 1, rows, cols), lambda bi, g, ti: (bi, g, 0, 0))
    kernel = functools.partial(_nsa_prompt_kernel, nq=nq, nc=nc, ns=ns, kc_keys=kc_keys)
    return pl.pallas_call(
        kernel,
        grid=(b, NSA_GROUPS, t // nq),
        in_specs=[
            pl.BlockSpec((1, 1, HEADS_PER_GROUP, nq, LANES), lambda bi, g, ti: (bi, g, 0, ti, 0)),
            pl.BlockSpec((1, nq, LANES), lambda bi, g, ti: (bi, ti, g)),
            per_bg(HEAD_DIM, ncp), per_bg(HEAD_DIM, ncp),
            pl.BlockSpec((nsp, ncp), lambda bi, g, ti: (0, 0)),
            per_bg(HEAD_DIM + MASK_ROWS, t), per_bg(HEAD_DIM, t), per_bg(HEAD_DIM, t), per_bg(HEAD_DIM, t),
        ],
        out_specs=pl.BlockSpec((1, nq, GROUP_LANES), lambda bi, g, ti: (bi, ti, g)),
        out_shape=jax.ShapeDtypeStruct((b, t, NSA_HEADS * HEAD_DIM), BF16),
        scratch_shapes=[pltpu.VMEM((nsp, nq), F32), pltpu.VMEM((HEADS_PER_GROUP * nq, kc_keys), F32),
                        pltpu.VMEM((HEADS_PER_GROUP * nq, kc_keys), F32)],
        compiler_params=_params("parallel", "parallel", "arbitrary"),
        name="nsa_attend_prompt",
    )(q, gates, kc_t, vc_t, ovt, ksel_t, vsel_t, kwin_t, vwin_t)


def _nsa_decode_kernel(q_ref, gate_ref, kc_ref, vc_ref, ovt_ref, ind_ref, ksel_ref, vsel_ref, kwin_ref, vwin_ref,
                       o_ref, *, nq, nc, ns, q_pos0, win_pos0):
    hpg = HEADS_PER_GROUP
    r = hpg * nq
    nsp = ovt_ref.shape[0]
    n_slabs, slab = ksel_ref.shape[2], ksel_ref.shape[4]
    l_keys = n_slabs * slab
    qpos = q_pos0 + lax.broadcasted_iota(I32, (r, 1), 0) % nq
    s_other = lax.broadcasted_iota(I32, (nsp, 1), 0)
    s_self = lax.broadcasted_iota(I32, (1, nsp), 1)
    qrow = lax.broadcasted_iota(I32, (nq, 1), 0)
    kpos = lax.broadcasted_iota(I32, (1, l_keys), 1)
    for g in range(NSA_GROUPS):
        q = q_ref[0, g].reshape(r, LANES)[:, 0:HEAD_DIM].astype(BF16)
        o_c, score_t, valid_s = _compressed_branch(q, qpos, q_pos0, kc_ref[0, g], vc_ref[0, g], ovt_ref[...],
                                                   nq, nc, ns)
        score = jnp.transpose(score_t)
        valid = jnp.transpose(valid_s.astype(F32))
        rank = jnp.zeros((nq, nsp), F32)
        for qi in range(nq):
            other = score_t[:, qi:qi + 1]
            own = score[qi:qi + 1, :]
            ahead = (other > own) | ((other == own) & (s_other < s_self))
            rank = jnp.where(qrow == qi, jnp.sum(ahead.astype(F32), axis=0, keepdims=True), rank)
        sel = ((rank < float(min(N_SELECT, ns))) & (valid > 0.5)).astype(F32)
        sel_rows = jnp.concatenate([sel] * hpg, axis=0).astype(BF16)

        s = jnp.concatenate([jnp.dot(q, ksel_ref[0, g, c], preferred_element_type=F32) for c in range(n_slabs)],
                            axis=1)
        picked = jnp.dot(sel_rows, ind_ref[...], preferred_element_type=F32)
        e_s, l_s = _exp_rows(s, (picked > 0.5) & (kpos <= qpos))
        e_s = e_s.astype(BF16)
        acc_s = jnp.zeros((r, HEAD_DIM), F32)
        for c in range(n_slabs):
            acc_s += lax.dot_general(e_s[:, c * slab:(c + 1) * slab], vsel_ref[0, g, c], NT_DIMS,
                                     preferred_element_type=F32)

        acc_w, l_w = _window_branch(q, qpos, kwin_ref[0, g], vwin_ref[0, g], win_pos0)
        _gate_and_store(o_ref, g * GROUP_LANES, gate_ref[0, :, g * LANES:(g + 1) * LANES], o_c, acc_s, l_s, acc_w, l_w, nq)


def nsa_attend_decode(q, gates, kc_t, vc_t, ovt, ind, ksel_t, vsel_t, kwin_t, vwin_t, *, nc, ns, q_pos0, win_pos0):
    b, _, _, nq, _ = q.shape
    ncp = kc_t.shape[3]
    nsp = ovt.shape[0]
    n_slabs, slab = ksel_t.shape[2], ksel_t.shape[4]
    l_keys = n_slabs * slab
    slabs = pl.BlockSpec((1, NSA_GROUPS, n_slabs, HEAD_DIM, slab), lambda bi: (bi, 0, 0, 0, 0))
    per_b = lambda cols: pl.BlockSpec((1, NSA_GROUPS, HEAD_DIM, cols), lambda bi: (bi, 0, 0, 0))
    kernel = functools.partial(_nsa_decode_kernel, nq=nq, nc=nc, ns=ns, q_pos0=q_pos0, win_pos0=win_pos0)
    return pl.pallas_call(
        kernel,
        grid=(b,),
        in_specs=[
            pl.BlockSpec((1, NSA_GROUPS, HEADS_PER_GROUP, nq, LANES), lambda bi: (bi, 0, 0, 0, 0)),
            pl.BlockSpec((1, nq, NSA_GROUPS * LANES), lambda bi: (bi, 0, 0)),
            per_b(ncp), per_b(ncp),
            pl.BlockSpec((nsp, ncp), lambda bi: (0, 0)),
            pl.BlockSpec((nsp, l_keys), lambda bi: (0, 0)),
            slabs, slabs, per_b(WIN_SPAN), per_b(WIN_SPAN),
        ],
        out_specs=pl.BlockSpec((1, nq, NSA_HEADS * HEAD_DIM), lambda bi: (bi, 0, 0)),
        out_shape=jax.ShapeDtypeStruct((b, nq, NSA_HEADS * HEAD_DIM), BF16),
        compiler_params=_params("parallel"),
        name="nsa_attend_decode",
    )(q, gates, kc_t, vc_t, ovt, ind, ksel_t, vsel_t, kwin_t, vwin_t)


def _gather_kernel(pt_ref, *refs, n_steps):
    page_refs = refs[:GATHER_PAGES]
    new_ref, perm_ref, cmp_ref, ksel_ref, vsel_ref = refs[GATHER_PAGES:]
    p = pl.program_id(1)
    page = page_refs[0].shape[2]
    k_lo = 2 * GROUP_LANES
    v_lo = 3 * GROUP_LANES

    @pl.when(p < n_steps - 1)
    def _():
        cmp_feats = []
        for k, page_ref in enumerate(page_refs):
            x = page_ref[0]
            cols = slice(k * page, (k + 1) * page)
            cmp_feats.append(x[0:2 * GROUP_LANES, :].astype(BF16))
            for g in range(NSA_GROUPS):
                ksel_ref[0, g, 0, :, cols] = x[k_lo + g * HEAD_DIM:k_lo + (g + 1) * HEAD_DIM, :].astype(BF16)
                vsel_ref[0, g, 0, :, cols] = x[v_lo + g * HEAD_DIM:v_lo + (g + 1) * HEAD_DIM, :].astype(BF16)
        feats = jnp.concatenate(cmp_feats, axis=1)
        toks = lax.dot_general(perm_ref[...], feats, NT_DIMS, preferred_element_type=F32).astype(BF16)
        n_chunks = toks.shape[0] // CMP_STRIDE
        for pos in range(CMP_STRIDE):
            cmp_ref[:, pos * 2 * GROUP_LANES:(pos + 1) * 2 * GROUP_LANES] = toks[pos * n_chunks:(pos + 1) * n_chunks, :]

    @pl.when(p == n_steps - 1)
    def _():
        new = new_ref[0]
        padded = jnp.concatenate([new, jnp.zeros((GATHER_PAGES * page - new.shape[0], new.shape[1]), F32)], axis=0)
        kt = jnp.transpose(padded[:, k_lo:k_lo + GROUP_LANES])
        vt = jnp.transpose(padded[:, v_lo:v_lo + GROUP_LANES])
        for g in range(NSA_GROUPS):
            ksel_ref[0, g, 0] = kt[g * HEAD_DIM:(g + 1) * HEAD_DIM, :].astype(BF16)
            vsel_ref[0, g, 0] = vt[g * HEAD_DIM:(g + 1) * HEAD_DIM, :].astype(BF16)


def gather_past(page_table, cache_t, new_rows):
    db, n_pages = page_table.shape
    page = cache_t.shape[2]
    dq = new_rows.shape[1]
    past = n_pages * page
    n_full = n_pages // GATHER_PAGES
    n_steps = n_full + 1
    step_keys = GATHER_PAGES * page
    chunk_lanes = CMP_STRIDE * 2 * GROUP_LANES
    n_chunks = step_keys // CMP_STRIDE
    out_row = np.arange(step_keys)
    perm = jnp.asarray((out_row % n_chunks * CMP_STRIDE + out_row // n_chunks)[:, None] == np.arange(step_keys)[None, :],
                       BF16)

    def page_spec(k):
        return pl.BlockSpec(
            (1, 4 * GROUP_LANES, page),
            lambda b, p, pt: (pt[b * n_pages + jnp.minimum(p, n_full - 1) * GATHER_PAGES + k], 0, 0))

    grid_spec = pltpu.PrefetchScalarGridSpec(
        num_scalar_prefetch=1,
        grid=(db, n_steps),
        in_specs=[page_spec(k) for k in range(GATHER_PAGES)] + [
            pl.BlockSpec((1, dq, 4 * GROUP_LANES), lambda b, p, pt: (b, 0, 0)),
            pl.BlockSpec((step_keys, step_keys), lambda b, p, pt: (0, 0))],
        out_specs=[
            pl.BlockSpec((step_keys // CMP_STRIDE, chunk_lanes),
                         lambda b, p, pt: (b * n_full + jnp.minimum(p, n_full - 1), 0)),
            pl.BlockSpec((1, NSA_GROUPS, 1, HEAD_DIM, step_keys), lambda b, p, pt: (b, 0, p, 0, 0)),
            pl.BlockSpec((1, NSA_GROUPS, 1, HEAD_DIM, step_keys), lambda b, p, pt: (b, 0, p, 0, 0)),
        ],
    )
    return pl.pallas_call(
        functools.partial(_gather_kernel, n_steps=n_steps),
        grid_spec=grid_spec,
        out_shape=[
            jax.ShapeDtypeStruct((db * past // CMP_STRIDE, chunk_lanes), BF16),
            jax.ShapeDtypeStruct((db, NSA_GROUPS, n_steps, HEAD_DIM, step_keys), BF16),
            jax.ShapeDtypeStruct((db, NSA_GROUPS, n_steps, HEAD_DIM, step_keys), BF16),
        ],
        compiler_params=_params("parallel", "arbitrary"),
        name="gather_past",
    )(page_table.reshape(-1), *([cache_t] * GATHER_PAGES), new_rows, perm)


def _win_assemble_kernel(cache_ref, new_ref, win_ref, kwin_ref, vwin_ref):
    old = cache_ref[0]
    new = new_ref[0]
    buf = old.shape[0]
    dq = new.shape[0]
    win_ref[0, 0:buf - dq, :] = old[dq:, :]
    win_ref[0, buf - dq:buf, :] = new
    old_t = jnp.transpose(old)
    tail = jnp.concatenate([new, jnp.zeros((WIN_SPAN - buf - dq, new.shape[1]), F32)], axis=0)
    tail_t = jnp.transpose(tail)
    for ref, off in ((kwin_ref, 0), (vwin_ref, GROUP_LANES)):
        for g in range(NSA_GROUPS):
            rows = slice(off + g * HEAD_DIM, off + (g + 1) * HEAD_DIM)
            ref[0, g] = jnp.concatenate([old_t[rows, :], tail_t[rows, :]], axis=1).astype(BF16)


def win_assemble(cache_win, new_win):
    db, buf, width = cache_win.shape
    dq = new_win.shape[1]
    return pl.pallas_call(
        _win_assemble_kernel,
        grid=(db,),
        in_specs=[
            pl.BlockSpec((1, buf, width), lambda b: (b, 0, 0)),
            pl.BlockSpec((1, dq, width), lambda b: (b, 0, 0)),
        ],
        out_specs=[
            pl.BlockSpec((1, buf, width), lambda b: (b, 0, 0)),
            pl.BlockSpec((1, NSA_GROUPS, HEAD_DIM, WIN_SPAN), lambda b: (b, 0, 0, 0)),
            pl.BlockSpec((1, NSA_GROUPS, HEAD_DIM, WIN_SPAN), lambda b: (b, 0, 0, 0)),
        ],
        out_shape=[
            jax.ShapeDtypeStruct((db, buf, width), F32),
            jax.ShapeDtypeStruct((db, NSA_GROUPS, HEAD_DIM, WIN_SPAN), BF16),
            jax.ShapeDtypeStruct((db, NSA_GROUPS, HEAD_DIM, WIN_SPAN), BF16),
        ],
        compiler_params=_params("parallel"),
        name="win_assemble",
    )(cache_win, new_win)


def _rope_angles(pos, half):
    inv = ROPE_THETA ** (-jnp.arange(half, dtype=F32) / half)
    ang = pos.astype(F32)[:, None] * inv[None, :]
    return jnp.cos(ang), jnp.sin(ang)


def _rope_tables_head64(pos):
    cos, sin = _rope_angles(pos, HEAD_DIM // 2)
    return jnp.concatenate([cos] * 4, axis=1), jnp.concatenate([-sin, sin] * 2, axis=1)


def _block_diag_groups(w):
    eye = jnp.eye(NSA_GROUPS, dtype=w.dtype)
    out = jnp.einsum("gh,...dn->...gdhn", eye, w)
    return out.reshape(*w.shape[:-2], GROUP_LANES, NSA_GROUPS * w.shape[-1])


def _compress_weights(cmp_w1, cmp_b1, cmp_w2, cmp_pe):
    r = CMP_BLOCK // CMP_STRIDE
    w1 = cmp_w1.reshape(2, r, CMP_STRIDE, HEAD_DIM, HEAD_DIM)
    bd = _block_diag_groups(w1)
    bd = jnp.concatenate([bd[:, 0], bd[:, 1]], axis=-1).astype(BF16)
    pe_rows = jnp.broadcast_to(cmp_pe.reshape(2, 1, CMP_BLOCK * HEAD_DIM), (2, 8, CMP_BLOCK * HEAD_DIM)).astype(BF16)
    w1_tiled = jnp.tile(cmp_w1, (1, 1, NSA_GROUPS)).astype(BF16)
    b1_tiled = jnp.tile(cmp_b1, (1, NSA_GROUPS)).reshape(2, 1, GROUP_LANES)
    w2_bd = _block_diag_groups(cmp_w2).astype(BF16)
    return bd[0], bd[1], pe_rows, w1_tiled, b1_tiled, w2_bd


def _overlap_table(nc, ncp, ns, nsp):
    ci = np.arange(ncp)[None, :]
    sj = np.arange(nsp)[:, None]
    overlap_t = ((ci * CMP_STRIDE < (sj + 1) * SEL_BLOCK) & (ci * CMP_STRIDE + CMP_BLOCK > sj * SEL_BLOCK)
                 & (ci < nc) & (sj < ns))
    return jnp.asarray(overlap_t, BF16)


def _block_membership(length, nsp):
    return jnp.asarray((np.arange(length)[None, :] // SEL_BLOCK) == np.arange(nsp)[:, None], BF16)


def _round_up(x, m):
    return -(-x // m) * m


def _trunk(x, pos, ret_s0, past, w):
    (ffn_norm, ffn_w_in, ffn_w_out, ret_norm, ret_w_in, ret_w_out, kv_norm, kv_w, k_norm_tiled,
     cmp_weights, nsa_norm, nsa_wq, nsa_wg, q_norm_tiled, nsa_w_out, ones_bd) = w
    b, t, _ = x.shape
    n = b * t
    prompt = past is None
    xf = x.reshape(n, D_MODEL)
    pos_rows = jnp.tile(pos, b) if not prompt else pos
    ret_cos, ret_sin = _rope_angles(pos_rows, RET_DK // 2)
    cos64, sin64 = _rope_tables_head64(pos)
    ret_states = []
    rows = win = attend = None
    for layer in range(DEPTH):
        if layer == N_A_LAYERS:
            xs = xf.reshape(b, t, D_MODEL)
            if prompt:
                rows, win, cmp_tok, ksel, vsel, kwin, vwin = kv_rows(
                    xs, kv_norm, kv_w, k_norm_tiled, cos64, sin64, ones_bd, aux=True)
                length = t
                new_win = win[:, t - min(WINDOW, t):]
            else:
                page_table, cache_t, cache_win = past
                rows, win = kv_rows(xs, kv_norm, kv_w, k_norm_tiled, cos64, sin64, ones_bd, aux=False)
                cmp_tok, ksel, vsel = gather_past(page_table, cache_t, rows)
                new_win, kwin, vwin = win_assemble(cache_win, win)
                past_len = page_table.shape[1] * cache_t.shape[2]
                length = past_len + t
            nc = (length - CMP_BLOCK) // CMP_STRIDE + 1
            n_chunk_rows = nc + CMP_BLOCK // CMP_STRIDE - 1
            ns = -(-length // SEL_BLOCK)
            wk_bd, wv_bd, pe_rows, w1_tiled, b1_tiled, w2_bd = cmp_weights
            assert cmp_tok.shape[0] == b * n_chunk_rows
            partial = cmp_partial(cmp_tok, wk_bd, wv_bd).reshape(b, n_chunk_rows, 4 * GROUP_LANES)
            c_end = jnp.arange(n_chunk_rows, dtype=I32) * CMP_STRIDE + (CMP_BLOCK - 1)
            cos_c, sin_c = _rope_tables_head64(c_end)
            kc_t, vc_t = cmp_combine(partial, pe_rows, w1_tiled, b1_tiled, w2_bd, k_norm_tiled, cos_c, sin_c, ones_bd)
            if prompt:
                assert ns <= MASK_ROWS and t % PROMPT_NQ == 0
                ovt = _overlap_table(nc, n_chunk_rows, ns, MASK_ROWS)
                attend = functools.partial(nsa_attend_prompt, kc_t=kc_t, vc_t=vc_t, ovt=ovt, ksel_t=ksel, vsel_t=vsel,
                                           kwin_t=kwin, vwin_t=vwin, nc=nc, ns=ns)
            else:
                nsp = _round_up(ns, 16)
                ovt = _overlap_table(nc, n_chunk_rows, ns, nsp)
                ind = _block_membership(ksel.shape[2] * ksel.shape[4], nsp)
                attend = functools.partial(nsa_attend_decode, kc_t=kc_t, vc_t=vc_t, ovt=ovt, ind=ind, ksel_t=ksel,
                                           vsel_t=vsel, kwin_t=kwin, vwin_t=vwin, nc=nc, ns=ns,
                                           q_pos0=int(past_len), win_pos0=int(past_len - cache_win.shape[1]))
        xf = ffn_half(xf, ffn_norm[layer, 0], ffn_w_in, ffn_w_out, layer, 0)
        if layer < N_A_LAYERS:
            qkvg = ret_inproj(xf, ret_norm[layer], ret_w_in, layer, ret_cos, ret_sin)
            gated, s_fin = ret_core(qkvg.reshape(b, t, 6 * D_MODEL), ret_s0, layer)
            ret_states.append(s_fin)
            mix, w_mix, mix_layer = gated.reshape(n, 2 * D_MODEL), ret_w_out, layer
        else:
            j = layer - N_A_LAYERS
            q, gates = nsa_q(xf.reshape(b, t, D_MODEL), nsa_norm[j], nsa_wq, nsa_wg, j, q_norm_tiled[j],
                             cos64, sin64, ones_bd)
            o = attend(q, gates)
            mix, w_mix, mix_layer = o.reshape(n, NSA_HEADS * HEAD_DIM), nsa_w_out, j
        xf = proj_ffn_half(mix, w_mix, mix_layer, xf, ffn_norm[layer, 1], ffn_w_in, ffn_w_out, layer, 1)
    return xf.reshape(b, t, D_MODEL), jnp.stack(ret_states), rows, new_win


def kernel(x_prompt, x_sample, state_ret, cache_kv, cache_win, page_table, ffn_norm, ffn_w_in, ffn_w_out, ret_norm,
           ret_w_in, ret_w_out, kv_norm, kv_w, k_norm, cmp_w1, cmp_b1, cmp_w2, cmp_pe, nsa_norm, nsa_w_in, q_norm,
           nsa_w_out):
    b, t, _ = x_prompt.shape
    db, dq, _ = x_sample.shape
    n_phys, page = cache_kv.shape[:2]
    past_len = page_table.shape[1] * page
    n_q_cols = NSA_HEADS * HEAD_DIM

    gate_w = nsa_w_in[:, :, n_q_cols:].reshape(-1, D_MODEL, NSA_GROUPS, N_GATES)
    gate_w = jnp.pad(gate_w, ((0, 0), (0, 0), (0, 0), (0, LANES - N_GATES))).reshape(-1, D_MODEL, NSA_GROUPS * LANES)
    eye = np.arange(GROUP_LANES)
    ones_bd = jnp.asarray((eye[:, None] // HEAD_DIM) == (eye[None, :] // HEAD_DIM), BF16)
    w = (ffn_norm, ffn_w_in.astype(BF16), ffn_w_out.astype(BF16), ret_norm, ret_w_in.astype(BF16),
         ret_w_out.astype(BF16), kv_norm, kv_w.astype(BF16), jnp.tile(k_norm, (1, NSA_GROUPS)),
         _compress_weights(cmp_w1, cmp_b1, cmp_w2, cmp_pe), nsa_norm, nsa_w_in[:, :, :n_q_cols].astype(BF16),
         gate_w.astype(BF16), jnp.tile(q_norm, (1, NSA_GROUPS)).reshape(-1, 1, GROUP_LANES),
         nsa_w_out.astype(BF16), ones_bd)

    pos_p = jnp.arange(t, dtype=I32)
    pos_s = past_len + jnp.arange(dq, dtype=I32)
    y_p, ret_p, rows_p, win_p = _trunk(x_prompt, pos_p, None, None, w)
    cache_t = jnp.transpose(cache_kv, (0, 2, 3, 4, 1)).reshape(n_phys, 4 * GROUP_LANES, page)
    cwin = cache_win.reshape(db, cache_win.shape[1], 2 * GROUP_LANES)
    y_s, ret_s, rows_s, win_s = _trunk(x_sample, pos_s, state_ret, (page_table, cache_t, cwin), w)
    kv_shape = (4, NSA_GROUPS, HEAD_DIM)
    win_shape = (2, NSA_GROUPS, HEAD_DIM)
    return (y_p, y_s, ret_p.astype(state_ret.dtype), ret_s.astype(state_ret.dtype),
            rows_p.reshape(b, t, *kv_shape), rows_s.reshape(db, dq, *kv_shape),
            win_p.reshape(b, win_p.shape[1], *win_shape), win_s.reshape(db, win_s.shape[1], *win_shape))
```

```python
import functools

import jax
import jax.numpy as jnp
import numpy as np
from jax import lax
from jax.experimental import pallas as pl
from jax.experimental.pallas import tpu as pltpu

F32 = jnp.float32
BF16 = jnp.bfloat16
I32 = jnp.int32

D_MODEL = 1024
DEPTH = 4
N_A_LAYERS = DEPTH // 2
RET_HEADS = 4
RET_DK = D_MODEL // RET_HEADS
RET_DV = 2 * D_MODEL // RET_HEADS
RET_CHUNK = 128
NSA_HEADS = 16
NSA_GROUPS = 4
HEADS_PER_GROUP = NSA_HEADS // NSA_GROUPS
HEAD_DIM = D_MODEL // NSA_HEADS
CMP_BLOCK = 32
CMP_STRIDE = 16
SEL_BLOCK = 64
N_SELECT = 16
N_LOCAL = 2
WINDOW = 512
D_FF = 2816
ROPE_THETA = 10000.0
EPS = 1e-6
NEG = -1e30
BIG = 1e9
N_GATES = 3 * HEADS_PER_GROUP
GROUP_LANES = NSA_GROUPS * HEAD_DIM
PROMPT_NQ = 128
WIN_SPAN = WINDOW + PROMPT_NQ
MASK_ROWS = 64
MASK_BIAS = -(2.0 ** 100)
GATHER_PAGES = 4
LOG2E = 1.4426950408889634

VMEM_LIMIT_BYTES = 56 * 1024 * 1024
LANES = 128

NT_DIMS = (((1,), (1,)), ((), ()))
TN_DIMS = (((0,), (0,)), ((), ()))


def _params(*semantics):
    return pltpu.CompilerParams(dimension_semantics=semantics, vmem_limit_bytes=VMEM_LIMIT_BYTES)


def _rms(x, gain):
    ms = jnp.mean(x * x, axis=-1, keepdims=True)
    return x * lax.rsqrt(ms + EPS) * gain


def _seg_rms(y, ones_bd, gain):
    sq = y * y
    hi = sq.astype(BF16)
    lo = (sq - hi.astype(F32)).astype(BF16)
    ss = jnp.dot(hi, ones_bd, preferred_element_type=F32) + jnp.dot(lo, ones_bd, preferred_element_type=F32)
    return y * lax.rsqrt(ss * (1.0 / HEAD_DIM) + EPS) * gain


def _rope64(x, cos, sin_signed):
    lane = lax.broadcasted_iota(I32, x.shape, 1)
    first_half = (lane % HEAD_DIM) < (HEAD_DIM // 2)
    rot = jnp.where(first_half, pltpu.roll(x, LANES - HEAD_DIM // 2, 1), pltpu.roll(x, HEAD_DIM // 2, 1))
    return x * cos + rot * sin_signed


def _exp_rows(s, ok):
    s = jnp.where(ok, s, NEG)
    m = jnp.max(s, axis=-1, keepdims=True)
    e = jnp.exp2(s - m)
    return e, jnp.sum(e, axis=-1, keepdims=True)


def _ffn_kernel(x_ref, g_ref, wi_ref, wo_ref, o_ref):
    x = x_ref[...]
    xn = _rms(x, g_ref[...]).astype(BF16)
    a = jnp.dot(xn, wi_ref[:, 0:D_FF], preferred_element_type=F32)
    b = jnp.dot(xn, wi_ref[:, D_FF:2 * D_FF], preferred_element_type=F32)
    h = (a * jax.nn.sigmoid(a) * b).astype(BF16)
    o_ref[...] = x + 0.5 * jnp.dot(h, wo_ref[...], preferred_element_type=F32)


def ffn_half(x, gain, w_in, w_out, layer, half):
    n = x.shape[0]
    tm = min(n, 512)
    resident = pl.Buffered(1)
    return pl.pallas_call(
        _ffn_kernel,
        grid=(n // tm,),
        in_specs=[
            pl.BlockSpec((tm, D_MODEL), lambda i: (i, 0)),
            pl.BlockSpec((1, D_MODEL), lambda i: (0, 0)),
            pl.BlockSpec((None, None, D_MODEL, 2 * D_FF), lambda i: (layer, half, 0, 0), pipeline_mode=resident),
            pl.BlockSpec((None, None, D_FF, D_MODEL), lambda i: (layer, half, 0, 0), pipeline_mode=resident),
        ],
        out_specs=pl.BlockSpec((tm, D_MODEL), lambda i: (i, 0)),
        out_shape=jax.ShapeDtypeStruct((n, D_MODEL), F32),
        compiler_params=_params("parallel"),
        name="ffn_half",
    )(x, gain.reshape(1, D_MODEL), w_in, w_out)


def _proj_ffn_kernel(a_ref, wp_ref, x_ref, g_ref, wi_ref, wo_ref, o_ref):
    x = x_ref[...] + jnp.dot(a_ref[...], wp_ref[...], preferred_element_type=F32)
    xn = _rms(x, g_ref[...]).astype(BF16)
    a = jnp.dot(xn, wi_ref[:, 0:D_FF], preferred_element_type=F32)
    b = jnp.dot(xn, wi_ref[:, D_FF:2 * D_FF], preferred_element_type=F32)
    h = (a * jax.nn.sigmoid(a) * b).astype(BF16)
    o_ref[...] = x + 0.5 * jnp.dot(h, wo_ref[...], preferred_element_type=F32)


def proj_ffn_half(a, w_proj, proj_layer, x, gain, w_in, w_out, layer, half):
    n, k = a.shape
    tm = min(n, 512)
    resident = pl.Buffered(1)
    return pl.pallas_call(
        _proj_ffn_kernel,
        grid=(n // tm,),
        in_specs=[
            pl.BlockSpec((tm, k), lambda i: (i, 0)),
            pl.BlockSpec((None, k, D_MODEL), lambda i: (proj_layer, 0, 0), pipeline_mode=resident),
            pl.BlockSpec((tm, D_MODEL), lambda i: (i, 0)),
            pl.BlockSpec((1, D_MODEL), lambda i: (0, 0)),
            pl.BlockSpec((None, None, D_MODEL, 2 * D_FF), lambda i: (layer, half, 0, 0), pipeline_mode=resident),
            pl.BlockSpec((None, None, D_FF, D_MODEL), lambda i: (layer, half, 0, 0), pipeline_mode=resident),
        ],
        out_specs=pl.BlockSpec((tm, D_MODEL), lambda i: (i, 0)),
        out_shape=jax.ShapeDtypeStruct((n, D_MODEL), F32),
        compiler_params=_params("parallel"),
        name="proj_ffn_half",
    )(a, w_proj, x, gain.reshape(1, D_MODEL), w_in, w_out)


def _ret_inproj_kernel(x_ref, g_ref, w_ref, cos_ref, sin_ref, o_ref):
    xn = _rms(x_ref[...], g_ref[...]).astype(BF16)
    c = cos_ref[...]
    s = sin_ref[...]
    half = RET_DK // 2
    for h in range(2 * RET_HEADS):
        lo = h * RET_DK
        y = jnp.dot(xn, w_ref[:, lo:lo + RET_DK], preferred_element_type=F32)
        scale = 1.0 if h < RET_HEADS else RET_DK ** -0.5
        x1 = y[:, :half]
        x2 = y[:, half:]
        o_ref[:, lo:lo + half] = ((x1 * c - x2 * s) * scale).astype(BF16)
        o_ref[:, lo + half:lo + RET_DK] = ((x1 * s + x2 * c) * scale).astype(BF16)
    for h in range(2 * RET_HEADS):
        lo = 2 * D_MODEL + h * RET_DV
        o_ref[:, lo:lo + RET_DV] = jnp.dot(xn, w_ref[:, lo:lo + RET_DV], preferred_element_type=F32).astype(BF16)


def ret_inproj(x, gain, w, layer, cos, sin):
    n = x.shape[0]
    p = cos.shape[0]
    tm = min(n, 512, p)
    n_out = w.shape[2]
    tab_blocks = p // tm
    return pl.pallas_call(
        _ret_inproj_kernel,
        grid=(n // tm,),
        in_specs=[
            pl.BlockSpec((tm, D_MODEL), lambda i: (i, 0)),
            pl.BlockSpec((1, D_MODEL), lambda i: (0, 0)),
            pl.BlockSpec((None, D_MODEL, n_out), lambda i: (layer, 0, 0), pipeline_mode=pl.Buffered(1)),
            pl.BlockSpec((tm, RET_DK // 2), lambda i: (i % tab_blocks, 0)),
            pl.BlockSpec((tm, RET_DK // 2), lambda i: (i % tab_blocks, 0)),
        ],
        out_specs=pl.BlockSpec((tm, n_out), lambda i: (i, 0)),
        out_shape=jax.ShapeDtypeStruct((n, n_out), BF16),
        compiler_params=_params("parallel"),
        name="ret_inproj",
    )(x, gain.reshape(1, D_MODEL), w, cos, sin)


def _ret_core_kernel(*refs, chunk, n_inner, hps, has_s0):
    if has_s0:
        (q_ref, k_ref, v_ref, g_ref, dm_ref, qd_ref, kd_ref, sd_ref, s0_ref, o_ref, so_ref, s_scr) = refs
    else:
        (q_ref, k_ref, v_ref, g_ref, dm_ref, qd_ref, kd_ref, sd_ref, o_ref, so_ref, s_scr) = refs
    t = pl.program_id(2)

    @pl.when(t == 0)
    def _():
        if has_s0:
            s_scr[...] = s0_ref[0]
        else:
            s_scr[...] = jnp.zeros_like(s_scr)

    for c in range(n_inner):
        rows = slice(c * chunk, (c + 1) * chunk)
        for h in range(hps):
            qk_cols = slice(h * RET_DK, (h + 1) * RET_DK)
            v_cols = slice(h * RET_DV, (h + 1) * RET_DV)
            q = q_ref[0, rows, qk_cols]
            k = k_ref[0, rows, qk_cols]
            v = v_ref[0, rows, v_cols]
            g = g_ref[0, rows, v_cols].astype(F32)
            s = s_scr[h]
            scores = lax.dot_general(q, k, NT_DIMS, preferred_element_type=F32) * dm_ref[h]
            intra = jnp.dot(scores.astype(BF16), v, preferred_element_type=F32)
            cross = jnp.dot((q.astype(F32) * qd_ref[h]).astype(BF16), s.astype(BF16), preferred_element_type=F32)
            o = intra + cross
            kv = lax.dot_general((k.astype(F32) * kd_ref[h]).astype(BF16), v, TN_DIMS, preferred_element_type=F32)
            s_scr[h] = sd_ref[h, 0:1, 0:1] * s + kv
            mu = jnp.mean(o, axis=-1, keepdims=True)
            d = o - mu
            var = jnp.mean(d * d, axis=-1, keepdims=True)
            on = d * lax.rsqrt(var + EPS)
            o_ref[0, rows, v_cols] = (g * jax.nn.sigmoid(g) * on).astype(BF16)

    @pl.when(t == pl.num_programs(2) - 1)
    def _():
        so_ref[0] = s_scr[...]


def _decay_tables(chunk):
    lg = jnp.log(1.0 - 2.0 ** (-5.0 - jnp.arange(RET_HEADS, dtype=F32)))
    idx = jnp.arange(chunk, dtype=F32)
    rel = idx[:, None] - idx[None, :]
    dmat = jnp.where(rel >= 0, jnp.exp(jnp.maximum(rel, 0.0)[None] * lg[:, None, None]), 0.0)
    qdec = jnp.exp((idx + 1.0)[None, :] * lg[:, None])
    kdec = jnp.exp((chunk - 1.0 - idx)[None, :] * lg[:, None])
    sdec = jnp.exp(chunk * lg)
    qdec = jnp.broadcast_to(qdec[:, :, None], (RET_HEADS, chunk, RET_DK))
    kdec = jnp.broadcast_to(kdec[:, :, None], (RET_HEADS, chunk, RET_DK))
    sdec = jnp.broadcast_to(sdec[:, None, None], (RET_HEADS, 8, LANES))
    return dmat, qdec, kdec, sdec


def ret_core(qkvg, s0_all, layer):
    b, t, _ = qkvg.shape
    chunk = RET_CHUNK if t % RET_CHUNK == 0 else t
    tb = min(t, 4 * chunk)
    n_inner = tb // chunk
    hps = RET_HEADS if t < RET_CHUNK else 1
    nh = RET_HEADS // hps
    dmat, qdec, kdec, sdec = _decay_tables(chunk)
    in_specs = [
        pl.BlockSpec((1, tb, hps * RET_DK), lambda bi, h, ti: (bi, ti, h)),
        pl.BlockSpec((1, tb, hps * RET_DK), lambda bi, h, ti: (bi, ti, nh + h)),
        pl.BlockSpec((1, tb, hps * RET_DV), lambda bi, h, ti: (bi, ti, nh + h)),
        pl.BlockSpec((1, tb, hps * RET_DV), lambda bi, h, ti: (bi, ti, 2 * nh + h)),
        pl.BlockSpec((hps, chunk, chunk), lambda bi, h, ti: (h, 0, 0)),
        pl.BlockSpec((hps, chunk, RET_DK), lambda bi, h, ti: (h, 0, 0)),
        pl.BlockSpec((hps, chunk, RET_DK), lambda bi, h, ti: (h, 0, 0)),
        pl.BlockSpec((hps, 8, LANES), lambda bi, h, ti: (h, 0, 0)),
    ]
    args = [qkvg, qkvg, qkvg, qkvg, dmat, qdec, kdec, sdec]
    if s0_all is not None:
        in_specs.append(pl.BlockSpec((None, 1, hps, RET_DK, RET_DV), lambda bi, h, ti: (layer, bi, h, 0, 0)))
        args.append(s0_all)
    return pl.pallas_call(
        functools.partial(_ret_core_kernel, chunk=chunk, n_inner=n_inner, hps=hps, has_s0=s0_all is not None),
        grid=(b, nh, t // tb),
        in_specs=in_specs,
        out_specs=[
            pl.BlockSpec((1, tb, hps * RET_DV), lambda bi, h, ti: (bi, ti, h)),
            pl.BlockSpec((1, hps, RET_DK, RET_DV), lambda bi, h, ti: (bi, h, 0, 0)),
        ],
        out_shape=[
            jax.ShapeDtypeStruct((b, t, 2 * D_MODEL), BF16),
            jax.ShapeDtypeStruct((b, RET_HEADS, RET_DK, RET_DV), F32),
        ],
        scratch_shapes=[pltpu.VMEM((hps, RET_DK, RET_DV), F32)],
        compiler_params=_params("parallel", "parallel", "arbitrary"),
        name="ret_core",
    )(*args)


def _store_chunk_major(cmp_ref, scr, n_rows):
    n_planes = scr.shape[0]
    for p in range(CMP_STRIDE):
        for c in range(n_planes):
            lo = p * n_planes * LANES + c * LANES
            cmp_ref[:, lo:lo + LANES] = scr[c, pl.ds(p, n_rows // CMP_STRIDE, stride=CMP_STRIDE), :].astype(BF16)


def _kv_rows_kernel(x_ref, g_ref, w_ref, kn_ref, cos_ref, sin_ref, ones_ref, rows_ref, win_ref, *aux_refs):
    xn = _rms(x_ref[0], g_ref[...]).astype(BF16)
    y = jnp.dot(xn, w_ref[...], preferred_element_type=F32)
    tm = y.shape[0]
    cos = cos_ref[...]
    sin = sin_ref[...]
    ones_bd = ones_ref[...]
    slot = lambda s: y[:, s * GROUP_LANES:(s + 1) * GROUP_LANES]

    def norm_rope(v, gain):
        vn = _seg_rms(v, ones_bd, gain)
        return jnp.concatenate([_rope64(vn[:, :LANES], cos, sin), _rope64(vn[:, LANES:], cos, sin)], axis=1)

    k_slc = norm_rope(slot(2), kn_ref[1:2, :])
    k_win = norm_rope(slot(4), kn_ref[2:3, :])
    rows_ref[0, :, 0:2 * GROUP_LANES] = y[:, 0:2 * GROUP_LANES]
    rows_ref[0, :, 2 * GROUP_LANES:3 * GROUP_LANES] = k_slc
    rows_ref[0, :, 3 * GROUP_LANES:4 * GROUP_LANES] = slot(3)
    win_ref[0, :, 0:GROUP_LANES] = k_win
    win_ref[0, :, GROUP_LANES:2 * GROUP_LANES] = slot(5)
    if aux_refs:
        cmp_ref, ksel_ref, vsel_ref, kwin_ref, vwin_ref, scr = aux_refs
        for c in range(scr.shape[0]):
            scr[c] = y[:, c * LANES:(c + 1) * LANES]
        _store_chunk_major(cmp_ref, scr, tm)
        for ref, val in ((ksel_ref, k_slc), (vsel_ref, slot(3)), (kwin_ref, k_win), (vwin_ref, slot(5))):
            vt = jnp.transpose(val)
            for g in range(NSA_GROUPS):
                ref[0, g, 0:HEAD_DIM, :] = vt[g * HEAD_DIM:(g + 1) * HEAD_DIM, :].astype(BF16)
        key_blk = (pl.program_id(1) * tm + lax.broadcasted_iota(I32, (MASK_ROWS, tm), 1)) // SEL_BLOCK
        mask_rows = jnp.where(key_blk == lax.broadcasted_iota(I32, (MASK_ROWS, tm), 0), MASK_BIAS, 0.0).astype(BF16)
        for g in range(NSA_GROUPS):
            ksel_ref[0, g, HEAD_DIM:HEAD_DIM + MASK_ROWS, :] = mask_rows


def kv_rows(x, gain, w, k_norm_tiled, cos, sin, ones_bd, aux):
    b, t, _ = x.shape
    tm = min(t, 512)
    n_kv = w.shape[1]
    nt = t // tm
    out_specs = [
        pl.BlockSpec((1, tm, 4 * GROUP_LANES), lambda bi, ti: (bi, ti, 0)),
        pl.BlockSpec((1, tm, 2 * GROUP_LANES), lambda bi, ti: (bi, ti, 0)),
    ]
    out_shape = [
        jax.ShapeDtypeStruct((b, t, 4 * GROUP_LANES), F32),
        jax.ShapeDtypeStruct((b, t, 2 * GROUP_LANES), F32),
    ]
    scratch = []
    if aux:
        chunk_lanes = CMP_STRIDE * 2 * GROUP_LANES
        out_specs.append(pl.BlockSpec((tm // CMP_STRIDE, chunk_lanes), lambda bi, ti: (bi * nt + ti, 0)))
        out_shape.append(jax.ShapeDtypeStruct((b * t // CMP_STRIDE, chunk_lanes), BF16))
        for rows in (HEAD_DIM + MASK_ROWS, HEAD_DIM, HEAD_DIM, HEAD_DIM):
            out_specs.append(pl.BlockSpec((1, NSA_GROUPS, rows, tm), lambda bi, ti: (bi, 0, 0, ti)))
            out_shape.append(jax.ShapeDtypeStruct((b, NSA_GROUPS, rows, t), BF16))
        scratch.append(pltpu.VMEM((2 * GROUP_LANES // LANES, tm, LANES), F32))
    return pl.pallas_call(
        _kv_rows_kernel,
        grid=(b, nt),
        in_specs=[
            pl.BlockSpec((1, tm, D_MODEL), lambda bi, ti: (bi, ti, 0)),
            pl.BlockSpec((1, D_MODEL), lambda bi, ti: (0, 0)),
            pl.BlockSpec((D_MODEL, n_kv), lambda bi, ti: (0, 0)),
            pl.BlockSpec((3, GROUP_LANES), lambda bi, ti: (0, 0)),
            pl.BlockSpec((tm, LANES), lambda bi, ti: (ti, 0)),
            pl.BlockSpec((tm, LANES), lambda bi, ti: (ti, 0)),
            pl.BlockSpec((GROUP_LANES, GROUP_LANES), lambda bi, ti: (0, 0)),
        ],
        out_specs=out_specs,
        out_shape=out_shape,
        scratch_shapes=scratch,
        compiler_params=_params("parallel", "parallel"),
        name="kv_rows",
    )(x, gain.reshape(1, D_MODEL), w, k_norm_tiled, cos, sin, ones_bd)


def _cmp_partial_kernel(x_ref, wk_ref, wv_ref, o_ref):
    acc_k = jnp.zeros((x_ref.shape[0], 2 * GROUP_LANES), F32)
    acc_v = jnp.zeros((x_ref.shape[0], 2 * GROUP_LANES), F32)
    for p in range(CMP_STRIDE):
        lo = p * 2 * GROUP_LANES
        acc_k += jnp.dot(x_ref[:, lo:lo + GROUP_LANES], wk_ref[p], preferred_element_type=F32)
        acc_v += jnp.dot(x_ref[:, lo + GROUP_LANES:lo + 2 * GROUP_LANES], wv_ref[p], preferred_element_type=F32)
    o_ref[:, 0:2 * GROUP_LANES] = acc_k
    o_ref[:, 2 * GROUP_LANES:4 * GROUP_LANES] = acc_v


def cmp_partial(tok_chunks, wk_bd, wv_bd):
    n, width = tok_chunks.shape
    tm = min(n, 512)
    return pl.pallas_call(
        _cmp_partial_kernel,
        grid=(n // tm,),
        in_specs=[
            pl.BlockSpec((tm, width), lambda i: (i, 0)),
            pl.BlockSpec(wk_bd.shape, lambda i: (0, 0, 0)),
            pl.BlockSpec(wv_bd.shape, lambda i: (0, 0, 0)),
        ],
        out_specs=pl.BlockSpec((tm, 4 * GROUP_LANES), lambda i: (i, 0)),
        out_shape=jax.ShapeDtypeStruct((n, 4 * GROUP_LANES), F32),
        compiler_params=_params("parallel"),
        name="cmp_partial",
    )(tok_chunks, wk_bd, wv_bd)


def _cmp_combine_kernel(a_ref, pe_ref, w1_ref, b1_ref, w2_ref, kn_ref, cos_ref, sin_ref, ones_ref, kc_ref, vc_ref):
    a = a_ref[0]
    n = a.shape[0]
    for t, out_ref in enumerate((kc_ref, vc_ref)):
        first = a[:, 2 * t * GROUP_LANES:(2 * t + 1) * GROUP_LANES]
        second = a[:, (2 * t + 1) * GROUP_LANES:(2 * t + 2) * GROUP_LANES]
        second = pltpu.roll(second, n - 1, 0)
        pe_term = jnp.dot(pe_ref[t], w1_ref[t], preferred_element_type=F32)[0:1]
        h = b1_ref[t] + pe_term + first + second
        y = jnp.dot(jax.nn.gelu(h).astype(BF16), w2_ref[t], preferred_element_type=F32)
        if t == 0:
            y = _seg_rms(y, ones_ref[...], kn_ref[0:1, :])
            y = jnp.concatenate(
                [_rope64(y[:, :LANES], cos_ref[...], sin_ref[...]), _rope64(y[:, LANES:], cos_ref[...], sin_ref[...])],
                axis=1)
        yt = jnp.transpose(y)
        for g in range(NSA_GROUPS):
            out_ref[0, g] = yt[g * HEAD_DIM:(g + 1) * HEAD_DIM, :].astype(BF16)


def cmp_combine(partial, pe_rows, w1_tiled, b1_tiled, w2_bd, k_norm_tiled, cos, sin, ones_bd):
    b, ncp, _ = partial.shape
    full = lambda *shape: pl.BlockSpec(shape, lambda bi: (0,) * len(shape))
    return pl.pallas_call(
        _cmp_combine_kernel,
        grid=(b,),
        in_specs=[
            pl.BlockSpec((1, ncp, 4 * GROUP_LANES), lambda bi: (bi, 0, 0)),
            full(*pe_rows.shape), full(*w1_tiled.shape), full(*b1_tiled.shape), full(*w2_bd.shape),
            full(3, GROUP_LANES), full(ncp, LANES), full(ncp, LANES), full(GROUP_LANES, GROUP_LANES),
        ],
        out_specs=[pl.BlockSpec((1, NSA_GROUPS, HEAD_DIM, ncp), lambda bi: (bi, 0, 0, 0))] * 2,
        out_shape=[jax.ShapeDtypeStruct((b, NSA_GROUPS, HEAD_DIM, ncp), BF16)] * 2,
        compiler_params=_params("parallel"),
        name="cmp_combine",
    )(partial, pe_rows, w1_tiled, b1_tiled, w2_bd, k_norm_tiled, cos, sin, ones_bd)


def _nsa_q_kernel(x_ref, g_ref, wq_ref, wg_ref, qn_ref, cos_ref, sin_ref, ones_ref, q_ref, gate_ref):
    xn = _rms(x_ref[0], g_ref[...]).astype(BF16)
    y = jnp.dot(xn, wq_ref[...], preferred_element_type=F32)
    cos = cos_ref[...]
    sin = sin_ref[...]
    scale = HEAD_DIM ** -0.5 * LOG2E
    for g in range(NSA_GROUPS):
        yg = _seg_rms(y[:, g * GROUP_LANES:(g + 1) * GROUP_LANES], ones_ref[...], qn_ref[...])
        low = lax.broadcasted_iota(I32, (y.shape[0], LANES), 1) < HEAD_DIM
        for half in range(2):
            r = _rope64(yg[:, half * LANES:(half + 1) * LANES], cos, sin) * scale
            q_ref[0, g, 2 * half] = jnp.where(low, r, 0.0)
            q_ref[0, g, 2 * half + 1] = jnp.where(low, pltpu.roll(r, HEAD_DIM, 1), 0.0)
    gates = jnp.dot(xn, wg_ref[...], preferred_element_type=F32)
    gate_ref[0] = jax.nn.sigmoid(gates)


def nsa_q(x, gain, wq, wg, layer, q_norm_tiled, cos, sin, ones_bd):
    b, t, _ = x.shape
    tm = min(t, 512)
    return pl.pallas_call(
        _nsa_q_kernel,
        grid=(b, t // tm),
        in_specs=[
            pl.BlockSpec((1, tm, D_MODEL), lambda bi, ti: (bi, ti, 0)),
            pl.BlockSpec((1, D_MODEL), lambda bi, ti: (0, 0)),
            pl.BlockSpec((None,) + wq.shape[1:], lambda bi, ti: (layer, 0, 0)),
            pl.BlockSpec((None,) + wg.shape[1:], lambda bi, ti: (layer, 0, 0)),
            pl.BlockSpec((1, GROUP_LANES), lambda bi, ti: (0, 0)),
            pl.BlockSpec((tm, LANES), lambda bi, ti: (ti, 0)),
            pl.BlockSpec((tm, LANES), lambda bi, ti: (ti, 0)),
            pl.BlockSpec((GROUP_LANES, GROUP_LANES), lambda bi, ti: (0, 0)),
        ],
        out_specs=[
            pl.BlockSpec((1, NSA_GROUPS, HEADS_PER_GROUP, tm, LANES), lambda bi, ti: (bi, 0, 0, ti, 0)),
            pl.BlockSpec((1, tm, NSA_GROUPS * LANES), lambda bi, ti: (bi, ti, 0)),
        ],
        out_shape=[
            jax.ShapeDtypeStruct((b, NSA_GROUPS, HEADS_PER_GROUP, t, LANES), F32),
            jax.ShapeDtypeStruct((b, t, NSA_GROUPS * LANES), F32),
        ],
        compiler_params=_params("parallel", "parallel"),
        name="nsa_q",
    )(x, gain.reshape(1, D_MODEL), wq, wg, q_norm_tiled, cos, sin, ones_bd)


def _compressed_branch(q, qpos, q_first, kc_t, vc_t, ovt, nq, nc, ns):
    hpg = HEADS_PER_GROUP
    ncp = kc_t.shape[1]
    nsp = ovt.shape[0]
    s_c = jnp.dot(q, kc_t, preferred_element_type=F32)
    cidx = lax.broadcasted_iota(I32, (1, ncp), 1)
    c_end = jnp.where(cidx < nc, cidx * CMP_STRIDE + (CMP_BLOCK - 1), 2 ** 30)
    valid_c = c_end <= qpos
    e_c, l_c = _exp_rows(s_c, valid_c)
    p_c = jnp.where(valid_c, e_c * (1.0 / l_c), 0.0)
    o_c = lax.dot_general(p_c.astype(BF16), vc_t, NT_DIMS, preferred_element_type=F32)
    p_sum = p_c[0:nq]
    for hh in range(1, hpg):
        p_sum = p_sum + p_c[hh * nq:(hh + 1) * nq]
    p_hi = p_sum.astype(BF16)
    p_lo = (p_sum - p_hi.astype(F32)).astype(BF16)
    imp_t = (lax.dot_general(ovt, p_hi, NT_DIMS, preferred_element_type=F32)
             + lax.dot_general(ovt, p_lo, NT_DIMS, preferred_element_type=F32))
    sidx = lax.broadcasted_iota(I32, (nsp, 1), 0)
    q_blk = (q_first + lax.broadcasted_iota(I32, (1, nq), 1)) // SEL_BLOCK
    valid_s = (sidx <= q_blk) & (sidx < ns)
    forced = (sidx == 0) | (valid_s & (q_blk - sidx < N_LOCAL))
    score_t = jnp.where(forced, BIG, jnp.where(valid_s, imp_t, NEG))
    return o_c, score_t, valid_s


def _window_branch(s_w, qpos, vw_t, kpos0):
    kpos_w = kpos0 + lax.broadcasted_iota(I32, (1, s_w.shape[1]), 1)
    kpos_w = jnp.where(kpos_w >= 0, kpos_w, -(2 ** 30))
    behind = lax.bitcast_convert_type(qpos - kpos_w, jnp.uint32)
    e_w, l_w = _exp_rows(s_w, behind < jnp.uint32(WINDOW))
    return lax.dot_general(e_w.astype(BF16), vw_t, NT_DIMS, preferred_element_type=F32), l_w


def _gate_and_store(o_ref, lane0, gates, o_c, acc_s, l_s, acc_w, l_w, nq):
    inv_s = 1.0 / l_s
    inv_w = 1.0 / l_w
    for hh in range(HEADS_PER_GROUP):
        rows = slice(hh * nq, (hh + 1) * nq)
        o_h = (gates[:, 3 * hh:3 * hh + 1] * o_c[rows] + (gates[:, 3 * hh + 1:3 * hh + 2] * inv_s[rows]) * acc_s[rows]
               + (gates[:, 3 * hh + 2:3 * hh + 3] * inv_w[rows]) * acc_w[rows])
        o_ref[0, :, lane0 + hh * HEAD_DIM:lane0 + (hh + 1) * HEAD_DIM] = o_h.astype(BF16)


def _nsa_prompt_kernel(q_ref, gate_ref, kc_ref, vc_ref, ovt_ref, ksel_ref, vsel_ref, kwin_ref, vwin_ref,
                       o_ref, score_scr, s_a, s_b, s_w, *, nq, nc, ns, kc_keys):
    i = pl.program_id(2)
    hpg = HEADS_PER_GROUP
    r = hpg * nq
    nsp = ovt_ref.shape[0]
    t_len = kwin_ref.shape[3]
    q_first = i * nq
    q_wide = q_ref[0, 0].reshape(r, LANES)
    q = q_wide[:, 0:HEAD_DIM].astype(BF16)
    qpos = q_first + lax.broadcasted_iota(I32, (r, 1), 0) % nq
    w0 = pl.multiple_of(jnp.clip(q_first + nq - WIN_SPAN, 0, t_len - WIN_SPAN), LANES)
    s_w[...] = jnp.dot(q, kwin_ref[0, 0, :, pl.ds(w0, WIN_SPAN)], preferred_element_type=F32)
    o_c, score_t, valid_s = _compressed_branch(q, qpos, q_first, kc_ref[0, 0], vc_ref[0, 0], ovt_ref[...], nq, nc, ns)

    score_scr[...] = score_t
    blk = lax.broadcasted_iota(I32, (nsp, nq), 0)
    n_live = jnp.minimum((q_first + nq - 1) // SEL_BLOCK + 1, ns)

    def rank_body(sp, rank):
        row = score_scr[pl.ds(sp, 1), :]
        tie = jnp.where(row == score_t, jnp.where(blk > sp, 1.0, 0.0), 0.0)
        return rank + jnp.where(row > score_t, 1.0, tie)

    rank = lax.fori_loop(0, n_live, rank_body, jnp.zeros((nsp, nq), F32))
    dropped_t = 1.0 - ((rank < float(min(N_SELECT, ns))) & valid_s).astype(F32)
    dropped = jnp.transpose(jnp.concatenate([jnp.zeros((HEAD_DIM, nq), F32), dropped_t], axis=0))
    qa = (q_wide + jnp.concatenate([dropped] * hpg, axis=0)).astype(BF16)

    def scores_into(dst, c):
        start = pl.multiple_of(c * kc_keys, kc_keys)
        dst[...] = jnp.dot(qa, ksel_ref[0, 0, :, pl.ds(start, kc_keys)], preferred_element_type=F32)

    def update(src, c, carry, causal):
        m, l, acc = carry
        start = pl.multiple_of(c * kc_keys, kc_keys)
        s = src[...]
        if causal:
            kpos = c * kc_keys + lax.broadcasted_iota(I32, (1, kc_keys), 1)
            s = jnp.where(kpos <= qpos, s, NEG)
        m_new = jnp.maximum(m, jnp.max(s, axis=-1, keepdims=True))
        alpha = jnp.exp2(m - m_new)
        p = jnp.exp2(s - m_new)
        l = alpha * l + jnp.sum(p, axis=-1, keepdims=True)
        pv = lax.dot_general(p.astype(BF16), vsel_ref[0, 0, :, pl.ds(start, kc_keys)], NT_DIMS,
                             preferred_element_type=F32)
        return m_new, l, alpha * acc + pv

    n_past = q_first // kc_keys
    scores_into(s_a, 0)

    def pair_body(pi, carry):
        c = 2 * pi
        scores_into(s_b, c + 1)
        carry = update(s_a, c, carry, causal=False)
        scores_into(s_a, c + 2)
        return update(s_b, c + 1, carry, causal=False)

    init = (jnp.full((r, 1), NEG, F32), jnp.zeros((r, 1), F32), jnp.zeros((r, HEAD_DIM), F32))
    carry = lax.fori_loop(0, n_past // 2, pair_body, init)

    def odd_tail(carry):
        scores_into(s_b, n_past)
        carry = update(s_a, n_past - 1, carry, causal=False)
        return update(s_b, n_past, carry, causal=True)

    def even_tail(carry):
        return update(s_a, n_past, carry, causal=True)

    _, l_s, acc_s = lax.cond(n_past % 2 == 1, odd_tail, even_tail, carry)

    acc_w, l_w = _window_branch(s_w[...], qpos, vwin_ref[0, 0, :, pl.ds(w0, WIN_SPAN)], w0)
    _gate_and_store(o_ref, 0, gate_ref[0], o_c, acc_s, l_s, acc_w, l_w, nq)


def nsa_attend_prompt(q, gates, kc_t, vc_t, ovt, ksel_t, vsel_t, kwin_t, vwin_t, *, nc, ns):
    b, _, _, t, _ = q.shape
    assert ovt.shape[0] == MASK_ROWS == LANES - HEAD_DIM
    nq = PROMPT_NQ
    kc_keys = min(512, t)
    ncp = kc_t.shape[3]
    nsp = ovt.shape[0]
    per_bg = lambda rows, cols: pl.BlockSpec((1, 1, rows, cols), lambda bi, g, ti: (bi, g, 0, 0))
    kernel = functools.partial(_nsa_prompt_kernel, nq=nq, nc=nc, ns=ns, kc_keys=kc_keys)
    return pl.pallas_call(
        kernel,
        grid=(b, NSA_GROUPS, t // nq),
        in_specs=[
            pl.BlockSpec((1, 1, HEADS_PER_GROUP, nq, LANES), lambda bi, g, ti: (bi, g, 0, ti, 0)),
            pl.BlockSpec((1, nq, LANES), lambda bi, g, ti: (bi, ti, g)),
            per_bg(HEAD_DIM, ncp), per_bg(HEAD_DIM, ncp),
            pl.BlockSpec((nsp, ncp), lambda bi, g, ti: (0, 0)),
            per_bg(HEAD_DIM + MASK_ROWS, t), per_bg(HEAD_DIM, t), per_bg(HEAD_DIM, t), per_bg(HEAD_DIM, t),
        ],
        out_specs=pl.BlockSpec((1, nq, GROUP_LANES), lambda bi, g, ti: (bi, ti, g)),
        out_shape=jax.ShapeDtypeStruct((b, t, NSA_HEADS * HEAD_DIM), BF16),
        scratch_shapes=[pltpu.VMEM((nsp, nq), F32), pltpu.VMEM((HEADS_PER_GROUP * nq, kc_keys), F32),
                        pltpu.VMEM((HEADS_PER_GROUP * nq, kc_keys), F32),
                        pltpu.VMEM((HEADS_PER_GROUP * nq, WIN_SPAN), F32)],
        compiler_params=_params("parallel", "parallel", "arbitrary"),
        name="nsa_attend_prompt",
    )(q, gates, kc_t, vc_t, ovt, ksel_t, vsel_t, kwin_t, vwin_t)


def _nsa_decode_kernel(q_ref, gate_ref, kc_ref, vc_ref, ovt_ref, ind_ref, ksel_ref, vsel_ref, kwin_ref, vwin_ref,
                       o_ref, *, nq, nc, ns, q_pos0, win_pos0):
    hpg = HEADS_PER_GROUP
    r = hpg * nq
    nsp = ovt_ref.shape[0]
    n_slabs, slab = ksel_ref.shape[2], ksel_ref.shape[4]
    l_keys = n_slabs * slab
    qpos = q_pos0 + lax.broadcasted_iota(I32, (r, 1), 0) % nq
    s_other = lax.broadcasted_iota(I32, (nsp, 1), 0)
    s_self = lax.broadcasted_iota(I32, (1, nsp), 1)
    qrow = lax.broadcasted_iota(I32, (nq, 1), 0)
    kpos = lax.broadcasted_iota(I32, (1, l_keys), 1)
    for g in range(NSA_GROUPS):
        q = q_ref[0, g].reshape(r, LANES)[:, 0:HEAD_DIM].astype(BF16)
        o_c, score_t, valid_s = _compressed_branch(q, qpos, q_pos0, kc_ref[0, g], vc_ref[0, g], ovt_ref[...],
                                                   nq, nc, ns)
        score = jnp.transpose(score_t)
        valid = jnp.transpose(valid_s.astype(F32))
        rank = jnp.zeros((nq, nsp), F32)
        for qi in range(nq):
            other = score_t[:, qi:qi + 1]
            own = score[qi:qi + 1, :]
            ahead = (other > own) | ((other == own) & (s_other < s_self))
            rank = jnp.where(qrow == qi, jnp.sum(ahead.astype(F32), axis=0, keepdims=True), rank)
        sel = ((rank < float(min(N_SELECT, ns))) & (valid > 0.5)).astype(F32)
        sel_rows = jnp.concatenate([sel] * hpg, axis=0).astype(BF16)

        s = jnp.concatenate([jnp.dot(q, ksel_ref[0, g, c], preferred_element_type=F32) for c in range(n_slabs)],
                            axis=1)
        picked = jnp.dot(sel_rows, ind_ref[...], preferred_element_type=F32)
        e_s, l_s = _exp_rows(s, (picked > 0.5) & (kpos <= qpos))
        e_s = e_s.astype(BF16)
        acc_s = jnp.zeros((r, HEAD_DIM), F32)
        for c in range(n_slabs):
            acc_s += lax.dot_general(e_s[:, c * slab:(c + 1) * slab], vsel_ref[0, g, c], NT_DIMS,
                                     preferred_element_type=F32)

        s_w = jnp.dot(q, kwin_ref[0, g], preferred_element_type=F32)
        acc_w, l_w = _window_branch(s_w, qpos, vwin_ref[0, g], win_pos0)
        _gate_and_store(o_ref, g * GROUP_LANES, gate_ref[0, :, g * LANES:(g + 1) * LANES], o_c, acc_s, l_s, acc_w, l_w, nq)


def nsa_attend_decode(q, gates, kc_t, vc_t, ovt, ind, ksel_t, vsel_t, kwin_t, vwin_t, *, nc, ns, q_pos0, win_pos0):
    b, _, _, nq, _ = q.shape
    ncp = kc_t.shape[3]
    nsp = ovt.shape[0]
    n_slabs, slab = ksel_t.shape[2], ksel_t.shape[4]
    l_keys = n_slabs * slab
    slabs = pl.BlockSpec((1, NSA_GROUPS, n_slabs, HEAD_DIM, slab), lambda bi: (bi, 0, 0, 0, 0))
    per_b = lambda cols: pl.BlockSpec((1, NSA_GROUPS, HEAD_DIM, cols), lambda bi: (bi, 0, 0, 0))
    kernel = functools.partial(_nsa_decode_kernel, nq=nq, nc=nc, ns=ns, q_pos0=q_pos0, win_pos0=win_pos0)
    return pl.pallas_call(
        kernel,
        grid=(b,),
        in_specs=[
            pl.BlockSpec((1, NSA_GROUPS, HEADS_PER_GROUP, nq, LANES), lambda bi: (bi, 0, 0, 0, 0)),
            pl.BlockSpec((1, nq, NSA_GROUPS * LANES), lambda bi: (bi, 0, 0)),
            per_b(ncp), per_b(ncp),
            pl.BlockSpec((nsp, ncp), lambda bi: (0, 0)),
            pl.BlockSpec((nsp, l_keys), lambda bi: (0, 0)),
            slabs, slabs, per_b(WIN_SPAN), per_b(WIN_SPAN),
        ],
        out_specs=pl.BlockSpec((1, nq, NSA_HEADS * HEAD_DIM), lambda bi: (bi, 0, 0)),
        out_shape=jax.ShapeDtypeStruct((b, nq, NSA_HEADS * HEAD_DIM), BF16),
        compiler_params=_params("parallel"),
        name="nsa_attend_decode",
    )(q, gates, kc_t, vc_t, ovt, ind, ksel_t, vsel_t, kwin_t, vwin_t)


def _gather_kernel(pt_ref, *refs, n_steps):
    page_refs = refs[:GATHER_PAGES]
    new_ref, perm_ref, cmp_ref, ksel_ref, vsel_ref = refs[GATHER_PAGES:]
    p = pl.program_id(1)
    page = page_refs[0].shape[2]
    k_lo = 2 * GROUP_LANES
    v_lo = 3 * GROUP_LANES

    @pl.when(p < n_steps - 1)
    def _():
        cmp_feats = []
        for k, page_ref in enumerate(page_refs):
            x = page_ref[0]
            cols = slice(k * page, (k + 1) * page)
            cmp_feats.append(x[0:2 * GROUP_LANES, :].astype(BF16))
            for g in range(NSA_GROUPS):
                ksel_ref[0, g, 0, :, cols] = x[k_lo + g * HEAD_DIM:k_lo + (g + 1) * HEAD_DIM, :].astype(BF16)
                vsel_ref[0, g, 0, :, cols] = x[v_lo + g * HEAD_DIM:v_lo + (g + 1) * HEAD_DIM, :].astype(BF16)
        feats = jnp.concatenate(cmp_feats, axis=1)
        toks = lax.dot_general(perm_ref[...], feats, NT_DIMS, preferred_element_type=F32).astype(BF16)
        n_chunks = toks.shape[0] // CMP_STRIDE
        for pos in range(CMP_STRIDE):
            cmp_ref[:, pos * 2 * GROUP_LANES:(pos + 1) * 2 * GROUP_LANES] = toks[pos * n_chunks:(pos + 1) * n_chunks, :]

    @pl.when(p == n_steps - 1)
    def _():
        new = new_ref[0]
        padded = jnp.concatenate([new, jnp.zeros((GATHER_PAGES * page - new.shape[0], new.shape[1]), F32)], axis=0)
        kt = jnp.transpose(padded[:, k_lo:k_lo + GROUP_LANES])
        vt = jnp.transpose(padded[:, v_lo:v_lo + GROUP_LANES])
        for g in range(NSA_GROUPS):
            ksel_ref[0, g, 0] = kt[g * HEAD_DIM:(g + 1) * HEAD_DIM, :].astype(BF16)
            vsel_ref[0, g, 0] = vt[g * HEAD_DIM:(g + 1) * HEAD_DIM, :].astype(BF16)


def gather_past(page_table, cache_t, new_rows):
    db, n_pages = page_table.shape
    page = cache_t.shape[2]
    dq = new_rows.shape[1]
    past = n_pages * page
    n_full = n_pages // GATHER_PAGES
    n_steps = n_full + 1
    step_keys = GATHER_PAGES * page
    chunk_lanes = CMP_STRIDE * 2 * GROUP_LANES
    n_chunks = step_keys // CMP_STRIDE
    out_row = np.arange(step_keys)
    perm = jnp.asarray((out_row % n_chunks * CMP_STRIDE + out_row // n_chunks)[:, None] == np.arange(step_keys)[None, :],
                       BF16)

    def page_spec(k):
        return pl.BlockSpec(
            (1, 4 * GROUP_LANES, page),
            lambda b, p, pt: (pt[b * n_pages + jnp.minimum(p, n_full - 1) * GATHER_PAGES + k], 0, 0))

    grid_spec = pltpu.PrefetchScalarGridSpec(
        num_scalar_prefetch=1,
        grid=(db, n_steps),
        in_specs=[page_spec(k) for k in range(GATHER_PAGES)] + [
            pl.BlockSpec((1, dq, 4 * GROUP_LANES), lambda b, p, pt: (b, 0, 0)),
            pl.BlockSpec((step_keys, step_keys), lambda b, p, pt: (0, 0))],
        out_specs=[
            pl.BlockSpec((step_keys // CMP_STRIDE, chunk_lanes),
                         lambda b, p, pt: (b * n_full + jnp.minimum(p, n_full - 1), 0)),
            pl.BlockSpec((1, NSA_GROUPS, 1, HEAD_DIM, step_keys), lambda b, p, pt: (b, 0, p, 0, 0)),
            pl.BlockSpec((1, NSA_GROUPS, 1, HEAD_DIM, step_keys), lambda b, p, pt: (b, 0, p, 0, 0)),
        ],
    )
    return pl.pallas_call(
        functools.partial(_gather_kernel, n_steps=n_steps),
        grid_spec=grid_spec,
        out_shape=[
            jax.ShapeDtypeStruct((db * past // CMP_STRIDE, chunk_lanes), BF16),
            jax.ShapeDtypeStruct((db, NSA_GROUPS, n_steps, HEAD_DIM, step_keys), BF16),
            jax.ShapeDtypeStruct((db, NSA_GROUPS, n_steps, HEAD_DIM, step_keys), BF16),
        ],
        compiler_params=_params("parallel", "arbitrary"),
        name="gather_past",
    )(page_table.reshape(-1), *([cache_t] * GATHER_PAGES), new_rows, perm)


def _win_assemble_kernel(cache_ref, new_ref, win_ref, kwin_ref, vwin_ref):
    old = cache_ref[0]
    new = new_ref[0]
    buf = old.shape[0]
    dq = new.shape[0]
    win_ref[0, 0:buf - dq, :] = old[dq:, :]
    win_ref[0, buf - dq:buf, :] = new
    old_t = jnp.transpose(old)
    tail = jnp.concatenate([new, jnp.zeros((WIN_SPAN - buf - dq, new.shape[1]), F32)], axis=0)
    tail_t = jnp.transpose(tail)
    for ref, off in ((kwin_ref, 0), (vwin_ref, GROUP_LANES)):
        for g in range(NSA_GROUPS):
            rows = slice(off + g * HEAD_DIM, off + (g + 1) * HEAD_DIM)
            ref[0, g] = jnp.concatenate([old_t[rows, :], tail_t[rows, :]], axis=1).astype(BF16)


def win_assemble(cache_win, new_win):
    db, buf, width = cache_win.shape
    dq = new_win.shape[1]
    return pl.pallas_call(
        _win_assemble_kernel,
        grid=(db,),
        in_specs=[
            pl.BlockSpec((1, buf, width), lambda b: (b, 0, 0)),
            pl.BlockSpec((1, dq, width), lambda b: (b, 0, 0)),
        ],
        out_specs=[
            pl.BlockSpec((1, buf, width), lambda b: (b, 0, 0)),
            pl.BlockSpec((1, NSA_GROUPS, HEAD_DIM, WIN_SPAN), lambda b: (b, 0, 0, 0)),
            pl.BlockSpec((1, NSA_GROUPS, HEAD_DIM, WIN_SPAN), lambda b: (b, 0, 0, 0)),
        ],
        out_shape=[
            jax.ShapeDtypeStruct((db, buf, width), F32),
            jax.ShapeDtypeStruct((db, NSA_GROUPS, HEAD_DIM, WIN_SPAN), BF16),
            jax.ShapeDtypeStruct((db, NSA_GROUPS, HEAD_DIM, WIN_SPAN), BF16),
        ],
        compiler_params=_params("parallel"),
        name="win_assemble",
    )(cache_win, new_win)


def _rope_angles(pos, half):
    inv = ROPE_THETA ** (-jnp.arange(half, dtype=F32) / half)
    ang = pos.astype(F32)[:, None] * inv[None, :]
    return jnp.cos(ang), jnp.sin(ang)


def _rope_tables_head64(pos):
    cos, sin = _rope_angles(pos, HEAD_DIM // 2)
    return jnp.concatenate([cos] * 4, axis=1), jnp.concatenate([-sin, sin] * 2, axis=1)


def _block_diag_groups(w):
    eye = jnp.eye(NSA_GROUPS, dtype=w.dtype)
    out = jnp.einsum("gh,...dn->...gdhn", eye, w)
    return out.reshape(*w.shape[:-2], GROUP_LANES, NSA_GROUPS * w.shape[-1])


def _compress_weights(cmp_w1, cmp_b1, cmp_w2, cmp_pe):
    r = CMP_BLOCK // CMP_STRIDE
    w1 = cmp_w1.reshape(2, r, CMP_STRIDE, HEAD_DIM, HEAD_DIM)
    bd = _block_diag_groups(w1)
    bd = jnp.concatenate([bd[:, 0], bd[:, 1]], axis=-1).astype(BF16)
    pe_rows = jnp.broadcast_to(cmp_pe.reshape(2, 1, CMP_BLOCK * HEAD_DIM), (2, 8, CMP_BLOCK * HEAD_DIM)).astype(BF16)
    w1_tiled = jnp.tile(cmp_w1, (1, 1, NSA_GROUPS)).astype(BF16)
    b1_tiled = jnp.tile(cmp_b1, (1, NSA_GROUPS)).reshape(2, 1, GROUP_LANES)
    w2_bd = _block_diag_groups(cmp_w2).astype(BF16)
    return bd[0], bd[1], pe_rows, w1_tiled, b1_tiled, w2_bd


def _overlap_table(nc, ncp, ns, nsp):
    ci = np.arange(ncp)[None, :]
    sj = np.arange(nsp)[:, None]
    overlap_t = ((ci * CMP_STRIDE < (sj + 1) * SEL_BLOCK) & (ci * CMP_STRIDE + CMP_BLOCK > sj * SEL_BLOCK)
                 & (ci < nc) & (sj < ns))
    return jnp.asarray(overlap_t, BF16)


def _block_membership(length, nsp):
    return jnp.asarray((np.arange(length)[None, :] // SEL_BLOCK) == np.arange(nsp)[:, None], BF16)


def _round_up(x, m):
    return -(-x // m) * m


def _trunk(x, pos, ret_s0, past, w):
    (ffn_norm, ffn_w_in, ffn_w_out, ret_norm, ret_w_in, ret_w_out, kv_norm, kv_w, k_norm_tiled,
     cmp_weights, nsa_norm, nsa_wq, nsa_wg, q_norm_tiled, nsa_w_out, ones_bd) = w
    b, t, _ = x.shape
    n = b * t
    prompt = past is None
    xf = x.reshape(n, D_MODEL)
    pos_rows = jnp.tile(pos, b) if not prompt else pos
    ret_cos, ret_sin = _rope_angles(pos_rows, RET_DK // 2)
    cos64, sin64 = _rope_tables_head64(pos)
    ret_states = []
    rows = win = attend = None
    for layer in range(DEPTH):
        if layer == N_A_LAYERS:
            xs = xf.reshape(b, t, D_MODEL)
            if prompt:
                rows, win, cmp_tok, ksel, vsel, kwin, vwin = kv_rows(
                    xs, kv_norm, kv_w, k_norm_tiled, cos64, sin64, ones_bd, aux=True)
                length = t
                new_win = win[:, t - min(WINDOW, t):]
            else:
                page_table, cache_t, cache_win = past
                rows, win = kv_rows(xs, kv_norm, kv_w, k_norm_tiled, cos64, sin64, ones_bd, aux=False)
                cmp_tok, ksel, vsel = gather_past(page_table, cache_t, rows)
                new_win, kwin, vwin = win_assemble(cache_win, win)
                past_len = page_table.shape[1] * cache_t.shape[2]
                length = past_len + t
            nc = (length - CMP_BLOCK) // CMP_STRIDE + 1
            n_chunk_rows = nc + CMP_BLOCK // CMP_STRIDE - 1
            ns = -(-length // SEL_BLOCK)
            wk_bd, wv_bd, pe_rows, w1_tiled, b1_tiled, w2_bd = cmp_weights
            assert cmp_tok.shape[0] == b * n_chunk_rows
            partial = cmp_partial(cmp_tok, wk_bd, wv_bd).reshape(b, n_chunk_rows, 4 * GROUP_LANES)
            c_end = jnp.arange(n_chunk_rows, dtype=I32) * CMP_STRIDE + (CMP_BLOCK - 1)
            cos_c, sin_c = _rope_tables_head64(c_end)
            kc_t, vc_t = cmp_combine(partial, pe_rows, w1_tiled, b1_tiled, w2_bd, k_norm_tiled, cos_c, sin_c, ones_bd)
            if prompt:
                assert ns <= MASK_ROWS and t % PROMPT_NQ == 0
                ovt = _overlap_table(nc, n_chunk_rows, ns, MASK_ROWS)
                attend = functools.partial(nsa_attend_prompt, kc_t=kc_t, vc_t=vc_t, ovt=ovt, ksel_t=ksel, vsel_t=vsel,
                                           kwin_t=kwin, vwin_t=vwin, nc=nc, ns=ns)
            else:
                nsp = _round_up(ns, 16)
                ovt = _overlap_table(nc, n_chunk_rows, ns, nsp)
                ind = _block_membership(ksel.shape[2] * ksel.shape[4], nsp)
                attend = functools.partial(nsa_attend_decode, kc_t=kc_t, vc_t=vc_t, ovt=ovt, ind=ind, ksel_t=ksel,
                                           vsel_t=vsel, kwin_t=kwin, vwin_t=vwin, nc=nc, ns=ns,
                                           q_pos0=int(past_len), win_pos0=int(past_len - cache_win.shape[1]))
        xf = ffn_half(xf, ffn_norm[layer, 0], ffn_w_in, ffn_w_out, layer, 0)
        if layer < N_A_LAYERS:
            qkvg = ret_inproj(xf, ret_norm[layer], ret_w_in, layer, ret_cos, ret_sin)
            gated, s_fin = ret_core(qkvg.reshape(b, t, 6 * D_MODEL), ret_s0, layer)
            ret_states.append(s_fin)
            mix, w_mix, mix_layer = gated.reshape(n, 2 * D_MODEL), ret_w_out, layer
        else:
            j = layer - N_A_LAYERS
            q, gates = nsa_q(xf.reshape(b, t, D_MODEL), nsa_norm[j], nsa_wq, nsa_wg, j, q_norm_tiled[j],
                             cos64, sin64, ones_bd)
            o = attend(q, gates)
            mix, w_mix, mix_layer = o.reshape(n, NSA_HEADS * HEAD_DIM), nsa_w_out, j
        xf = proj_ffn_half(mix, w_mix, mix_layer, xf, ffn_norm[layer, 1], ffn_w_in, ffn_w_out, layer, 1)
    return xf.reshape(b, t, D_MODEL), jnp.stack(ret_states), rows, new_win


def kernel(x_prompt, x_sample, state_ret, cache_kv, cache_win, page_table, ffn_norm, ffn_w_in, ffn_w_out, ret_norm,
           ret_w_in, ret_w_out, kv_norm, kv_w, k_norm, cmp_w1, cmp_b1, cmp_w2, cmp_pe, nsa_norm, nsa_w_in, q_norm,
           nsa_w_out):
    b, t, _ = x_prompt.shape
    db, dq, _ = x_sample.shape
    n_phys, page = cache_kv.shape[:2]
    past_len = page_table.shape[1] * page
    n_q_cols = NSA_HEADS * HEAD_DIM

    gate_w = nsa_w_in[:, :, n_q_cols:].reshape(-1, D_MODEL, NSA_GROUPS, N_GATES)
    gate_w = jnp.pad(gate_w, ((0, 0), (0, 0), (0, 0), (0, LANES - N_GATES))).reshape(-1, D_MODEL, NSA_GROUPS * LANES)
    eye = np.arange(GROUP_LANES)
    ones_bd = jnp.asarray((eye[:, None] // HEAD_DIM) == (eye[None, :] // HEAD_DIM), BF16)
    w = (ffn_norm, ffn_w_in.astype(BF16), ffn_w_out.astype(BF16), ret_norm, ret_w_in.astype(BF16),
         ret_w_out.astype(BF16), kv_norm, kv_w.astype(BF16), jnp.tile(k_norm, (1, NSA_GROUPS)),
         _compress_weights(cmp_w1, cmp_b1, cmp_w2, cmp_pe), nsa_norm, nsa_w_in[:, :, :n_q_cols].astype(BF16),
         gate_w.astype(BF16), jnp.tile(q_norm, (1, NSA_GROUPS)).reshape(-1, 1, GROUP_LANES),
         nsa_w_out.astype(BF16), ones_bd)

    pos_p = jnp.arange(t, dtype=I32)
    pos_s = past_len + jnp.arange(dq, dtype=I32)
    y_p, ret_p, rows_p, win_p = _trunk(x_prompt, pos_p, None, None, w)
    cache_t = jnp.transpose(cache_kv, (0, 2, 3, 4, 1)).reshape(n_phys, 4 * GROUP_LANES, page)
    cwin = cache_win.reshape(db, cache_win.shape[1], 2 * GROUP_LANES)
    y_s, ret_s, rows_s, win_s = _trunk(x_sample, pos_s, state_ret, (page_table, cache_t, cwin), w)
    kv_shape = (4, NSA_GROUPS, HEAD_DIM)
    win_shape = (2, NSA_GROUPS, HEAD_DIM)
    return (y_p, y_s, ret_p.astype(state_ret.dtype), ret_s.astype(state_ret.dtype),
            rows_p.reshape(b, t, *kv_shape), rows_s.reshape(db, dq, *kv_shape),
            win_p.reshape(b, win_p.shape[1], *win_shape), win_s.reshape(db, win_s.shape[1], *win_shape))
```

```python
import functools

import jax
import jax.numpy as jnp
import numpy as np
from jax import lax
from jax.experimental import pallas as pl
from jax.experimental.pallas import tpu as pltpu

F32 = jnp.float32
BF16 = jnp.bfloat16
I32 = jnp.int32

D_MODEL = 1024
DEPTH = 4
N_A_LAYERS = DEPTH // 2
RET_HEADS = 4
RET_DK = D_MODEL // RET_HEADS
RET_DV = 2 * D_MODEL // RET_HEADS
RET_CHUNK = 128
NSA_HEADS = 16
NSA_GROUPS = 4
HEADS_PER_GROUP = NSA_HEADS // NSA_GROUPS
HEAD_DIM = D_MODEL // NSA_HEADS
CMP_BLOCK = 32
CMP_STRIDE = 16
SEL_BLOCK = 64
N_SELECT = 16
N_LOCAL = 2
WINDOW = 512
D_FF = 2816
ROPE_THETA = 10000.0
EPS = 1e-6
NEG = -1e30
BIG = 1e9
N_GATES = 3 * HEADS_PER_GROUP
GROUP_LANES = NSA_GROUPS * HEAD_DIM
PROMPT_NQ = 128
WIN_SPAN = WINDOW + PROMPT_NQ
MASK_ROWS = 64
MASK_BIAS = -(2.0 ** 100)
GATHER_PAGES = 8
LOG2E = 1.4426950408889634

VMEM_LIMIT_BYTES = 56 * 1024 * 1024
LANES = 128

NT_DIMS = (((1,), (1,)), ((), ()))
TN_DIMS = (((0,), (0,)), ((), ()))


def _params(*semantics):
    return pltpu.CompilerParams(dimension_semantics=semantics, vmem_limit_bytes=VMEM_LIMIT_BYTES)


def _rms(x, gain):
    ms = jnp.mean(x * x, axis=-1, keepdims=True)
    return x * lax.rsqrt(ms + EPS) * gain


def _seg_rms(y, ones_bd, gain):
    sq = y * y
    hi = sq.astype(BF16)
    lo = (sq - hi.astype(F32)).astype(BF16)
    ss = jnp.dot(hi, ones_bd, preferred_element_type=F32) + jnp.dot(lo, ones_bd, preferred_element_type=F32)
    return y * lax.rsqrt(ss * (1.0 / HEAD_DIM) + EPS) * gain


def _rope64(x, cos, sin_signed):
    lane = lax.broadcasted_iota(I32, x.shape, 1)
    first_half = (lane % HEAD_DIM) < (HEAD_DIM // 2)
    rot = jnp.where(first_half, pltpu.roll(x, LANES - HEAD_DIM // 2, 1), pltpu.roll(x, HEAD_DIM // 2, 1))
    return x * cos + rot * sin_signed


def _exp_rows(s, ok):
    s = jnp.where(ok, s, NEG)
    m = jnp.max(s, axis=-1, keepdims=True)
    e = jnp.exp2(s - m)
    return e, jnp.sum(e, axis=-1, keepdims=True)


def _ffn_kernel(x_ref, g_ref, wi_ref, wo_ref, o_ref):
    x = x_ref[...]
    xn = _rms(x, g_ref[...]).astype(BF16)
    a = jnp.dot(xn, wi_ref[:, 0:D_FF], preferred_element_type=F32)
    b = jnp.dot(xn, wi_ref[:, D_FF:2 * D_FF], preferred_element_type=F32)
    h = (a * jax.nn.sigmoid(a) * b).astype(BF16)
    o_ref[...] = x + 0.5 * jnp.dot(h, wo_ref[...], preferred_element_type=F32)


def ffn_half(x, gain, w_in, w_out, layer, half):
    n = x.shape[0]
    tm = min(n, 512)
    resident = pl.Buffered(1)
    return pl.pallas_call(
        _ffn_kernel,
        grid=(n // tm,),
        in_specs=[
            pl.BlockSpec((tm, D_MODEL), lambda i: (i, 0)),
            pl.BlockSpec((1, D_MODEL), lambda i: (0, 0)),
            pl.BlockSpec((None, None, D_MODEL, 2 * D_FF), lambda i: (layer, half, 0, 0), pipeline_mode=resident),
            pl.BlockSpec((None, None, D_FF, D_MODEL), lambda i: (layer, half, 0, 0), pipeline_mode=resident),
        ],
        out_specs=pl.BlockSpec((tm, D_MODEL), lambda i: (i, 0)),
        out_shape=jax.ShapeDtypeStruct((n, D_MODEL), F32),
        compiler_params=_params("parallel"),
        name="ffn_half",
    )(x, gain.reshape(1, D_MODEL), w_in, w_out)


def _proj_ffn_kernel(a_ref, wp_ref, x_ref, g_ref, wi_ref, wo_ref, o_ref):
    x = x_ref[...] + jnp.dot(a_ref[...], wp_ref[...], preferred_element_type=F32)
    xn = _rms(x, g_ref[...]).astype(BF16)
    a = jnp.dot(xn, wi_ref[:, 0:D_FF], preferred_element_type=F32)
    b = jnp.dot(xn, wi_ref[:, D_FF:2 * D_FF], preferred_element_type=F32)
    h = (a * jax.nn.sigmoid(a) * b).astype(BF16)
    o_ref[...] = x + 0.5 * jnp.dot(h, wo_ref[...], preferred_element_type=F32)


def proj_ffn_half(a, w_proj, proj_layer, x, gain, w_in, w_out, layer, half):
    n, k = a.shape
    tm = min(n, 512)
    resident = pl.Buffered(1)
    return pl.pallas_call(
        _proj_ffn_kernel,
        grid=(n // tm,),
        in_specs=[
            pl.BlockSpec((tm, k), lambda i: (i, 0)),
            pl.BlockSpec((None, k, D_MODEL), lambda i: (proj_layer, 0, 0), pipeline_mode=resident),
            pl.BlockSpec((tm, D_MODEL), lambda i: (i, 0)),
            pl.BlockSpec((1, D_MODEL), lambda i: (0, 0)),
            pl.BlockSpec((None, None, D_MODEL, 2 * D_FF), lambda i: (layer, half, 0, 0), pipeline_mode=resident),
            pl.BlockSpec((None, None, D_FF, D_MODEL), lambda i: (layer, half, 0, 0), pipeline_mode=resident),
        ],
        out_specs=pl.BlockSpec((tm, D_MODEL), lambda i: (i, 0)),
        out_shape=jax.ShapeDtypeStruct((n, D_MODEL), F32),
        compiler_params=_params("parallel"),
        name="proj_ffn_half",
    )(a, w_proj, x, gain.reshape(1, D_MODEL), w_in, w_out)


def _ret_inproj_kernel(x_ref, g_ref, w_ref, cos_ref, sin_ref, o_ref):
    xn = _rms(x_ref[...], g_ref[...]).astype(BF16)
    c = cos_ref[...]
    s = sin_ref[...]
    half = RET_DK // 2
    for h in range(2 * RET_HEADS):
        lo = h * RET_DK
        y = jnp.dot(xn, w_ref[:, lo:lo + RET_DK], preferred_element_type=F32)
        scale = 1.0 if h < RET_HEADS else RET_DK ** -0.5
        x1 = y[:, :half]
        x2 = y[:, half:]
        o_ref[:, lo:lo + half] = ((x1 * c - x2 * s) * scale).astype(BF16)
        o_ref[:, lo + half:lo + RET_DK] = ((x1 * s + x2 * c) * scale).astype(BF16)
    for h in range(2 * RET_HEADS):
        lo = 2 * D_MODEL + h * RET_DV
        o_ref[:, lo:lo + RET_DV] = jnp.dot(xn, w_ref[:, lo:lo + RET_DV], preferred_element_type=F32).astype(BF16)


def ret_inproj(x, gain, w, layer, cos, sin):
    n = x.shape[0]
    p = cos.shape[0]
    tm = min(n, 512, p)
    n_out = w.shape[2]
    tab_blocks = p // tm
    return pl.pallas_call(
        _ret_inproj_kernel,
        grid=(n // tm,),
        in_specs=[
            pl.BlockSpec((tm, D_MODEL), lambda i: (i, 0)),
            pl.BlockSpec((1, D_MODEL), lambda i: (0, 0)),
            pl.BlockSpec((None, D_MODEL, n_out), lambda i: (layer, 0, 0), pipeline_mode=pl.Buffered(1)),
            pl.BlockSpec((tm, RET_DK // 2), lambda i: (i % tab_blocks, 0)),
            pl.BlockSpec((tm, RET_DK // 2), lambda i: (i % tab_blocks, 0)),
        ],
        out_specs=pl.BlockSpec((tm, n_out), lambda i: (i, 0)),
        out_shape=jax.ShapeDtypeStruct((n, n_out), BF16),
        compiler_params=_params("parallel"),
        name="ret_inproj",
    )(x, gain.reshape(1, D_MODEL), w, cos, sin)


def _ret_core_kernel(*refs, chunk, n_inner, hps, has_s0):
    if has_s0:
        (q_ref, k_ref, v_ref, g_ref, dm_ref, qd_ref, kd_ref, sd_ref, s0_ref, o_ref, so_ref, s_scr) = refs
    else:
        (q_ref, k_ref, v_ref, g_ref, dm_ref, qd_ref, kd_ref, sd_ref, o_ref, so_ref, s_scr) = refs
    t = pl.program_id(2)

    @pl.when(t == 0)
    def _():
        if has_s0:
            s_scr[...] = s0_ref[0]
        else:
            s_scr[...] = jnp.zeros_like(s_scr)

    for c in range(n_inner):
        rows = slice(c * chunk, (c + 1) * chunk)
        for h in range(hps):
            qk_cols = slice(h * RET_DK, (h + 1) * RET_DK)
            v_cols = slice(h * RET_DV, (h + 1) * RET_DV)
            q = q_ref[0, rows, qk_cols]
            k = k_ref[0, rows, qk_cols]
            v = v_ref[0, rows, v_cols]
            g = g_ref[0, rows, v_cols].astype(F32)
            s = s_scr[h]
            scores = lax.dot_general(q, k, NT_DIMS, preferred_element_type=F32) * dm_ref[h]
            intra = jnp.dot(scores.astype(BF16), v, preferred_element_type=F32)
            cross = jnp.dot((q.astype(F32) * qd_ref[h]).astype(BF16), s.astype(BF16), preferred_element_type=F32)
            o = intra + cross
            kv = lax.dot_general((k.astype(F32) * kd_ref[h]).astype(BF16), v, TN_DIMS, preferred_element_type=F32)
            s_scr[h] = sd_ref[h, 0:1, 0:1] * s + kv
            mu = jnp.mean(o, axis=-1, keepdims=True)
            d = o - mu
            var = jnp.mean(d * d, axis=-1, keepdims=True)
            on = d * lax.rsqrt(var + EPS)
            o_ref[0, rows, v_cols] = (g * jax.nn.sigmoid(g) * on).astype(BF16)

    @pl.when(t == pl.num_programs(2) - 1)
    def _():
        so_ref[0] = s_scr[...]


def _decay_tables(chunk):
    lg = jnp.log(1.0 - 2.0 ** (-5.0 - jnp.arange(RET_HEADS, dtype=F32)))
    idx = jnp.arange(chunk, dtype=F32)
    rel = idx[:, None] - idx[None, :]
    dmat = jnp.where(rel >= 0, jnp.exp(jnp.maximum(rel, 0.0)[None] * lg[:, None, None]), 0.0)
    qdec = jnp.exp((idx + 1.0)[None, :] * lg[:, None])
    kdec = jnp.exp((chunk - 1.0 - idx)[None, :] * lg[:, None])
    sdec = jnp.exp(chunk * lg)
    qdec = jnp.broadcast_to(qdec[:, :, None], (RET_HEADS, chunk, RET_DK))
    kdec = jnp.broadcast_to(kdec[:, :, None], (RET_HEADS, chunk, RET_DK))
    sdec = jnp.broadcast_to(sdec[:, None, None], (RET_HEADS, 8, LANES))
    return dmat, qdec, kdec, sdec


def ret_core(qkvg, s0_all, layer):
    b, t, _ = qkvg.shape
    chunk = min(t, 4 * RET_CHUNK) if t % RET_CHUNK == 0 else t
    tb = chunk
    n_inner = tb // chunk
    hps = RET_HEADS if t < RET_CHUNK else 1
    nh = RET_HEADS // hps
    dmat, qdec, kdec, sdec = _decay_tables(chunk)
    in_specs = [
        pl.BlockSpec((1, tb, hps * RET_DK), lambda bi, h, ti: (bi, ti, h)),
        pl.BlockSpec((1, tb, hps * RET_DK), lambda bi, h, ti: (bi, ti, nh + h)),
        pl.BlockSpec((1, tb, hps * RET_DV), lambda bi, h, ti: (bi, ti, nh + h)),
        pl.BlockSpec((1, tb, hps * RET_DV), lambda bi, h, ti: (bi, ti, 2 * nh + h)),
        pl.BlockSpec((hps, chunk, chunk), lambda bi, h, ti: (h, 0, 0)),
        pl.BlockSpec((hps, chunk, RET_DK), lambda bi, h, ti: (h, 0, 0)),
        pl.BlockSpec((hps, chunk, RET_DK), lambda bi, h, ti: (h, 0, 0)),
        pl.BlockSpec((hps, 8, LANES), lambda bi, h, ti: (h, 0, 0)),
    ]
    args = [qkvg, qkvg, qkvg, qkvg, dmat, qdec, kdec, sdec]
    if s0_all is not None:
        in_specs.append(pl.BlockSpec((None, 1, hps, RET_DK, RET_DV), lambda bi, h, ti: (layer, bi, h, 0, 0)))
        args.append(s0_all)
    return pl.pallas_call(
        functools.partial(_ret_core_kernel, chunk=chunk, n_inner=n_inner, hps=hps, has_s0=s0_all is not None),
        grid=(b, nh, t // tb),
        in_specs=in_specs,
        out_specs=[
            pl.BlockSpec((1, tb, hps * RET_DV), lambda bi, h, ti: (bi, ti, h)),
            pl.BlockSpec((1, hps, RET_DK, RET_DV), lambda bi, h, ti: (bi, h, 0, 0)),
        ],
        out_shape=[
            jax.ShapeDtypeStruct((b, t, 2 * D_MODEL), BF16),
            jax.ShapeDtypeStruct((b, RET_HEADS, RET_DK, RET_DV), F32),
        ],
        scratch_shapes=[pltpu.VMEM((hps, RET_DK, RET_DV), F32)],
        compiler_params=_params("parallel", "parallel", "arbitrary"),
        name="ret_core",
    )(*args)


def _store_chunk_major(cmp_ref, scr, n_rows):
    n_planes = scr.shape[0]
    for p in range(CMP_STRIDE):
        for c in range(n_planes):
            lo = p * n_planes * LANES + c * LANES
            cmp_ref[:, lo:lo + LANES] = scr[c, pl.ds(p, n_rows // CMP_STRIDE, stride=CMP_STRIDE), :].astype(BF16)


def _kv_rows_kernel(x_ref, g_ref, w_ref, kn_ref, cos_ref, sin_ref, ones_ref, rows_ref, win_ref, *aux_refs):
    xn = _rms(x_ref[0], g_ref[...]).astype(BF16)
    y = jnp.dot(xn, w_ref[...], preferred_element_type=F32)
    tm = y.shape[0]
    cos = cos_ref[...]
    sin = sin_ref[...]
    ones_bd = ones_ref[...]
    slot = lambda s: y[:, s * GROUP_LANES:(s + 1) * GROUP_LANES]

    def norm_rope(v, gain):
        vn = _seg_rms(v, ones_bd, gain)
        return jnp.concatenate([_rope64(vn[:, :LANES], cos, sin), _rope64(vn[:, LANES:], cos, sin)], axis=1)

    k_slc = norm_rope(slot(2), kn_ref[1:2, :])
    k_win = norm_rope(slot(4), kn_ref[2:3, :])
    rows_ref[0, :, 0:2 * GROUP_LANES] = y[:, 0:2 * GROUP_LANES]
    rows_ref[0, :, 2 * GROUP_LANES:3 * GROUP_LANES] = k_slc
    rows_ref[0, :, 3 * GROUP_LANES:4 * GROUP_LANES] = slot(3)
    win_ref[0, :, 0:GROUP_LANES] = k_win
    win_ref[0, :, GROUP_LANES:2 * GROUP_LANES] = slot(5)
    if aux_refs:
        cmp_ref, ksel_ref, vsel_ref, kwin_ref, vwin_ref, scr = aux_refs
        for c in range(scr.shape[0]):
            scr[c] = y[:, c * LANES:(c + 1) * LANES]
        _store_chunk_major(cmp_ref, scr, tm)
        for ref, val in ((ksel_ref, k_slc), (vsel_ref, slot(3)), (kwin_ref, k_win), (vwin_ref, slot(5))):
            vt = jnp.transpose(val)
            for g in range(NSA_GROUPS):
                ref[0, g, 0:HEAD_DIM, :] = vt[g * HEAD_DIM:(g + 1) * HEAD_DIM, :].astype(BF16)
        key_blk = (pl.program_id(1) * tm + lax.broadcasted_iota(I32, (MASK_ROWS, tm), 1)) // SEL_BLOCK
        mask_rows = jnp.where(key_blk == lax.broadcasted_iota(I32, (MASK_ROWS, tm), 0), MASK_BIAS, 0.0).astype(BF16)
        for g in range(NSA_GROUPS):
            ksel_ref[0, g, HEAD_DIM:HEAD_DIM + MASK_ROWS, :] = mask_rows


def kv_rows(x, gain, w, k_norm_tiled, cos, sin, ones_bd, aux):
    b, t, _ = x.shape
    tm = min(t, 512)
    n_kv = w.shape[1]
    nt = t // tm
    out_specs = [
        pl.BlockSpec((1, tm, 4 * GROUP_LANES), lambda bi, ti: (bi, ti, 0)),
        pl.BlockSpec((1, tm, 2 * GROUP_LANES), lambda bi, ti: (bi, ti, 0)),
    ]
    out_shape = [
        jax.ShapeDtypeStruct((b, t, 4 * GROUP_LANES), F32),
        jax.ShapeDtypeStruct((b, t, 2 * GROUP_LANES), F32),
    ]
    scratch = []
    if aux:
        chunk_lanes = CMP_STRIDE * 2 * GROUP_LANES
        out_specs.append(pl.BlockSpec((tm // CMP_STRIDE, chunk_lanes), lambda bi, ti: (bi * nt + ti, 0)))
        out_shape.append(jax.ShapeDtypeStruct((b * t // CMP_STRIDE, chunk_lanes), BF16))
        for rows in (HEAD_DIM + MASK_ROWS, HEAD_DIM, HEAD_DIM, HEAD_DIM):
            out_specs.append(pl.BlockSpec((1, NSA_GROUPS, rows, tm), lambda bi, ti: (bi, 0, 0, ti)))
            out_shape.append(jax.ShapeDtypeStruct((b, NSA_GROUPS, rows, t), BF16))
        scratch.append(pltpu.VMEM((2 * GROUP_LANES // LANES, tm, LANES), F32))
    return pl.pallas_call(
        _kv_rows_kernel,
        grid=(b, nt),
        in_specs=[
            pl.BlockSpec((1, tm, D_MODEL), lambda bi, ti: (bi, ti, 0)),
            pl.BlockSpec((1, D_MODEL), lambda bi, ti: (0, 0)),
            pl.BlockSpec((D_MODEL, n_kv), lambda bi, ti: (0, 0)),
            pl.BlockSpec((3, GROUP_LANES), lambda bi, ti: (0, 0)),
            pl.BlockSpec((tm, LANES), lambda bi, ti: (ti, 0)),
            pl.BlockSpec((tm, LANES), lambda bi, ti: (ti, 0)),
            pl.BlockSpec((GROUP_LANES, GROUP_LANES), lambda bi, ti: (0, 0)),
        ],
        out_specs=out_specs,
        out_shape=out_shape,
        scratch_shapes=scratch,
        compiler_params=_params("parallel", "parallel"),
        name="kv_rows",
    )(x, gain.reshape(1, D_MODEL), w, k_norm_tiled, cos, sin, ones_bd)


def _cmp_partial_kernel(x_ref, wk_ref, wv_ref, o_ref):
    acc_k = jnp.zeros((x_ref.shape[0], 2 * GROUP_LANES), F32)
    acc_v = jnp.zeros((x_ref.shape[0], 2 * GROUP_LANES), F32)
    for p in range(CMP_STRIDE):
        lo = p * 2 * GROUP_LANES
        acc_k += jnp.dot(x_ref[:, lo:lo + GROUP_LANES], wk_ref[p], preferred_element_type=F32)
        acc_v += jnp.dot(x_ref[:, lo + GROUP_LANES:lo + 2 * GROUP_LANES], wv_ref[p], preferred_element_type=F32)
    o_ref[:, 0:2 * GROUP_LANES] = acc_k
    o_ref[:, 2 * GROUP_LANES:4 * GROUP_LANES] = acc_v


def cmp_partial(tok_chunks, wk_bd, wv_bd):
    n, width = tok_chunks.shape
    tm = min(n, 512)
    return pl.pallas_call(
        _cmp_partial_kernel,
        grid=(n // tm,),
        in_specs=[
            pl.BlockSpec((tm, width), lambda i: (i, 0)),
            pl.BlockSpec(wk_bd.shape, lambda i: (0, 0, 0)),
            pl.BlockSpec(wv_bd.shape, lambda i: (0, 0, 0)),
        ],
        out_specs=pl.BlockSpec((tm, 4 * GROUP_LANES), lambda i: (i, 0)),
        out_shape=jax.ShapeDtypeStruct((n, 4 * GROUP_LANES), F32),
        compiler_params=_params("parallel"),
        name="cmp_partial",
    )(tok_chunks, wk_bd, wv_bd)


def _cmp_combine_kernel(a_ref, pe_ref, w1_ref, b1_ref, w2_ref, kn_ref, cos_ref, sin_ref, ones_ref, kc_ref, vc_ref):
    a = a_ref[0]
    n = a.shape[0]
    for t, out_ref in enumerate((kc_ref, vc_ref)):
        first = a[:, 2 * t * GROUP_LANES:(2 * t + 1) * GROUP_LANES]
        second = a[:, (2 * t + 1) * GROUP_LANES:(2 * t + 2) * GROUP_LANES]
        second = pltpu.roll(second, n - 1, 0)
        pe_term = jnp.dot(pe_ref[t], w1_ref[t], preferred_element_type=F32)[0:1]
        h = b1_ref[t] + pe_term + first + second
        y = jnp.dot(jax.nn.gelu(h).astype(BF16), w2_ref[t], preferred_element_type=F32)
        if t == 0:
            y = _seg_rms(y, ones_ref[...], kn_ref[0:1, :])
            y = jnp.concatenate(
                [_rope64(y[:, :LANES], cos_ref[...], sin_ref[...]), _rope64(y[:, LANES:], cos_ref[...], sin_ref[...])],
                axis=1)
        yt = jnp.transpose(y)
        for g in range(NSA_GROUPS):
            out_ref[0, g] = yt[g * HEAD_DIM:(g + 1) * HEAD_DIM, :].astype(BF16)


def cmp_combine(partial, pe_rows, w1_tiled, b1_tiled, w2_bd, k_norm_tiled, cos, sin, ones_bd):
    b, ncp, _ = partial.shape
    full = lambda *shape: pl.BlockSpec(shape, lambda bi: (0,) * len(shape))
    return pl.pallas_call(
        _cmp_combine_kernel,
        grid=(b,),
        in_specs=[
            pl.BlockSpec((1, ncp, 4 * GROUP_LANES), lambda bi: (bi, 0, 0)),
            full(*pe_rows.shape), full(*w1_tiled.shape), full(*b1_tiled.shape), full(*w2_bd.shape),
            full(3, GROUP_LANES), full(ncp, LANES), full(ncp, LANES), full(GROUP_LANES, GROUP_LANES),
        ],
        out_specs=[pl.BlockSpec((1, NSA_GROUPS, HEAD_DIM, ncp), lambda bi: (bi, 0, 0, 0))] * 2,
        out_shape=[jax.ShapeDtypeStruct((b, NSA_GROUPS, HEAD_DIM, ncp), BF16)] * 2,
        compiler_params=_params("parallel"),
        name="cmp_combine",
    )(partial, pe_rows, w1_tiled, b1_tiled, w2_bd, k_norm_tiled, cos, sin, ones_bd)


def _nsa_q_kernel(x_ref, g_ref, wq_ref, wg_ref, qn_ref, cos_ref, sin_ref, ones_ref, q_ref, gate_ref):
    xn = _rms(x_ref[0], g_ref[...]).astype(BF16)
    y = jnp.dot(xn, wq_ref[...], preferred_element_type=F32)
    cos = cos_ref[...]
    sin = sin_ref[...]
    scale = HEAD_DIM ** -0.5 * LOG2E
    for g in range(NSA_GROUPS):
        yg = _seg_rms(y[:, g * GROUP_LANES:(g + 1) * GROUP_LANES], ones_ref[...], qn_ref[...])
        low = lax.broadcasted_iota(I32, (y.shape[0], LANES), 1) < HEAD_DIM
        for half in range(2):
            r = _rope64(yg[:, half * LANES:(half + 1) * LANES], cos, sin) * scale
            q_ref[0, g, 2 * half] = jnp.where(low, r, 0.0)
            q_ref[0, g, 2 * half + 1] = jnp.where(low, pltpu.roll(r, HEAD_DIM, 1), 0.0)
    gates = jnp.dot(xn, wg_ref[...], preferred_element_type=F32)
    gate_ref[0] = jax.nn.sigmoid(gates)


def nsa_q(x, gain, wq, wg, layer, q_norm_tiled, cos, sin, ones_bd):
    b, t, _ = x.shape
    tm = min(t, 512)
    return pl.pallas_call(
        _nsa_q_kernel,
        grid=(b, t // tm),
        in_specs=[
            pl.BlockSpec((1, tm, D_MODEL), lambda bi, ti: (bi, ti, 0)),
            pl.BlockSpec((1, D_MODEL), lambda bi, ti: (0, 0)),
            pl.BlockSpec((None,) + wq.shape[1:], lambda bi, ti: (layer, 0, 0)),
            pl.BlockSpec((None,) + wg.shape[1:], lambda bi, ti: (layer, 0, 0)),
            pl.BlockSpec((1, GROUP_LANES), lambda bi, ti: (0, 0)),
            pl.BlockSpec((tm, LANES), lambda bi, ti: (ti, 0)),
            pl.BlockSpec((tm, LANES), lambda bi, ti: (ti, 0)),
            pl.BlockSpec((GROUP_LANES, GROUP_LANES), lambda bi, ti: (0, 0)),
        ],
        out_specs=[
            pl.BlockSpec((1, NSA_GROUPS, HEADS_PER_GROUP, tm, LANES), lambda bi, ti: (bi, 0, 0, ti, 0)),
            pl.BlockSpec((1, tm, NSA_GROUPS * LANES), lambda bi, ti: (bi, ti, 0)),
        ],
        out_shape=[
            jax.ShapeDtypeStruct((b, NSA_GROUPS, HEADS_PER_GROUP, t, LANES), F32),
            jax.ShapeDtypeStruct((b, t, NSA_GROUPS * LANES), F32),
        ],
        compiler_params=_params("parallel", "parallel"),
        name="nsa_q",
    )(x, gain.reshape(1, D_MODEL), wq, wg, q_norm_tiled, cos, sin, ones_bd)


def _compressed_branch(q, qpos, q_first, kc_t, vc_t, ovt, nq, nc, ns):
    hpg = HEADS_PER_GROUP
    ncp = kc_t.shape[1]
    nsp = ovt.shape[0]
    s_c = jnp.dot(q, kc_t, preferred_element_type=F32)
    cidx = lax.broadcasted_iota(I32, (1, ncp), 1)
    c_end = jnp.where(cidx < nc, cidx * CMP_STRIDE + (CMP_BLOCK - 1), 2 ** 30)
    valid_c = c_end <= qpos
    e_c, l_c = _exp_rows(s_c, valid_c)
    p_c = jnp.where(valid_c, e_c * (1.0 / l_c), 0.0)
    o_c = lax.dot_general(p_c.astype(BF16), vc_t, NT_DIMS, preferred_element_type=F32)
    p_sum = p_c[0:nq]
    for hh in range(1, hpg):
        p_sum = p_sum + p_c[hh * nq:(hh + 1) * nq]
    p_hi = p_sum.astype(BF16)
    p_lo = (p_sum - p_hi.astype(F32)).astype(BF16)
    imp_t = (lax.dot_general(ovt, p_hi, NT_DIMS, preferred_element_type=F32)
             + lax.dot_general(ovt, p_lo, NT_DIMS, preferred_element_type=F32))
    sidx = lax.broadcasted_iota(I32, (nsp, 1), 0)
    q_blk = (q_first + lax.broadcasted_iota(I32, (1, nq), 1)) // SEL_BLOCK
    valid_s = (sidx <= q_blk) & (sidx < ns)
    forced = (sidx == 0) | (valid_s & (q_blk - sidx < N_LOCAL))
    score_t = jnp.where(forced, BIG, jnp.where(valid_s, imp_t, NEG))
    return o_c, score_t, valid_s


def _window_branch(s_w, qpos, vw_t, kpos0):
    kpos_w = kpos0 + lax.broadcasted_iota(I32, (1, s_w.shape[1]), 1)
    kpos_w = jnp.where(kpos_w >= 0, kpos_w, -(2 ** 30))
    behind = lax.bitcast_convert_type(qpos - kpos_w, jnp.uint32)
    e_w, l_w = _exp_rows(s_w, behind < jnp.uint32(WINDOW))
    return lax.dot_general(e_w.astype(BF16), vw_t, NT_DIMS, preferred_element_type=F32), l_w


def _gate_and_store(o_ref, lane0, gates, o_c, acc_s, l_s, acc_w, l_w, nq):
    inv_s = 1.0 / l_s
    inv_w = 1.0 / l_w
    for hh in range(HEADS_PER_GROUP):
        rows = slice(hh * nq, (hh + 1) * nq)
        o_h = (gates[:, 3 * hh:3 * hh + 1] * o_c[rows] + (gates[:, 3 * hh + 1:3 * hh + 2] * inv_s[rows]) * acc_s[rows]
               + (gates[:, 3 * hh + 2:3 * hh + 3] * inv_w[rows]) * acc_w[rows])
        o_ref[0, :, lane0 + hh * HEAD_DIM:lane0 + (hh + 1) * HEAD_DIM] = o_h.astype(BF16)


def _nsa_prompt_kernel(q_ref, gate_ref, kc_ref, vc_ref, ovt_ref, ksel_ref, vsel_ref, kwin_ref, vwin_ref,
                       o_ref, score_scr, s_a, s_b, s_w, *, nq, nc, ns, kc_keys):
    i = pl.program_id(2)
    hpg = HEADS_PER_GROUP
    r = hpg * nq
    nsp = ovt_ref.shape[0]
    t_len = kwin_ref.shape[3]
    q_first = i * nq
    q_wide = q_ref[0, 0].reshape(r, LANES)
    q = q_wide[:, 0:HEAD_DIM].astype(BF16)
    qpos = q_first + lax.broadcasted_iota(I32, (r, 1), 0) % nq
    w0 = pl.multiple_of(jnp.clip(q_first + nq - WIN_SPAN, 0, t_len - WIN_SPAN), LANES)
    s_w[...] = jnp.dot(q, kwin_ref[0, 0, :, pl.ds(w0, WIN_SPAN)], preferred_element_type=F32)
    o_c, score_t, valid_s = _compressed_branch(q, qpos, q_first, kc_ref[0, 0], vc_ref[0, 0], ovt_ref[...], nq, nc, ns)

    score_scr[...] = score_t
    blk = lax.broadcasted_iota(I32, (nsp, nq), 0)
    n_live = jnp.minimum((q_first + nq - 1) // SEL_BLOCK + 1, ns)

    def rank_body(sp, rank):
        row = score_scr[pl.ds(sp, 1), :]
        tie = jnp.where(row == score_t, jnp.where(blk > sp, 1.0, 0.0), 0.0)
        return rank + jnp.where(row > score_t, 1.0, tie)

    rank = lax.fori_loop(0, n_live, rank_body, jnp.zeros((nsp, nq), F32))
    dropped_t = 1.0 - ((rank < float(min(N_SELECT, ns))) & valid_s).astype(F32)
    dropped = jnp.transpose(jnp.concatenate([jnp.zeros((HEAD_DIM, nq), F32), dropped_t], axis=0))
    qa = (q_wide + jnp.concatenate([dropped] * hpg, axis=0)).astype(BF16)

    def scores_into(dst, c):
        start = pl.multiple_of(c * kc_keys, kc_keys)
        dst[...] = jnp.dot(qa, ksel_ref[0, 0, :, pl.ds(start, kc_keys)], preferred_element_type=F32)

    def update(src, c, carry, causal):
        m, l, acc = carry
        start = pl.multiple_of(c * kc_keys, kc_keys)
        s = src[...]
        if causal:
            kpos = c * kc_keys + lax.broadcasted_iota(I32, (1, kc_keys), 1)
            s = jnp.where(kpos <= qpos, s, NEG)
        m_new = jnp.maximum(m, jnp.max(s, axis=-1, keepdims=True))
        alpha = jnp.exp2(m - m_new)
        p = jnp.exp2(s - m_new)
        l = alpha * l + jnp.sum(p, axis=-1, keepdims=True)
        pv = lax.dot_general(p.astype(BF16), vsel_ref[0, 0, :, pl.ds(start, kc_keys)], NT_DIMS,
                             preferred_element_type=F32)
        return m_new, l, alpha * acc + pv

    n_past = q_first // kc_keys
    scores_into(s_a, 0)

    def pair_body(pi, carry):
        c = 2 * pi
        scores_into(s_b, c + 1)
        carry = update(s_a, c, carry, causal=False)
        scores_into(s_a, c + 2)
        return update(s_b, c + 1, carry, causal=False)

    init = (jnp.full((r, 1), NEG, F32), jnp.zeros((r, 1), F32), jnp.zeros((r, HEAD_DIM), F32))
    carry = lax.fori_loop(0, n_past // 2, pair_body, init)

    def odd_tail(carry):
        scores_into(s_b, n_past)
        carry = update(s_a, n_past - 1, carry, causal=False)
        return update(s_b, n_past, carry, causal=True)

    def even_tail(carry):
        return update(s_a, n_past, carry, causal=True)

    _, l_s, acc_s = lax.cond(n_past % 2 == 1, odd_tail, even_tail, carry)

    acc_w, l_w = _window_branch(s_w[...], qpos, vwin_ref[0, 0, :, pl.ds(w0, WIN_SPAN)], w0)
    _gate_and_store(o_ref, 0, gate_ref[0], o_c, acc_s, l_s, acc_w, l_w, nq)


def nsa_attend_prompt(q, gates, kc_t, vc_t, ovt, ksel_t, vsel_t, kwin_t, vwin_t, *, nc, ns):
    b, _, _, t, _ = q.shape
    assert ovt.shape[0] == MASK_ROWS == LANES - HEAD_DIM
    nq = PROMPT_NQ
    kc_keys = min(512, t)
    ncp = kc_t.shape[3]
    nsp = ovt.shape[0]
    per_bg = lambda rows, cols: pl.BlockSpec((1, 1, rows, cols), lambda bi, g, ti: (bi, g, 0, 0))
    kernel = functools.partial(_nsa_prompt_kernel, nq=nq, nc=nc, ns=ns, kc_keys=kc_keys)
    return pl.pallas_call(
        kernel,
        grid=(b, NSA_GROUPS, t // nq),
        in_specs=[
            pl.BlockSpec((1, 1, HEADS_PER_GROUP, nq, LANES), lambda bi, g, ti: (bi, g, 0, ti, 0)),
            pl.BlockSpec((1, nq, LANES), lambda bi, g, ti: (bi, ti, g)),
            per_bg(HEAD_DIM, ncp), per_bg(HEAD_DIM, ncp),
            pl.BlockSpec((nsp, ncp), lambda bi, g, ti: (0, 0)),
            per_bg(HEAD_DIM + MASK_ROWS, t), per_bg(HEAD_DIM, t), per_bg(HEAD_DIM, t), per_bg(HEAD_DIM, t),
        ],
        out_specs=pl.BlockSpec((1, nq, GROUP_LANES), lambda bi, g, ti: (bi, ti, g)),
        out_shape=jax.ShapeDtypeStruct((b, t, NSA_HEADS * HEAD_DIM), BF16),
        scratch_shapes=[pltpu.VMEM((nsp, nq), F32), pltpu.VMEM((HEADS_PER_GROUP * nq, kc_keys), F32),
                        pltpu.VMEM((HEADS_PER_GROUP * nq, kc_keys), F32),
                        pltpu.VMEM((HEADS_PER_GROUP * nq, WIN_SPAN), F32)],
        compiler_params=_params("parallel", "parallel", "arbitrary"),
        name="nsa_attend_prompt",
    )(q, gates, kc_t, vc_t, ovt, ksel_t, vsel_t, kwin_t, vwin_t)


def _nsa_decode_kernel(q_ref, gate_ref, kc_ref, vc_ref, ovt_ref, ind_ref, ksel_ref, vsel_ref, kwin_ref, vwin_ref,
                       o_ref, *, nq, nc, ns, q_pos0, win_pos0):
    hpg = HEADS_PER_GROUP
    r = hpg * nq
    nsp = ovt_ref.shape[0]
    n_slabs, slab = ksel_ref.shape[2], ksel_ref.shape[4]
    l_keys = n_slabs * slab
    qpos = q_pos0 + lax.broadcasted_iota(I32, (r, 1), 0) % nq
    s_other = lax.broadcasted_iota(I32, (nsp, 1), 0)
    s_self = lax.broadcasted_iota(I32, (1, nsp), 1)
    qrow = lax.broadcasted_iota(I32, (nq, 1), 0)
    kpos = lax.broadcasted_iota(I32, (1, l_keys), 1)
    for g in range(NSA_GROUPS):
        q = q_ref[0, g].reshape(r, LANES)[:, 0:HEAD_DIM].astype(BF16)
        o_c, score_t, valid_s = _compressed_branch(q, qpos, q_pos0, kc_ref[0, g], vc_ref[0, g], ovt_ref[...],
                                                   nq, nc, ns)
        score = jnp.transpose(score_t)
        valid = jnp.transpose(valid_s.astype(F32))
        rank = jnp.zeros((nq, nsp), F32)
        for qi in range(nq):
            other = score_t[:, qi:qi + 1]
            own = score[qi:qi + 1, :]
            ahead = (other > own) | ((other == own) & (s_other < s_self))
            rank = jnp.where(qrow == qi, jnp.sum(ahead.astype(F32), axis=0, keepdims=True), rank)
        sel = ((rank < float(min(N_SELECT, ns))) & (valid > 0.5)).astype(F32)
        sel_rows = jnp.concatenate([sel] * hpg, axis=0).astype(BF16)

        s = jnp.concatenate([jnp.dot(q, ksel_ref[0, g, c], preferred_element_type=F32) for c in range(n_slabs)],
                            axis=1)
        picked = jnp.dot(sel_rows, ind_ref[...], preferred_element_type=F32)
        e_s, l_s = _exp_rows(s, (picked > 0.5) & (kpos <= qpos))
        e_s = e_s.astype(BF16)
        acc_s = jnp.zeros((r, HEAD_DIM), F32)
        for c in range(n_slabs):
            acc_s += lax.dot_general(e_s[:, c * slab:(c + 1) * slab], vsel_ref[0, g, c], NT_DIMS,
                                     preferred_element_type=F32)

        s_w = jnp.dot(q, kwin_ref[0, g], preferred_element_type=F32)
        acc_w, l_w = _window_branch(s_w, qpos, vwin_ref[0, g], win_pos0)
        _gate_and_store(o_ref, g * GROUP_LANES, gate_ref[0, :, g * LANES:(g + 1) * LANES], o_c, acc_s, l_s, acc_w, l_w, nq)


def nsa_attend_decode(q, gates, kc_t, vc_t, ovt, ind, ksel_t, vsel_t, kwin_t, vwin_t, *, nc, ns, q_pos0, win_pos0):
    b, _, _, nq, _ = q.shape
    ncp = kc_t.shape[3]
    nsp = ovt.shape[0]
    n_slabs, slab = ksel_t.shape[2], ksel_t.shape[4]
    l_keys = n_slabs * slab
    slabs = pl.BlockSpec((1, NSA_GROUPS, n_slabs, HEAD_DIM, slab), lambda bi: (bi, 0, 0, 0, 0))
    per_b = lambda cols: pl.BlockSpec((1, NSA_GROUPS, HEAD_DIM, cols), lambda bi: (bi, 0, 0, 0))
    kernel = functools.partial(_nsa_decode_kernel, nq=nq, nc=nc, ns=ns, q_pos0=q_pos0, win_pos0=win_pos0)
    return pl.pallas_call(
        kernel,
        grid=(b,),
        in_specs=[
            pl.BlockSpec((1, NSA_GROUPS, HEADS_PER_GROUP, nq, LANES), lambda bi: (bi, 0, 0, 0, 0)),
            pl.BlockSpec((1, nq, NSA_GROUPS * LANES), lambda bi: (bi, 0, 0)),
            per_b(ncp), per_b(ncp),
            pl.BlockSpec((nsp, ncp), lambda bi: (0, 0)),
            pl.BlockSpec((nsp, l_keys), lambda bi: (0, 0)),
            slabs, slabs, per_b(WIN_SPAN), per_b(WIN_SPAN),
        ],
        out_specs=pl.BlockSpec((1, nq, NSA_HEADS * HEAD_DIM), lambda bi: (bi, 0, 0)),
        out_shape=jax.ShapeDtypeStruct((b, nq, NSA_HEADS * HEAD_DIM), BF16),
        compiler_params=_params("parallel"),
        name="nsa_attend_decode",
    )(q, gates, kc_t, vc_t, ovt, ind, ksel_t, vsel_t, kwin_t, vwin_t)


def _gather_kernel(pt_ref, *refs, n_steps):
    page_refs = refs[:GATHER_PAGES]
    new_ref, perm_ref, cmp_ref, ksel_ref, vsel_ref = refs[GATHER_PAGES:]
    p = pl.program_id(1)
    page = page_refs[0].shape[2]
    k_lo = 2 * GROUP_LANES
    v_lo = 3 * GROUP_LANES

    @pl.when(p < n_steps - 1)
    def _():
        cmp_feats = []
        for k, page_ref in enumerate(page_refs):
            x = page_ref[0]
            cols = slice(k * page, (k + 1) * page)
            cmp_feats.append(x[0:2 * GROUP_LANES, :].astype(BF16))
            for g in range(NSA_GROUPS):
                ksel_ref[0, g, 0, :, cols] = x[k_lo + g * HEAD_DIM:k_lo + (g + 1) * HEAD_DIM, :].astype(BF16)
                vsel_ref[0, g, 0, :, cols] = x[v_lo + g * HEAD_DIM:v_lo + (g + 1) * HEAD_DIM, :].astype(BF16)
        feats = jnp.concatenate(cmp_feats, axis=1)
        toks = lax.dot_general(perm_ref[...], feats, NT_DIMS, preferred_element_type=F32).astype(BF16)
        n_chunks = toks.shape[0] // CMP_STRIDE
        for pos in range(CMP_STRIDE):
            cmp_ref[:, pos * 2 * GROUP_LANES:(pos + 1) * 2 * GROUP_LANES] = toks[pos * n_chunks:(pos + 1) * n_chunks, :]

    @pl.when(p == n_steps - 1)
    def _():
        new = new_ref[0]
        padded = jnp.concatenate([new, jnp.zeros((GATHER_PAGES * page - new.shape[0], new.shape[1]), F32)], axis=0)
        kt = jnp.transpose(padded[:, k_lo:k_lo + GROUP_LANES])
        vt = jnp.transpose(padded[:, v_lo:v_lo + GROUP_LANES])
        for g in range(NSA_GROUPS):
            ksel_ref[0, g, 0] = kt[g * HEAD_DIM:(g + 1) * HEAD_DIM, :].astype(BF16)
            vsel_ref[0, g, 0] = vt[g * HEAD_DIM:(g + 1) * HEAD_DIM, :].astype(BF16)


def gather_past(page_table, cache_t, new_rows):
    db, n_pages = page_table.shape
    page = cache_t.shape[2]
    dq = new_rows.shape[1]
    past = n_pages * page
    n_full = n_pages // GATHER_PAGES
    n_steps = n_full + 1
    step_keys = GATHER_PAGES * page
    chunk_lanes = CMP_STRIDE * 2 * GROUP_LANES
    n_chunks = step_keys // CMP_STRIDE
    out_row = np.arange(step_keys)
    perm = jnp.asarray((out_row % n_chunks * CMP_STRIDE + out_row // n_chunks)[:, None] == np.arange(step_keys)[None, :],
                       BF16)

    def page_spec(k):
        return pl.BlockSpec(
            (1, 4 * GROUP_LANES, page),
            lambda b, p, pt: (pt[b * n_pages + jnp.minimum(p, n_full - 1) * GATHER_PAGES + k], 0, 0))

    grid_spec = pltpu.PrefetchScalarGridSpec(
        num_scalar_prefetch=1,
        grid=(db, n_steps),
        in_specs=[page_spec(k) for k in range(GATHER_PAGES)] + [
            pl.BlockSpec((1, dq, 4 * GROUP_LANES), lambda b, p, pt: (b, 0, 0)),
            pl.BlockSpec((step_keys, step_keys), lambda b, p, pt: (0, 0))],
        out_specs=[
            pl.BlockSpec((step_keys // CMP_STRIDE, chunk_lanes),
                         lambda b, p, pt: (b * n_full + jnp.minimum(p, n_full - 1), 0)),
            pl.BlockSpec((1, NSA_GROUPS, 1, HEAD_DIM, step_keys), lambda b, p, pt: (b, 0, p, 0, 0)),
            pl.BlockSpec((1, NSA_GROUPS, 1, HEAD_DIM, step_keys), lambda b, p, pt: (b, 0, p, 0, 0)),
        ],
    )
    return pl.pallas_call(
        functools.partial(_gather_kernel, n_steps=n_steps),
        grid_spec=grid_spec,
        out_shape=[
            jax.ShapeDtypeStruct((db * past // CMP_STRIDE, chunk_lanes), BF16),
            jax.ShapeDtypeStruct((db, NSA_GROUPS, n_steps, HEAD_DIM, step_keys), BF16),
            jax.ShapeDtypeStruct((db, NSA_GROUPS, n_steps, HEAD_DIM, step_keys), BF16),
        ],
        compiler_params=_params("parallel", "arbitrary"),
        name="gather_past",
    )(page_table.reshape(-1), *([cache_t] * GATHER_PAGES), new_rows, perm)


def _win_assemble_kernel(cache_ref, new_ref, win_ref, kwin_ref, vwin_ref):
    old = cache_ref[0]
    new = new_ref[0]
    buf = old.shape[0]
    dq = new.shape[0]
    win_ref[0, 0:buf - dq, :] = old[dq:, :]
    win_ref[0, buf - dq:buf, :] = new
    old_t = jnp.transpose(old)
    tail = jnp.concatenate([new, jnp.zeros((WIN_SPAN - buf - dq, new.shape[1]), F32)], axis=0)
    tail_t = jnp.transpose(tail)
    for ref, off in ((kwin_ref, 0), (vwin_ref, GROUP_LANES)):
        for g in range(NSA_GROUPS):
            rows = slice(off + g * HEAD_DIM, off + (g + 1) * HEAD_DIM)
            ref[0, g] = jnp.concatenate([old_t[rows, :], tail_t[rows, :]], axis=1).astype(BF16)


def win_assemble(cache_win, new_win):
    db, buf, width = cache_win.shape
    dq = new_win.shape[1]
    return pl.pallas_call(
        _win_assemble_kernel,
        grid=(db,),
        in_specs=[
            pl.BlockSpec((1, buf, width), lambda b: (b, 0, 0)),
            pl.BlockSpec((1, dq, width), lambda b: (b, 0, 0)),
        ],
        out_specs=[
            pl.BlockSpec((1, buf, width), lambda b: (b, 0, 0)),
            pl.BlockSpec((1, NSA_GROUPS, HEAD_DIM, WIN_SPAN), lambda b: (b, 0, 0, 0)),
            pl.BlockSpec((1, NSA_GROUPS, HEAD_DIM, WIN_SPAN), lambda b: (b, 0, 0, 0)),
        ],
        out_shape=[
            jax.ShapeDtypeStruct((db, buf, width), F32),
            jax.ShapeDtypeStruct((db, NSA_GROUPS, HEAD_DIM, WIN_SPAN), BF16),
            jax.ShapeDtypeStruct((db, NSA_GROUPS, HEAD_DIM, WIN_SPAN), BF16),
        ],
        compiler_params=_params("parallel"),
        name="win_assemble",
    )(cache_win, new_win)


def _rope_angles(pos, half):
    inv = ROPE_THETA ** (-jnp.arange(half, dtype=F32) / half)
    ang = pos.astype(F32)[:, None] * inv[None, :]
    return jnp.cos(ang), jnp.sin(ang)


def _rope_tables_head64(pos):
    cos, sin = _rope_angles(pos, HEAD_DIM // 2)
    return jnp.concatenate([cos] * 4, axis=1), jnp.concatenate([-sin, sin] * 2, axis=1)


def _block_diag_groups(w):
    eye = jnp.eye(NSA_GROUPS, dtype=w.dtype)
    out = jnp.einsum("gh,...dn->...gdhn", eye, w)
    return out.reshape(*w.shape[:-2], GROUP_LANES, NSA_GROUPS * w.shape[-1])


def _compress_weights(cmp_w1, cmp_b1, cmp_w2, cmp_pe):
    r = CMP_BLOCK // CMP_STRIDE
    w1 = cmp_w1.reshape(2, r, CMP_STRIDE, HEAD_DIM, HEAD_DIM)
    bd = _block_diag_groups(w1)
    bd = jnp.concatenate([bd[:, 0], bd[:, 1]], axis=-1).astype(BF16)
    pe_rows = jnp.broadcast_to(cmp_pe.reshape(2, 1, CMP_BLOCK * HEAD_DIM), (2, 8, CMP_BLOCK * HEAD_DIM)).astype(BF16)
    w1_tiled = jnp.tile(cmp_w1, (1, 1, NSA_GROUPS)).astype(BF16)
    b1_tiled = jnp.tile(cmp_b1, (1, NSA_GROUPS)).reshape(2, 1, GROUP_LANES)
    w2_bd = _block_diag_groups(cmp_w2).astype(BF16)
    return bd[0], bd[1], pe_rows, w1_tiled, b1_tiled, w2_bd


def _overlap_table(nc, ncp, ns, nsp):
    ci = np.arange(ncp)[None, :]
    sj = np.arange(nsp)[:, None]
    overlap_t = ((ci * CMP_STRIDE < (sj + 1) * SEL_BLOCK) & (ci * CMP_STRIDE + CMP_BLOCK > sj * SEL_BLOCK)
                 & (ci < nc) & (sj < ns))
    return jnp.asarray(overlap_t, BF16)


def _block_membership(length, nsp):
    return jnp.asarray((np.arange(length)[None, :] // SEL_BLOCK) == np.arange(nsp)[:, None], BF16)


def _round_up(x, m):
    return -(-x // m) * m


def _trunk(x, pos, ret_s0, past, w):
    (ffn_norm, ffn_w_in, ffn_w_out, ret_norm, ret_w_in, ret_w_out, kv_norm, kv_w, k_norm_tiled,
     cmp_weights, nsa_norm, nsa_wq, nsa_wg, q_norm_tiled, nsa_w_out, ones_bd) = w
    b, t, _ = x.shape
    n = b * t
    prompt = past is None
    xf = x.reshape(n, D_MODEL)
    pos_rows = jnp.tile(pos, b) if not prompt else pos
    ret_cos, ret_sin = _rope_angles(pos_rows, RET_DK // 2)
    cos64, sin64 = _rope_tables_head64(pos)
    ret_states = []
    rows = win = attend = None
    for layer in range(DEPTH):
        if layer == N_A_LAYERS:
            xs = xf.reshape(b, t, D_MODEL)
            if prompt:
                rows, win, cmp_tok, ksel, vsel, kwin, vwin = kv_rows(
                    xs, kv_norm, kv_w, k_norm_tiled, cos64, sin64, ones_bd, aux=True)
                length = t
                new_win = win[:, t - min(WINDOW, t):]
            else:
                page_table, cache_t, cache_win = past
                rows, win = kv_rows(xs, kv_norm, kv_w, k_norm_tiled, cos64, sin64, ones_bd, aux=False)
                cmp_tok, ksel, vsel = gather_past(page_table, cache_t, rows)
                new_win, kwin, vwin = win_assemble(cache_win, win)
                past_len = page_table.shape[1] * cache_t.shape[2]
                length = past_len + t
            nc = (length - CMP_BLOCK) // CMP_STRIDE + 1
            n_chunk_rows = nc + CMP_BLOCK // CMP_STRIDE - 1
            ns = -(-length // SEL_BLOCK)
            wk_bd, wv_bd, pe_rows, w1_tiled, b1_tiled, w2_bd = cmp_weights
            assert cmp_tok.shape[0] == b * n_chunk_rows
            partial = cmp_partial(cmp_tok, wk_bd, wv_bd).reshape(b, n_chunk_rows, 4 * GROUP_LANES)
            c_end = jnp.arange(n_chunk_rows, dtype=I32) * CMP_STRIDE + (CMP_BLOCK - 1)
            cos_c, sin_c = _rope_tables_head64(c_end)
            kc_t, vc_t = cmp_combine(partial, pe_rows, w1_tiled, b1_tiled, w2_bd, k_norm_tiled, cos_c, sin_c, ones_bd)
            if prompt:
                assert ns <= MASK_ROWS and t % PROMPT_NQ == 0
                ovt = _overlap_table(nc, n_chunk_rows, ns, MASK_ROWS)
                attend = functools.partial(nsa_attend_prompt, kc_t=kc_t, vc_t=vc_t, ovt=ovt, ksel_t=ksel, vsel_t=vsel,
                                           kwin_t=kwin, vwin_t=vwin, nc=nc, ns=ns)
            else:
                nsp = _round_up(ns, 16)
                ovt = _overlap_table(nc, n_chunk_rows, ns, nsp)
                ind = _block_membership(ksel.shape[2] * ksel.shape[4], nsp)
                attend = functools.partial(nsa_attend_decode, kc_t=kc_t, vc_t=vc_t, ovt=ovt, ind=ind, ksel_t=ksel,
                                           vsel_t=vsel, kwin_t=kwin, vwin_t=vwin, nc=nc, ns=ns,
                                           q_pos0=int(past_len), win_pos0=int(past_len - cache_win.shape[1]))
        xf = ffn_half(xf, ffn_norm[layer, 0], ffn_w_in, ffn_w_out, layer, 0)
        if layer < N_A_LAYERS:
            qkvg = ret_inproj(xf, ret_norm[layer], ret_w_in, layer, ret_cos, ret_sin)
            gated, s_fin = ret_core(qkvg.reshape(b, t, 6 * D_MODEL), ret_s0, layer)
            ret_states.append(s_fin)
            mix, w_mix, mix_layer = gated.reshape(n, 2 * D_MODEL), ret_w_out, layer
        else:
            j = layer - N_A_LAYERS
            q, gates = nsa_q(xf.reshape(b, t, D_MODEL), nsa_norm[j], nsa_wq, nsa_wg, j, q_norm_tiled[j],
                             cos64, sin64, ones_bd)
            o = attend(q, gates)
            mix, w_mix, mix_layer = o.reshape(n, NSA_HEADS * HEAD_DIM), nsa_w_out, j
        xf = proj_ffn_half(mix, w_mix, mix_layer, xf, ffn_norm[layer, 1], ffn_w_in, ffn_w_out, layer, 1)
    return xf.reshape(b, t, D_MODEL), jnp.stack(ret_states), rows, new_win


def kernel(x_prompt, x_sample, state_ret, cache_kv, cache_win, page_table, ffn_norm, ffn_w_in, ffn_w_out, ret_norm,
           ret_w_in, ret_w_out, kv_norm, kv_w, k_norm, cmp_w1, cmp_b1, cmp_w2, cmp_pe, nsa_norm, nsa_w_in, q_norm,
           nsa_w_out):
    b, t, _ = x_prompt.shape
    db, dq, _ = x_sample.shape
    n_phys, page = cache_kv.shape[:2]
    past_len = page_table.shape[1] * page
    n_q_cols = NSA_HEADS * HEAD_DIM

    gate_w = nsa_w_in[:, :, n_q_cols:].reshape(-1, D_MODEL, NSA_GROUPS, N_GATES)
    gate_w = jnp.pad(gate_w, ((0, 0), (0, 0), (0, 0), (0, LANES - N_GATES))).reshape(-1, D_MODEL, NSA_GROUPS * LANES)
    eye = np.arange(GROUP_LANES)
    ones_bd = jnp.asarray((eye[:, None] // HEAD_DIM) == (eye[None, :] // HEAD_DIM), BF16)
    w = (ffn_norm, ffn_w_in.astype(BF16), ffn_w_out.astype(BF16), ret_norm, ret_w_in.astype(BF16),
         ret_w_out.astype(BF16), kv_norm, kv_w.astype(BF16), jnp.tile(k_norm, (1, NSA_GROUPS)),
         _compress_weights(cmp_w1, cmp_b1, cmp_w2, cmp_pe), nsa_norm, nsa_w_in[:, :, :n_q_cols].astype(BF16),
         gate_w.astype(BF16), jnp.tile(q_norm, (1, NSA_GROUPS)).reshape(-1, 1, GROUP_LANES),
         nsa_w_out.astype(BF16), ones_bd)

    pos_p = jnp.arange(t, dtype=I32)
    pos_s = past_len + jnp.arange(dq, dtype=I32)
    y_p, ret_p, rows_p, win_p = _trunk(x_prompt, pos_p, None, None, w)
    cache_t = jnp.transpose(cache_kv, (0, 2, 3, 4, 1)).reshape(n_phys, 4 * GROUP_LANES, page)
    cwin = cache_win.reshape(db, cache_win.shape[1], 2 * GROUP_LANES)
    y_s, ret_s, rows_s, win_s = _trunk(x_sample, pos_s, state_ret, (page_table, cache_t, cwin), w)
    kv_shape = (4, NSA_GROUPS, HEAD_DIM)
    win_shape = (2, NSA_GROUPS, HEAD_DIM)
    return (y_p, y_s, ret_p.astype(state_ret.dtype), ret_s.astype(state_ret.dtype),
            rows_p.reshape(b, t, *kv_shape), rows_s.reshape(db, dq, *kv_shape),
            win_p.reshape(b, win_p.shape[1], *win_shape), win_s.reshape(db, win_s.shape[1], *win_shape))
```
